```python
import math
import jax, jax.numpy as jnp
from jax import lax
import numpy as np

D_MODEL = 1024
BATCH = 4
SEQ = 8192
DEPTH = 2

CTX_LEN = 256
GRID_W = 64
RMS_EPS = 1e-6
POOL_WIDTH = 1024
POOL_WINDOWS = (2, 4, 8, 16)
POOL_GROUP = POOL_WIDTH // len(POOL_WINDOWS)
LRU_WIDTH = 1024
LRU_BLOCKS = 8
LRU_BLOCK_W = LRU_WIDTH // LRU_BLOCKS
CONV_W = 4
LRU_C = 8.0
MLA_HEADS = 8
Q_LORA = 384
KV_LORA = 256
QK_NOPE = 128
QK_ROPE = 64
V_DIM = 128
MLA_WIDTH = MLA_HEADS * V_DIM
MLA_SCALE = (QK_NOPE + QK_ROPE) ** -0.5
ROPE_FREQS = QK_ROPE // 4
ROPE_THETA = 10000.0
Q_BLOCK = 128
N_BRANCH = 3
IN_SPLITS = (POOL_WIDTH, LRU_WIDTH, LRU_WIDTH, Q_LORA, KV_LORA, QK_ROPE, N_BRANCH * D_MODEL)
IN_COLS = sum(IN_SPLITS)
D_FF = 2816
N_EXPERTS = 8
TOP_K = 2
EXPERT_FF = 3584
MOE_BLOCK = 256
N_DENSE = (DEPTH + 1) // 2
N_MOE = DEPTH // 2

kernel_name = "hybrid_pool_rglru_mla_moe_diffusion_trunk"


def rms_norm(x, g):
    xf = x.astype(jnp.float32)
    y = xf * lax.rsqrt(jnp.mean(xf * xf, axis=-1, keepdims=True) + RMS_EPS)
    return (y * g.astype(jnp.float32)).astype(x.dtype)


def modulate(h, shift, scale):
    return h * (1 + scale) + shift


def split_cols(z):
    offs = [int(o) for o in np.cumsum(IN_SPLITS)[:-1]]
    return jnp.split(z, offs, axis=-1)


def swiglu(h, w1, w3, w2):
    return (jax.nn.silu(h @ w1) * (h @ w3)) @ w2


def multiscale_pool(u):
    L = u.shape[1]
    cs = jnp.pad(jnp.cumsum(u.astype(jnp.float32), axis=1), ((0, 0), (1, 0), (0, 0)))
    t = jnp.arange(L)
    outs = []
    for g, w in enumerate(POOL_WINDOWS):
        lo = jnp.clip(t - w // 2, 0, L)
        hi = jnp.clip(t + w - w // 2, 0, L)
        cg = cs[..., g * POOL_GROUP:(g + 1) * POOL_GROUP]
        s = jnp.take(cg, hi, axis=1) - jnp.take(cg, lo, axis=1)
        outs.append(s / (hi - lo).astype(jnp.float32)[None, :, None])
    return jnp.concatenate(outs, axis=-1).astype(u.dtype) - u


def pool_branch(u, pool_w, pool_scale, pool_proj):
    B, L, _ = u.shape
    m = multiscale_pool(u).reshape(B, L, len(POOL_WINDOWS), POOL_GROUP)
    m = jnp.einsum('blgi,gij->blgj', m, pool_w).reshape(B, L, POOL_WIDTH) * pool_scale
    return m @ pool_proj


def short_conv(u, w, b):
    L = u.shape[1]
    left = CONV_W // 2
    up = jnp.pad(u, ((0, 0), (left, CONV_W - 1 - left), (0, 0)))
    out = b
    for k in range(CONV_W):
        out = out + w[k] * up[:, k:k + L]
    return out


def block_diag(u, w, b):
    B, L, _ = u.shape
    y = jnp.einsum('blnj,njk->blnk', u.reshape(B, L, LRU_BLOCKS, LRU_BLOCK_W), w)
    return y.reshape(B, L, LRU_WIDTH) + b


def rglru_coeffs(u, wa, ba, wx, bx, lam):
    r = jax.nn.sigmoid(block_diag(u, wa, ba).astype(jnp.float32))
    i = jax.nn.sigmoid(block_diag(u, wx, bx).astype(jnp.float32))
    log_a = -LRU_C * r * jax.nn.softplus(-lam.astype(jnp.float32))
    a = jnp.exp(log_a)
    b = jnp.sqrt(-jnp.expm1(2.0 * log_a)) * (i * u.astype(jnp.float32))
    return a, b


def linear_scan(a, b, h0, reverse):
    def combine(l, r):
        return l[0] * r[0], r[0] * l[1] + r[1]
    a_cum, b_cum = lax.associative_scan(combine, (a, b), reverse=reverse, axis=1)
    return a_cum * h0[:, None, :] + b_cum


def axial_rope_tables(n_tokens):
    rows = n_tokens // GRID_W
    row = jnp.repeat(jnp.arange(rows, dtype=jnp.float32), GRID_W)
    col = (jnp.arange(rows * GRID_W) % GRID_W).astype(jnp.float32)
    inv = ROPE_THETA ** (-jnp.arange(ROPE_FREQS, dtype=jnp.float32) / ROPE_FREQS)
    ang = jnp.stack([row[:, None] * inv, col[:, None] * inv], axis=1)
    return jnp.cos(ang), jnp.sin(ang)


def apply_axial_rope(x, cos, sin):
    xs = x.reshape(x.shape[:-1] + (2, 2, ROPE_FREQS))
    x1, x2 = xs[..., 0, :], xs[..., 1, :]
    out = jnp.stack([x1 * cos - x2 * sin, x2 * cos + x1 * sin], axis=-2)
    return out.reshape(x.shape).astype(x.dtype)


def mla_query(cq, q_norm_g, w_uq):
    B, L, _ = cq.shape
    return (rms_norm(cq, q_norm_g) @ w_uq).reshape(B, L, MLA_HEADS, QK_NOPE + QK_ROPE)


def mla_keys_values(ckv, k_rope, kv_norm_g, w_ukv):
    B, L, _ = ckv.shape
    kv = (rms_norm(ckv, kv_norm_g) @ w_ukv).reshape(B, L, MLA_HEADS, QK_NOPE + V_DIM)
    k_r = jnp.broadcast_to(k_rope[:, :, None, :], (B, L, MLA_HEADS, QK_ROPE))
    return jnp.concatenate([kv[..., :QK_NOPE], k_r], axis=-1), kv[..., QK_NOPE:]


def attend(q, k, v):
    s = jnp.einsum('bqhd,bkhd->bhqk', q, k).astype(jnp.float32) * MLA_SCALE
    p = jax.nn.softmax(s, axis=-1).astype(v.dtype)
    return jnp.einsum('bhqk,bkhd->bqhd', p, v)


def attend_blocks(q, k, v):
    B, L, H, Dk = q.shape
    nb = L // Q_BLOCK
    qb = q.reshape(B, nb, Q_BLOCK, H, Dk).transpose(1, 0, 2, 3, 4)
    ob = lax.map(lambda qi: attend(qi, k, v), qb)
    return ob.transpose(1, 0, 2, 3, 4).reshape(B, L, H, V_DIM)


def merge_branches(gt, ys, w_out):
    g = jnp.split(jax.nn.sigmoid(gt.astype(jnp.float32)).astype(gt.dtype), N_BRANCH, axis=-1)
    return (g[0] * ys[0] + g[1] * ys[1] + g[2] * ys[2]) @ w_out


def token_mixer(h_ctx, h_lat, cos, sin, w_in, pool_w, pool_scale, pool_proj, conv_w, conv_b,
                gate_a_w, gate_a_b, gate_x_w, gate_x_b, lru_lambda, lru_proj,
                q_norm_g, w_uq, kv_norm_g, w_ukv, mla_proj, w_out, need_ctx):
    B, L, _ = h_lat.shape
    pool_c, lx_c, lg_c, cq_c, ckv_c, kr_c, gt_c = split_cols(h_ctx @ w_in)
    pool_l, lx_l, lg_l, cq_l, ckv_l, kr_l, gt_l = split_cols(h_lat @ w_in)

    u_c = short_conv(lx_c, conv_w, conv_b)
    u_l = short_conv(lx_l, conv_w, conv_b)
    ctx_dirs, lat_dirs = [], []
    for d in range(2):
        rev = d == 1
        a, b = rglru_coeffs(u_c, gate_a_w[d], gate_a_b[d], gate_x_w[d], gate_x_b[d], lru_lambda[d])
        hc = linear_scan(a, b, jnp.zeros_like(b[:, 0]), rev)
        ctx_dirs.append(hc)
        h0 = hc[:, 0] if rev else hc[:, -1]
        a, b = rglru_coeffs(u_l, gate_a_w[d], gate_a_b[d], gate_x_w[d], gate_x_b[d], lru_lambda[d])
        lat_dirs.append(linear_scan(a, b, h0, rev))
    y_lru_l = ((lat_dirs[0] + lat_dirs[1]).astype(lx_l.dtype) * jax.nn.gelu(lg_l)) @ lru_proj

    k_c, v_c = mla_keys_values(ckv_c, kr_c, kv_norm_g, w_ukv)
    k_l, v_l = mla_keys_values(ckv_l, apply_axial_rope(kr_l, cos, sin), kv_norm_g, w_ukv)
    q_l = mla_query(cq_l, q_norm_g, w_uq)
    q_l = jnp.concatenate([q_l[..., :QK_NOPE],
                           apply_axial_rope(q_l[..., QK_NOPE:], cos[:, None], sin[:, None])], axis=-1)
    k_all = jnp.concatenate([k_c, k_l], axis=1)
    v_all = jnp.concatenate([v_c, v_l], axis=1)
    y_mla_l = attend_blocks(q_l, k_all, v_all).reshape(B, L, MLA_WIDTH) @ mla_proj

    y_pool_l = pool_branch(pool_l, pool_w, pool_scale, pool_proj)
    y_lat = merge_branches(gt_l, (y_pool_l, y_lru_l, y_mla_l), w_out)
    if not need_ctx:
        return None, y_lat

    Bc, Lc, _ = h_ctx.shape
    y_pool_c = pool_branch(pool_c, pool_w, pool_scale, pool_proj)
    y_lru_c = ((ctx_dirs[0] + ctx_dirs[1]).astype(lx_c.dtype) * jax.nn.gelu(lg_c)) @ lru_proj
    q_c = mla_query(cq_c, q_norm_g, w_uq)
    y_mla_c = attend(q_c, k_c, v_c).reshape(Bc, Lc, MLA_WIDTH) @ mla_proj
    y_ctx = merge_branches(gt_c, (y_pool_c, y_lru_c, y_mla_c), w_out)
    return y_ctx, y_lat


def moe_swiglu(h, router_w, w1, w3, w2):
    n_tok = h.shape[0]
    n_assign = n_tok * TOP_K
    logits = (h @ router_w).astype(jnp.float32)
    top_v, top_i = lax.top_k(logits, TOP_K)
    gates = jax.nn.softmax(top_v, axis=-1)
    e_flat = top_i.reshape(-1)
    tok_flat = jnp.repeat(jnp.arange(n_tok), TOP_K)
    order = jnp.argsort(e_flat)
    e_s, tok_s, g_s = e_flat[order], tok_flat[order], gates.reshape(-1)[order]
    counts = jnp.bincount(e_flat, length=N_EXPERTS)
    starts = jnp.cumsum(counts) - counts
    padded = (counts + MOE_BLOCK - 1) // MOE_BLOCK * MOE_BLOCK
    pad_end = jnp.cumsum(padded)
    pad_start = pad_end - padded
    dest = pad_start[e_s] + jnp.arange(n_assign) - starts[e_s]
    n_rows = -(-n_assign // MOE_BLOCK) * MOE_BLOCK + N_EXPERTS * MOE_BLOCK
    n_blocks = n_rows // MOE_BLOCK
    buf = jnp.zeros((n_rows, h.shape[1]), h.dtype).at[dest].set(h[tok_s])
    blk_e = jnp.minimum(jnp.searchsorted(pad_end, jnp.arange(n_blocks) * MOE_BLOCK, side='right'),
                        N_EXPERTS - 1)

    def expert_block(args):
        xb, e = args
        return swiglu(xb, w1[e], w3[e], w2[e])

    y_blk = lax.map(expert_block, (buf.reshape(n_blocks, MOE_BLOCK, h.shape[1]), blk_e))
    y_rows = y_blk.reshape(n_rows, h.shape[1])[dest]
    return jnp.zeros_like(h).at[tok_s].add(g_s[:, None].astype(h.dtype) * y_rows)


def setup_inputs(seed: int = 0) -> dict:
    key = jax.random.key(seed)
    ks = iter(jax.random.split(key, 48))

    def nrm(shape, scale):
        return scale * jax.random.normal(next(ks), shape, jnp.float32)

    D = D_MODEL
    u = jax.random.uniform(next(ks), (DEPTH, 2, LRU_WIDTH), jnp.float32, 0.9, 0.999)
    return {
        "x": nrm((BATCH, SEQ, D), 1.0),
        "c": nrm((BATCH, D), 1.0),
        "ctx": nrm((BATCH, CTX_LEN, D), 1.0),
        "c_ctx": nrm((D,), 1.0),
        "mod_w": nrm((DEPTH, D, 6 * D), 0.5 * D ** -0.5),
        "mod_b": nrm((DEPTH, 6 * D), 0.02),
        "pre_mix_g": 1.0 + nrm((DEPTH, D), 0.05),
        "post_mix_g": 1.0 + nrm((DEPTH, D), 0.05),
        "pre_ffn_g": 1.0 + nrm((DEPTH, D), 0.05),
        "post_ffn_g": 1.0 + nrm((DEPTH, D), 0.05),
        "w_in": nrm((DEPTH, D, IN_COLS), D ** -0.5),
        "pool_w": nrm((DEPTH, len(POOL_WINDOWS), POOL_GROUP, POOL_GROUP), POOL_GROUP ** -0.5),
        "pool_scale": 1.0 + nrm((DEPTH, POOL_WIDTH), 0.1),
        "pool_proj": nrm((DEPTH, POOL_WIDTH, D), POOL_WIDTH ** -0.5),
        "conv_w": nrm((DEPTH, CONV_W, LRU_WIDTH), CONV_W ** -0.5),
        "conv_b": nrm((DEPTH, LRU_WIDTH), 0.01),
        "gate_a_w": nrm((DEPTH, 2, LRU_BLOCKS, LRU_BLOCK_W, LRU_BLOCK_W), LRU_BLOCK_W ** -0.5),
        "gate_a_b": nrm((DEPTH, 2, LRU_WIDTH), 0.1),
        "gate_x_w": nrm((DEPTH, 2, LRU_BLOCKS, LRU_BLOCK_W, LRU_BLOCK_W), LRU_BLOCK_W ** -0.5),
        "gate_x_b": nrm((DEPTH, 2, LRU_WIDTH), 0.1),
        "lru_lambda": jnp.log(u) - jnp.log1p(-u),
        "lru_proj": nrm((DEPTH, LRU_WIDTH, D), LRU_WIDTH ** -0.5),
        "q_norm_g": 1.0 + nrm((DEPTH, Q_LORA), 0.05),
        "w_uq": nrm((DEPTH, Q_LORA, MLA_HEADS * (QK_NOPE + QK_ROPE)), Q_LORA ** -0.5),
        "kv_norm_g": 1.0 + nrm((DEPTH, KV_LORA), 0.05),
        "w_ukv": nrm((DEPTH, KV_LORA, MLA_HEADS * (QK_NOPE + V_DIM)), KV_LORA ** -0.5),
        "mla_proj": nrm((DEPTH, MLA_WIDTH, D), MLA_WIDTH ** -0.5),
        "w_out": nrm((DEPTH, D, D), D ** -0.5),
        "ffn_w1": nrm((N_DENSE, D, D_FF), D ** -0.5),
        "ffn_w3": nrm((N_DENSE, D, D_FF), D ** -0.5),
        "ffn_w2": nrm((N_DENSE, D_FF, D), D_FF ** -0.5),
        "router_w": nrm((N_MOE, D, N_EXPERTS), D ** -0.5),
        "moe_w1": nrm((N_MOE, N_EXPERTS, D, EXPERT_FF), D ** -0.5),
        "moe_w3": nrm((N_MOE, N_EXPERTS, D, EXPERT_FF), D ** -0.5),
        "moe_w2": nrm((N_MOE, N_EXPERTS, EXPERT_FF, D), EXPERT_FF ** -0.5),
    }


def reference(x, c, ctx, c_ctx, mod_w, mod_b, pre_mix_g, post_mix_g, pre_ffn_g, post_ffn_g, w_in,
              pool_w, pool_scale, pool_proj, conv_w, conv_b, gate_a_w, gate_a_b, gate_x_w, gate_x_b,
              lru_lambda, lru_proj, q_norm_g, w_uq, kv_norm_g, w_ukv, mla_proj, w_out,
              ffn_w1, ffn_w3, ffn_w2, router_w, moe_w1, moe_w3, moe_w2):
    cos, sin = axial_rope_tables(x.shape[1])
    x_lat, x_ctx = x, ctx
    for l in range(DEPTH):
        last = l == DEPTH - 1
        m_lat = jnp.split((jax.nn.silu(c) @ mod_w[l] + mod_b[l])[:, None, :], 6, axis=-1)
        m_ctx = jnp.split(jax.nn.silu(c_ctx) @ mod_w[l] + mod_b[l], 6, axis=-1)

        h_lat = modulate(rms_norm(x_lat, pre_mix_g[l]), m_lat[0], m_lat[1])
        h_ctx = modulate(rms_norm(x_ctx, pre_mix_g[l]), m_ctx[0], m_ctx[1])
        y_ctx, y_lat = token_mixer(h_ctx, h_lat, cos, sin, w_in[l], pool_w[l], pool_scale[l], pool_proj[l],
                                   conv_w[l], conv_b[l], gate_a_w[l], gate_a_b[l], gate_x_w[l], gate_x_b[l],
                                   lru_lambda[l], lru_proj[l], q_norm_g[l], w_uq[l], kv_norm_g[l], w_ukv[l],
                                   mla_proj[l], w_out[l], not last)
        x_lat = x_lat + m_lat[2] * rms_norm(y_lat, post_mix_g[l])

        h2_lat = modulate(rms_norm(x_lat, pre_ffn_g[l]), m_lat[3], m_lat[4])
        n_lat = h2_lat.shape[0] * h2_lat.shape[1]
        if last:
            tokens = h2_lat.reshape(n_lat, D_MODEL)
        else:
            x_ctx = x_ctx + m_ctx[2] * rms_norm(y_ctx, post_mix_g[l])
            h2_ctx = modulate(rms_norm(x_ctx, pre_ffn_g[l]), m_ctx[3], m_ctx[4])
            tokens = jnp.concatenate([h2_lat.reshape(n_lat, D_MODEL), h2_ctx.reshape(-1, D_MODEL)], axis=0)
        if l % 2 == 0:
            f = swiglu(tokens, ffn_w1[l // 2], ffn_w3[l // 2], ffn_w2[l // 2])
        else:
            f = moe_swiglu(tokens, router_w[l // 2], moe_w1[l // 2], moe_w3[l // 2], moe_w2[l // 2])
        x_lat = x_lat + m_lat[5] * rms_norm(f[:n_lat].reshape(x_lat.shape), post_ffn_g[l])
        if not last:
            x_ctx = x_ctx + m_ctx[5] * rms_norm(f[n_lat:].reshape(x_ctx.shape), post_ffn_g[l])
    return x_lat
```

```python
import functools
import math

import numpy as np
import jax
import jax.numpy as jnp
from jax import lax
from jax.experimental import pallas as pl
from jax.experimental.pallas import tpu as pltpu

BF = jnp.bfloat16
F32 = jnp.float32

RMS_EPS = 1e-6
GRID_W = 64
POOL_WINDOWS = (2, 4, 8, 16)
LRU_BLOCKS = 8
CONV_W = 4
LRU_C = 8.0
HEADS = 8
Q_LORA = 384
KV_LORA = 256
QK_NOPE = 128
QK_ROPE = 64
V_DIM = 128
MLA_SCALE = (QK_NOPE + QK_ROPE) ** -0.5
ROPE_FREQS = QK_ROPE // 4
ROPE_THETA = 10000.0
N_EXPERTS = 8
TOP_K = 2

LANE = 128
HALO = 16
SEQ_TILE = 256
HEAD_W = 256
VMEM_LIMIT = 48 * 1024 * 1024


def _cparams(*sem):
    return pltpu.CompilerParams(dimension_semantics=sem, vmem_limit_bytes=VMEM_LIMIT)


def _rms(x, g):
    ms = jnp.mean(x * x, axis=-1, keepdims=True)
    return x * lax.rsqrt(ms + RMS_EPS) * g


def _sigmoid(x):
    return 1.0 / (1.0 + jnp.exp(-x))


def _silu(x):
    return x * _sigmoid(x)


def _gelu_tanh(x):
    return 0.5 * x * (1.0 + jnp.tanh(math.sqrt(2.0 / math.pi) * (x + 0.044715 * (x * x * x))))


def _dot(a, b):
    return jnp.dot(a, b, preferred_element_type=F32)


def _mod_kernel(c_ref, w_ref, b_ref, o_ref):
    s = _silu(c_ref[...])
    o_ref[...] = _dot(s.astype(BF), w_ref[...].astype(BF)) + b_ref[...]


def _mod_call(cc, mod_w, mod_b):
    depth, d, n6 = mod_w.shape
    tn = 1536
    return pl.pallas_call(
        _mod_kernel,
        grid=(depth, n6 // tn),
        in_specs=[pl.BlockSpec((8, d), lambda l, j: (0, 0)),
                  pl.BlockSpec((None, d, tn), lambda l, j: (l, 0, j)),
                  pl.BlockSpec((None, 1, tn), lambda l, j: (l, 0, j))],
        out_specs=pl.BlockSpec((None, 8, tn), lambda l, j: (l, 0, j)),
        out_shape=jax.ShapeDtypeStruct((depth, 8, n6), F32),
        compiler_params=_cparams("parallel", "arbitrary"),
        name="mod",
    )(cc, mod_w, mod_b.reshape(depth, 1, n6))


def _inproj_kernel(x_ref, mod_ref, g_ref, w_ref, z_ref, h_scr):
    @pl.when(pl.program_id(1) == 0)
    def _():
        h = _rms(x_ref[...], g_ref[...])
        h = h * (1.0 + mod_ref[1:2, :]) + mod_ref[0:1, :]
        h_scr[...] = h.astype(BF)

    z_ref[...] = _dot(h_scr[...], w_ref[...]).astype(BF)


def _inproj_call(x, mod, g, w, tm, mod_idx):
    m, d = x.shape
    n = w.shape[1]
    tn = 768
    return pl.pallas_call(
        _inproj_kernel,
        grid=(m // tm, n // tn),
        in_specs=[pl.BlockSpec((tm, d), lambda i, j: (i, 0)),
                  pl.BlockSpec((None, 6, d), lambda i, j: (mod_idx(i), 0, 0)),
                  pl.BlockSpec((1, d), lambda i, j: (0, 0)),
                  pl.BlockSpec((d, tn), lambda i, j: (0, j))],
        out_specs=pl.BlockSpec((tm, tn), lambda i, j: (i, j)),
        out_shape=jax.ShapeDtypeStruct((m, n), BF),
        scratch_shapes=[pltpu.VMEM((tm, d), BF)],
        compiler_params=_cparams("parallel", "arbitrary"),
        name="inproj",
    )(x, mod, g, w)


def _rope(x, cos, sin):
    return x * cos + pltpu.roll(x, LANE // 2, 1) * sin


def _qkv_kernel(z_ref, cos_ref, sin_ref, qg_ref, kvg_ref, wuq_ref, wukv_ref, q_ref, k_ref, v_ref):
    z = z_ref[...]
    cq = z[:, :Q_LORA].astype(F32)
    ckv = z[:, Q_LORA:Q_LORA + KV_LORA].astype(F32)
    kr = z[:, Q_LORA + KV_LORA:].astype(F32)
    cos = cos_ref[...]
    sin = sin_ref[...]
    q = _dot(_rms(cq, qg_ref[...]).astype(BF), wuq_ref[...])
    kv = _dot(_rms(ckv, kvg_ref[...]).astype(BF), wukv_ref[...])
    krot = _rope(kr, cos, sin).astype(BF)
    for h in range(HEADS):
        c0 = h * HEAD_W
        q_ref[:, c0:c0 + QK_NOPE] = (q[:, c0:c0 + QK_NOPE] * MLA_SCALE).astype(BF)
        q_ref[:, c0 + QK_NOPE:c0 + HEAD_W] = (_rope(q[:, c0 + QK_NOPE:c0 + HEAD_W], cos, sin) * MLA_SCALE).astype(BF)
        k_ref[h, :, 0:QK_NOPE] = kv[:, c0:c0 + QK_NOPE].astype(BF)
        k_ref[h, :, QK_NOPE:HEAD_W] = krot
        v_ref[h] = kv[:, c0 + QK_NOPE:c0 + HEAD_W].astype(BF)


def _qkv_call(z, cos_t, sin_t, qg, kvg, wuq, wukv, bsz, seq, ctx_len):
    m = z.shape[0]
    ts = SEQ_TILE
    nt = seq // ts
    nlat = bsz * nt
    lk = seq + ctx_len
    zw = Q_LORA + KV_LORA + LANE
    zcol = (z.shape[1] - zw) // zw

    def tab_idx(i):
        return (jnp.where(i < nlat, i % nt, nt), 0)

    def kv_idx(i):
        return (jnp.where(i < nlat, i // nt, i - nlat), 0, jnp.where(i < nlat, i % nt, nt), 0)

    return pl.pallas_call(
        _qkv_kernel,
        grid=(m // ts,),
        in_specs=[pl.BlockSpec((ts, zw), lambda i: (i, zcol)),
                  pl.BlockSpec((ts, LANE), tab_idx),
                  pl.BlockSpec((ts, LANE), tab_idx),
                  pl.BlockSpec((1, Q_LORA), lambda i: (0, 0)),
                  pl.BlockSpec((1, KV_LORA), lambda i: (0, 0)),
                  pl.BlockSpec(wuq.shape, lambda i: (0, 0)),
                  pl.BlockSpec(wukv.shape, lambda i: (0, 0))],
        out_specs=[pl.BlockSpec((ts, HEADS * HEAD_W), lambda i: (i, 0)),
                   pl.BlockSpec((None, HEADS, ts, HEAD_W), kv_idx),
                   pl.BlockSpec((None, HEADS, ts, V_DIM), kv_idx)],
        out_shape=[jax.ShapeDtypeStruct((m, HEADS * HEAD_W), BF),
                   jax.ShapeDtypeStruct((bsz, HEADS, lk, HEAD_W), BF),
                   jax.ShapeDtypeStruct((bsz, HEADS, lk, V_DIM), BF)],
        compiler_params=_cparams("parallel"),
        name="qkv_up",
    )(z, cos_t, sin_t, qg, kvg, wuq, wukv)


def _attn_kernel(q_ref, k_ref, v_ref, *rest, tk, nk):
    o_ref = rest[-1]
    q = q_ref[...]
    tq = q.shape[0]

    def body(j, carry):
        m, l, acc = carry
        r0 = pl.multiple_of(j * tk, tk)
        ks = k_ref[pl.ds(r0, tk), :]
        vs = v_ref[pl.ds(r0, tk), :]
        s = lax.dot_general(q, ks, (((1,), (1,)), ((), ())), preferred_element_type=F32)
        m_new = jnp.maximum(m, jnp.max(s, axis=1, keepdims=True))
        alpha = jnp.exp(m - m_new)
        p = jnp.exp(s - m_new)
        l = alpha * l + jnp.sum(p, axis=1, keepdims=True)
        acc = alpha * acc + _dot(p.astype(BF), vs)
        return m_new, l, acc

    init = (jnp.full((tq, 1), -1e30, F32), jnp.zeros((tq, 1), F32), jnp.zeros((tq, V_DIM), F32))
    m, l, acc = lax.fori_loop(0, nk, body, init)
    o_ref[...] = (acc / l).astype(o_ref.dtype)


def _attn_call(q, k, v, o_prev, *, bsz, row_blk0, nq, tq, kblk, klen, tk):
    m = q.shape[0]
    kern = functools.partial(_attn_kernel, tk=tk, nk=klen // tk)
    in_specs = [pl.BlockSpec((tq, HEAD_W), lambda b, h, i: (row_blk0 + b * nq + i, h)),
                pl.BlockSpec((None, None, klen, HEAD_W), lambda b, h, i: (b, h, kblk, 0)),
                pl.BlockSpec((None, None, klen, V_DIM), lambda b, h, i: (b, h, kblk, 0))]
    args = [q, k, v]
    aliases = {}
    if o_prev is not None:
        in_specs.append(pl.BlockSpec(memory_space=pl.ANY))
        args.append(o_prev)
        aliases = {3: 0}
    return pl.pallas_call(
        kern,
        grid=(bsz, HEADS, nq),
        in_specs=in_specs,
        out_specs=pl.BlockSpec((tq, V_DIM), lambda b, h, i: (row_blk0 + b * nq + i, h)),
        out_shape=jax.ShapeDtypeStruct((m, HEADS * V_DIM), BF),
        input_output_aliases=aliases,
        compiler_params=_cparams("parallel", "parallel", "arbitrary"),
        name="attn",
    )(*args)


def _seq_flags(i, nlat, nt):
    is_ctx = i >= nlat
    t = jnp.where(is_ctx, 0, i % nt)
    first = jnp.logical_or(is_ctx, t == 0)
    last = jnp.logical_or(is_ctx, t == nt - 1)
    return is_ctx, t, first, last


def _pool_kernel(x_ref, xp_ref, xn_ref, pw_ref, ps_ref, o_ref, *, nlat, nt, seq, ctx_len):
    ts = x_ref.shape[0]
    is_ctx, t, first, last = _seq_flags(pl.program_id(0), nlat, nt)
    seq_len = jnp.where(is_ctx, ctx_len, seq)
    x = x_ref[...]
    xp = jnp.where(first, jnp.zeros_like(xp_ref[...]), xp_ref[...])
    xn = jnp.where(last, jnp.zeros_like(xn_ref[...]), xn_ref[...])
    xe = jnp.concatenate([xp, x, xn], axis=0)
    tpos = t * ts + lax.broadcasted_iota(jnp.int32, (ts, 1), 0)
    rel = (lax.broadcasted_iota(jnp.int32, (ts, ts + 2 * HALO), 1) - HALO
           - lax.broadcasted_iota(jnp.int32, (ts, ts + 2 * HALO), 0))
    gw = x.shape[1] // len(POOL_WINDOWS)
    for g, w in enumerate(POOL_WINDOWS):
        cs = slice(g * gw, (g + 1) * gw)
        band = jnp.where(rel >= -(w // 2), jnp.where(rel < w - w // 2, 1.0, 0.0), 0.0).astype(BF)
        s = _dot(band, xe[:, cs])
        cnt = (jnp.minimum(tpos + (w - w // 2), seq_len) - jnp.maximum(tpos - w // 2, 0)).astype(F32)
        mean_minus = s / cnt - x[:, cs].astype(F32)
        o_ref[:, cs] = (_dot(mean_minus.astype(BF), pw_ref[g]) * ps_ref[:, cs]).astype(BF)


def _halo_specs(ts, width, col, row_blk, m):
    r = ts // HALO
    nh = m // HALO
    prev = pl.BlockSpec((HALO, width), lambda *a: (jnp.maximum(row_blk(*a) * r - 1, 0), col))
    nxt = pl.BlockSpec((HALO, width), lambda *a: (jnp.minimum((row_blk(*a) + 1) * r, nh - 1), col))
    return prev, nxt


def _pool_call(z, pool_w, pool_scale, bsz, seq, ctx_len, n_rows):
    m = z.shape[0]
    ts = SEQ_TILE
    nt = seq // ts
    nlat = bsz * nt
    width = pool_scale.shape[1]
    prev, nxt = _halo_specs(ts, width, 0, lambda i: i, m)
    kern = functools.partial(_pool_kernel, nlat=nlat, nt=nt, seq=seq, ctx_len=ctx_len)
    return pl.pallas_call(
        kern,
        grid=(n_rows // ts,),
        in_specs=[pl.BlockSpec((ts, width), lambda i: (i, 0)), prev, nxt,
                  pl.BlockSpec(pool_w.shape, lambda i: (0, 0, 0)),
                  pl.BlockSpec((1, width), lambda i: (0, 0))],
        out_specs=pl.BlockSpec((ts, width), lambda i: (i, 0)),
        out_shape=jax.ShapeDtypeStruct((m, width), BF),
        compiler_params=_cparams("parallel"),
        name="pool",
    )(z, z, z, pool_w, pool_scale)


def _lru_kernel(x_ref, xp_ref, xn_ref, cw_ref, cb_ref, wa_ref, ba_ref, wx_ref, bx_ref, lam_ref, o_ref,
                xe_scr, a_scr, b_scr, h_scr, *, reverse, nt):
    ts, width = x_ref.shape
    s = pl.program_id(1)
    t = (nt - s) if reverse else (s - 1)
    first = jnp.logical_or(s == 0, t == 0)
    last = jnp.logical_or(s == 0, t == nt - 1)

    @pl.when(s == 0)
    def _():
        h_scr[...] = jnp.zeros_like(h_scr)

    xp = xp_ref[...].astype(F32)
    xn = xn_ref[...].astype(F32)
    xe_scr[0:HALO, :] = jnp.where(first, jnp.zeros_like(xp), xp)
    xe_scr[HALO:HALO + ts, :] = x_ref[...].astype(F32)
    xe_scr[HALO + ts:, :] = jnp.where(last, jnp.zeros_like(xn), xn)
    left = CONV_W // 2
    u = cb_ref[...] + cw_ref[0:1, :] * xe_scr[pl.ds(HALO - left, ts), :]
    for k in range(1, CONV_W):
        u = u + cw_ref[k:k + 1, :] * xe_scr[pl.ds(HALO - left + k, ts), :]

    lam = lam_ref[...]
    neg = -lam
    softplus = jnp.maximum(neg, 0.0) + jnp.log1p(jnp.exp(-jnp.abs(neg)))
    bw = width // LRU_BLOCKS
    for n in range(LRU_BLOCKS):
        cs = slice(n * bw, (n + 1) * bw)
        un = u[:, cs]
        ub = un.astype(BF)
        r = _sigmoid(_dot(ub, wa_ref[n]) + ba_ref[:, cs])
        gi = _sigmoid(_dot(ub, wx_ref[n]) + bx_ref[:, cs])
        log_a = -LRU_C * r * softplus[:, cs]
        a = jnp.exp(log_a)
        a_scr[:, cs] = a
        b_scr[:, cs] = jnp.sqrt(1.0 - a * a) * (gi * un)

    ng = ts // 8
    row = lax.broadcasted_iota(jnp.int32, (8, width), 0)

    def group(gi_, h):
        g = (ng - 1 - gi_) if reverse else gi_
        r0 = pl.multiple_of(g * 8, 8)
        a = a_scr[pl.ds(r0, 8), :]
        b = b_scr[pl.ds(r0, 8), :]
        for sft in (1, 2, 4):
            if reverse:
                a_sh = pltpu.roll(a, 8 - sft, 0)
                b_sh = pltpu.roll(b, 8 - sft, 0)
                valid = row < 8 - sft
            else:
                a_sh = pltpu.roll(a, sft, 0)
                b_sh = pltpu.roll(b, sft, 0)
                valid = row >= sft
            b = jnp.where(valid, a * b_sh + b, b)
            a = jnp.where(valid, a * a_sh, a)
        hrows = a * h + b
        b_scr[pl.ds(r0, 8), :] = hrows
        edge = hrows[0:1, :] if reverse else hrows[7:8, :]
        return jnp.broadcast_to(edge, (8, width))

    h_scr[...] = lax.fori_loop(0, ng, group, h_scr[...])
    o_ref[...] = b_scr[...].astype(o_ref.dtype)


def _lru_call(z, conv_w, conv_b, wa, ba, wx, bx, lam, bsz, seq, reverse):
    m = z.shape[0]
    ts = SEQ_TILE
    nt = seq // ts
    nlat = bsz * nt
    width = conv_b.shape[1]

    def row_blk(b, s):
        t = (nt - s) if reverse else (s - 1)
        return jnp.where(s == 0, nlat + b, b * nt + t)

    prev, nxt = _halo_specs(ts, width, 1, row_blk, m)
    vec = lambda shape: pl.BlockSpec(shape, lambda b, s: (0,) * len(shape))
    kern = functools.partial(_lru_kernel, reverse=reverse, nt=nt)
    return pl.pallas_call(
        kern,
        grid=(bsz, nt + 1),
        in_specs=[pl.BlockSpec((ts, width), lambda b, s: (row_blk(b, s), 1)), prev, nxt,
                  vec(conv_w.shape), vec(conv_b.shape), vec(wa.shape), vec(ba.shape),
                  vec(wx.shape), vec(bx.shape), vec(lam.shape)],
        out_specs=pl.BlockSpec((ts, width), lambda b, s: (row_blk(b, s), 0)),
        out_shape=jax.ShapeDtypeStruct((m, width), BF),
        scratch_shapes=[pltpu.VMEM((ts + 2 * HALO, width), F32), pltpu.VMEM((ts, width), F32),
                        pltpu.VMEM((ts, width), F32), pltpu.VMEM((8, width), F32)],
        compiler_params=_cparams("parallel", "arbitrary"),
        name="lru_bwd" if reverse else "lru_fwd",
    )(z, z, z, conv_w, conv_b, wa, ba, wx, bx, lam)


def _merge_kernel(x_ref, mod_ref, mp_ref, hf_ref, hb_ref, lg_ref, o_ref, gt_ref, wp_ref, wl_ref, wm_ref, wo_ref,
                  g1_ref, g2_ref, *rest, route):
    if route:
        rw_ref, x1_ref, h2_ref, rt_ref = rest
    else:
        x1_ref, h2_ref = rest
    d = x_ref.shape[1]
    y_pool = _dot(mp_ref[...], wp_ref[...])
    lru_in = (hf_ref[...].astype(F32) + hb_ref[...].astype(F32)) * _gelu_tanh(lg_ref[...].astype(F32))
    y_lru = _dot(lru_in.astype(BF), wl_ref[...])
    y_mla = _dot(o_ref[...], wm_ref[...])
    mix = (_sigmoid(gt_ref[:, 0:d].astype(F32)) * y_pool
           + _sigmoid(gt_ref[:, d:2 * d].astype(F32)) * y_lru
           + _sigmoid(gt_ref[:, 2 * d:3 * d].astype(F32)) * y_mla)
    y = _dot(mix.astype(BF), wo_ref[...])
    x1 = x_ref[...] + mod_ref[2:3, :] * _rms(y, g1_ref[...])
    x1_ref[...] = x1
    h2 = _rms(x1, g2_ref[...]) * (1.0 + mod_ref[4:5, :]) + mod_ref[3:4, :]
    h2_ref[...] = h2.astype(BF)
    if route:
        logits = jnp.dot(h2, rw_ref[...], preferred_element_type=F32, precision=lax.Precision.HIGHEST)
        col = lax.broadcasted_iota(jnp.int32, logits.shape, 1).astype(F32)
        logits = jnp.where(col < N_EXPERTS, logits, -jnp.inf)
        v1 = jnp.max(logits, axis=1, keepdims=True)
        i1 = jnp.min(jnp.where(logits == v1, col, float(LANE)), axis=1, keepdims=True)
        rest_l = jnp.where(col == i1, -jnp.inf, logits)
        v2 = jnp.max(rest_l, axis=1, keepdims=True)
        i2 = jnp.min(jnp.where(rest_l == v2, col, float(LANE)), axis=1, keepdims=True)
        e = jnp.exp(v2 - v1)
        gate1 = 1.0 / (1.0 + e)
        gate2 = e / (1.0 + e)
        rt_ref[...] = jnp.where(col == 0, i1, jnp.where(col == 1, i2,
                                jnp.where(col == 2, gate1, jnp.where(col == 3, gate2, 0.0))))


def _merge_call(x, mod, mp, hf, hb, z, o, wp, wl, wm, wo, g1, g2, rw, tm, n_rows, mod_idx):
    d = x.shape[1]
    route = rw is not None
    row = lambda c: pl.BlockSpec((tm, d), lambda i: (i, c))
    full = lambda a: pl.BlockSpec(a.shape, lambda i: (0,) * a.ndim)
    in_specs = [row(0), pl.BlockSpec((None, 6, d), lambda i: (mod_idx(i), 0, 0)),
                row(0), row(0), row(0), row(2), row(0), pl.BlockSpec((tm, 3 * d), lambda i: (i, 1)),
                full(wp), full(wl), full(wm), full(wo), full(g1), full(g2)]
    args = [x, mod, mp, hf, hb, z, o, z, wp, wl, wm, wo, g1, g2]
    out_specs = [row(0), row(0)]
    out_shape = [jax.ShapeDtypeStruct((n_rows, d), F32), jax.ShapeDtypeStruct((n_rows, d), BF)]
    if route:
        in_specs.append(full(rw))
        args.append(rw)
        out_specs.append(pl.BlockSpec((tm, LANE), lambda i: (i, 0)))
        out_shape.append(jax.ShapeDtypeStruct((n_rows, LANE), F32))
    return pl.pallas_call(
        functools.partial(_merge_kernel, route=route),
        grid=(n_rows // tm,),
        in_specs=in_specs, out_specs=out_specs, out_shape=out_shape,
        compiler_params=_cparams("parallel"),
        name="merge",
    )(*args)


def _swiglu_step(x, w1, w3, w2):
    a = _dot(x, w1)
    b = _dot(x, w3)
    return _dot((_silu(a) * b).astype(BF), w2)


def _ffn_kernel(h_ref, x_ref, mod_ref, g_ref, w1_ref, w3_ref, w2_ref, o_ref, acc):
    f = pl.program_id(1)

    @pl.when(f == 0)
    def _():
        acc[...] = jnp.zeros_like(acc)

    acc[...] += _swiglu_step(h_ref[...], w1_ref[...].astype(BF), w3_ref[...].astype(BF), w2_ref[...].astype(BF))

    @pl.when(f == pl.num_programs(1) - 1)
    def _():
        o_ref[...] = x_ref[...] + mod_ref[5:6, :] * _rms(acc[...], g_ref[...])


def _ffn_call(h2, x1, mod, g, w1, w3, w2, tm, mod_idx):
    m, d = x1.shape
    ff = w1.shape[1]
    tf = 256
    return pl.pallas_call(
        _ffn_kernel,
        grid=(m // tm, ff // tf),
        in_specs=[pl.BlockSpec((tm, d), lambda i, f: (i, 0)),
                  pl.BlockSpec((tm, d), lambda i, f: (i, 0)),
                  pl.BlockSpec((None, 6, d), lambda i, f: (mod_idx(i), 0, 0)),
                  pl.BlockSpec((1, d), lambda i, f: (0, 0)),
                  pl.BlockSpec((d, tf), lambda i, f: (0, f)),
                  pl.BlockSpec((d, tf), lambda i, f: (0, f)),
                  pl.BlockSpec((tf, d), lambda i, f: (f, 0))],
        out_specs=pl.BlockSpec((tm, d), lambda i, f: (i, 0)),
        out_shape=jax.ShapeDtypeStruct((m, d), F32),
        scratch_shapes=[pltpu.VMEM((tm, d), F32)],
        compiler_params=_cparams("parallel", "arbitrary"),
        name="ffn",
    )(h2, x1, mod, g, w1, w3, w2)


def _moe_kernel(blk_e_ref, nused_ref, x_ref, w1_ref, w3_ref, w2_ref, o_ref, acc):
    i = pl.program_id(0)
    f = pl.program_id(1)
    last = f == pl.num_programs(1) - 1
    used = i < nused_ref[0]

    @pl.when(jnp.logical_and(used, f == 0))
    def _():
        acc[...] = jnp.zeros_like(acc)

    @pl.when(used)
    def _():
        acc[...] += _swiglu_step(x_ref[...], w1_ref[...].astype(BF), w3_ref[...].astype(BF),
                                 w2_ref[...].astype(BF))

    @pl.when(jnp.logical_and(used, last))
    def _():
        o_ref[...] = acc[...].astype(o_ref.dtype)

    @pl.when(jnp.logical_and(jnp.logical_not(used), last))
    def _():
        o_ref[...] = jnp.zeros_like(o_ref)


def _moe_call(blk_e, nused, xs, w1, w3, w2, tme):
    n_rows, d = xs.shape
    ff = w1.shape[2]
    tf = 512
    return pl.pallas_call(
        _moe_kernel,
        grid_spec=pltpu.PrefetchScalarGridSpec(
            num_scalar_prefetch=2,
            grid=(n_rows // tme, ff // tf),
            in_specs=[pl.BlockSpec((tme, d), lambda i, f, be, nu: (i, 0)),
                      pl.BlockSpec((None, d, tf), lambda i, f, be, nu: (be[i], 0, f)),
                      pl.BlockSpec((None, d, tf), lambda i, f, be, nu: (be[i], 0, f)),
                      pl.BlockSpec((None, tf, d), lambda i, f, be, nu: (be[i], f, 0))],
            out_specs=pl.BlockSpec((tme, d), lambda i, f, be, nu: (i, 0)),
            scratch_shapes=[pltpu.VMEM((tme, d), F32)]),
        out_shape=jax.ShapeDtypeStruct((n_rows, d), BF),
        compiler_params=_cparams("arbitrary", "arbitrary"),
        name="moe",
    )(blk_e, nused, xs, w1, w3, w2)


def _combine_kernel(y_ref, rt_ref, x_ref, mod_ref, g_ref, o_ref):
    d = x_ref.shape[1]
    g1 = rt_ref[:, 2:3]
    g2 = rt_ref[:, 3:4]
    f = g1 * y_ref[:, 0:d].astype(F32) + g2 * y_ref[:, d:2 * d].astype(F32)
    o_ref[...] = x_ref[...] + mod_ref[5:6, :] * _rms(f, g_ref[...])


def _combine_call(y2, rt, x1, mod, g, tm, mod_idx):
    n, d = x1.shape
    return pl.pallas_call(
        _combine_kernel,
        grid=(n // tm,),
        in_specs=[pl.BlockSpec((tm, 2 * d), lambda i: (i, 0)),
                  pl.BlockSpec((tm, LANE), lambda i: (i, 0)),
                  pl.BlockSpec((tm, d), lambda i: (i, 0)),
                  pl.BlockSpec((None, 6, d), lambda i: (mod_idx(i), 0, 0)),
                  pl.BlockSpec((1, d), lambda i: (0, 0))],
        out_specs=pl.BlockSpec((tm, d), lambda i: (i, 0)),
        out_shape=jax.ShapeDtypeStruct((n, d), F32),
        compiler_params=_cparams("parallel"),
        name="combine",
    )(y2, rt, x1, mod, g)


def _route_plan(rt, tme):
    n = rt.shape[0]
    e_flat = rt[:, :TOP_K].astype(jnp.int32).reshape(-1)
    onehot = (e_flat[:, None] == jnp.arange(N_EXPERTS, dtype=jnp.int32)[None, :]).astype(jnp.int32)
    csum = jnp.cumsum(onehot, axis=0)
    rank = jnp.sum(onehot * csum, axis=1) - 1
    counts = csum[-1]
    padded = (counts + tme - 1) // tme * tme
    pad_end = jnp.cumsum(padded)
    pad_start = pad_end - padded
    dest = pad_start[e_flat] + rank
    n_rows = n * TOP_K + N_EXPERTS * tme
    n_blk = n_rows // tme
    src_tok = jnp.zeros((n_rows,), jnp.int32).at[dest].set(jnp.arange(n * TOP_K, dtype=jnp.int32) // TOP_K)
    blk_e = jnp.minimum(jnp.searchsorted(pad_end, jnp.arange(n_blk, dtype=jnp.int32) * tme, side="right"),
                        N_EXPERTS - 1).astype(jnp.int32)
    nused = (pad_end[-1] // tme).astype(jnp.int32).reshape(1)
    return dest, src_tok, blk_e, nused


def _swap_perm():
    j = np.arange(QK_ROPE)
    axis, half, f = j // (2 * ROPE_FREQS), (j % (2 * ROPE_FREQS)) // ROPE_FREQS, j % ROPE_FREQS
    return axis * 2 * ROPE_FREQS + (1 - half) * ROPE_FREQS + f


def _rope_tables(seq, ctx_len):
    rows = seq // GRID_W
    row = jnp.repeat(jnp.arange(rows, dtype=F32), GRID_W)
    col = (jnp.arange(rows * GRID_W) % GRID_W).astype(F32)
    inv = ROPE_THETA ** (-jnp.arange(ROPE_FREQS, dtype=F32) / ROPE_FREQS)
    ang = jnp.stack([row[:, None] * inv, col[:, None] * inv], axis=1)
    cos, sin = jnp.cos(ang), jnp.sin(ang)
    cos64 = jnp.stack([cos, cos], axis=2).reshape(seq, QK_ROPE)
    sin64 = jnp.stack([-sin, sin], axis=2).reshape(seq, QK_ROPE)
    pad = jnp.zeros((seq, LANE - QK_ROPE), F32)
    cos_t = jnp.concatenate([cos64, pad], axis=1)
    sin_t = jnp.concatenate([sin64, pad], axis=1)
    ident = jnp.concatenate([jnp.ones((ctx_len, QK_ROPE), F32), jnp.zeros((ctx_len, LANE - QK_ROPE), F32)], axis=1)
    return jnp.concatenate([cos_t, ident], axis=0), jnp.concatenate([sin_t, jnp.zeros((ctx_len, LANE), F32)], axis=0)


def _prep_w_in(w):
    d = w.shape[0]
    pw = lw = d
    o = np.cumsum([0, pw, lw, lw, Q_LORA, KV_LORA, QK_ROPE, 3 * d])
    pool, lx, lg, cq, ckv, kr, gt = (w[:, o[i]:o[i + 1]] for i in range(7))
    return jnp.concatenate([pool, lx, lg, gt, cq, ckv, kr, kr[:, _swap_perm()]], axis=1).astype(BF)


def _prep_w_uq(w):
    qk = QK_NOPE + QK_ROPE
    w = w.reshape(w.shape[0], HEADS, qk)
    rope = w[:, :, QK_NOPE:]
    return jnp.concatenate([w, rope[:, :, _swap_perm()]], axis=2).reshape(w.shape[0], HEADS * HEAD_W).astype(BF)


def kernel(x, c, ctx, c_ctx, mod_w, mod_b, pre_mix_g, post_mix_g, pre_ffn_g, post_ffn_g, w_in, pool_w, pool_scale,
           pool_proj, conv_w, conv_b, gate_a_w, gate_a_b, gate_x_w, gate_x_b, lru_lambda, lru_proj, q_norm_g, w_uq,
           kv_norm_g, w_ukv, mla_proj, w_out, ffn_w1, ffn_w3, ffn_w2, router_w, moe_w1, moe_w3, moe_w2):
    bsz, seq, d = x.shape
    ctx_len = ctx.shape[1]
    depth = mod_w.shape[0]
    assert ctx_len == SEQ_TILE and seq % SEQ_TILE == 0 and seq % GRID_W == 0
    n_lat = bsz * seq
    n_ctx = bsz * ctx_len
    n_all = n_lat + n_ctx
    tm = min(1024, seq, n_ctx)
    assert seq % tm == 0 and n_ctx % tm == 0
    tm_merge = tm // 2

    def mod_idx_for(rows):
        return lambda i: jnp.where(i < n_lat // rows, i // (seq // rows), bsz)

    mod_idx = mod_idx_for(tm)

    cc = jnp.concatenate([c, c_ctx[None, :], jnp.zeros((8 - bsz - 1, d), F32)], axis=0)
    mods = _mod_call(cc, mod_w, mod_b)[:, :bsz + 1].reshape(depth, bsz + 1, 6, d)
    cos_t, sin_t = _rope_tables(seq, ctx_len)
    x_all = jnp.concatenate([x.reshape(n_lat, d), ctx.reshape(n_ctx, d)], axis=0)
    row1 = lambda v: v.reshape(1, -1)

    for l in range(depth):
        last = l == depth - 1
        n_out = n_lat if last else n_all
        mod = mods[l]
        z = _inproj_call(x_all, mod, row1(pre_mix_g[l]), _prep_w_in(w_in[l]), tm, mod_idx)
        q, k, v = _qkv_call(z, cos_t, sin_t, row1(q_norm_g[l]), row1(kv_norm_g[l]), _prep_w_uq(w_uq[l]),
                            w_ukv[l].astype(BF), bsz, seq, ctx_len)
        tq = min(256, seq)
        tk = 768 if (seq + ctx_len) % 768 == 0 else SEQ_TILE
        o = _attn_call(q, k, v, None, bsz=bsz, row_blk0=0, nq=seq // tq, tq=tq, kblk=0, klen=seq + ctx_len, tk=tk)
        if not last:
            o = _attn_call(q, k, v, o, bsz=bsz, row_blk0=n_lat // ctx_len, nq=1, tq=ctx_len,
                           kblk=seq // ctx_len, klen=ctx_len, tk=ctx_len)
        mp = _pool_call(z, pool_w[l].astype(BF), row1(pool_scale[l]), bsz, seq, ctx_len, n_out)
        hs = []
        for dr in range(2):
            hs.append(_lru_call(z, conv_w[l], row1(conv_b[l]), gate_a_w[l, dr].astype(BF), row1(gate_a_b[l, dr]),
                                gate_x_w[l, dr].astype(BF), row1(gate_x_b[l, dr]), row1(lru_lambda[l, dr]),
                                bsz, seq, dr == 1))
        moe_layer = l % 2 == 1
        rw = None
        if moe_layer:
            rw = jnp.concatenate([router_w[l // 2], jnp.zeros((d, LANE - N_EXPERTS), F32)], axis=1)
        outs = _merge_call(x_all, mod, mp, hs[0], hs[1], z, o, pool_proj[l].astype(BF), lru_proj[l].astype(BF),
                           mla_proj[l].astype(BF), w_out[l].astype(BF), row1(post_mix_g[l]), row1(pre_ffn_g[l]),
                           rw, tm_merge, n_out, mod_idx_for(tm_merge))
        if not moe_layer:
            x1, h2 = outs
            x_all = _ffn_call(h2, x1, mod, row1(post_ffn_g[l]), ffn_w1[l // 2], ffn_w3[l // 2], ffn_w2[l // 2],
                              tm, mod_idx)
        else:
            x1, h2, rt = outs
            tme = min(1024, n_out * TOP_K // N_EXPERTS)
            dest, src_tok, blk_e, nused = _route_plan(rt, tme)
            xs = jnp.take(h2, src_tok, axis=0)
            y = _moe_call(blk_e, nused, xs, moe_w1[l // 2], moe_w3[l // 2], moe_w2[l // 2], tme)
            y2 = jnp.take(y, dest, axis=0).reshape(n_out, TOP_K * d)
            x_all = _combine_call(y2, rt, x1, mod, row1(post_ffn_g[l]), tm, mod_idx)
    return x_all[:n_lat].reshape(bsz, seq, d)
```

```python
import functools
import math

import numpy as np
import jax
import jax.numpy as jnp
from jax import lax
from jax.experimental import pallas as pl
from jax.experimental.pallas import tpu as pltpu

BF = jnp.bfloat16
F32 = jnp.float32

RMS_EPS = 1e-6
GRID_W = 64
POOL_WINDOWS = (2, 4, 8, 16)
LRU_BLOCKS = 8
CONV_W = 4
LRU_C = 8.0
HEADS = 8
Q_LORA = 384
KV_LORA = 256
QK_NOPE = 128
QK_ROPE = 64
V_DIM = 128
MLA_SCALE = (QK_NOPE + QK_ROPE) ** -0.5
Q_SCALE = MLA_SCALE * math.log2(math.e)
ROPE_FREQS = QK_ROPE // 4
ROPE_THETA = 10000.0
N_EXPERTS = 8
TOP_K = 2

LANE = 128
HALO = 16
SEQ_TILE = 256
HEAD_W = 256
VMEM_LIMIT = 48 * 1024 * 1024


def _cparams(*sem):
    return pltpu.CompilerParams(dimension_semantics=sem, vmem_limit_bytes=VMEM_LIMIT)


def _rms(x, g):
    ms = jnp.mean(x * x, axis=-1, keepdims=True)
    return x * lax.rsqrt(ms + RMS_EPS) * g


def _sigmoid(x):
    return 1.0 / (1.0 + jnp.exp(-x))


def _silu(x):
    return x * _sigmoid(x)


def _gelu_tanh(x):
    return 0.5 * x * (1.0 + jnp.tanh(math.sqrt(2.0 / math.pi) * (x + 0.044715 * (x * x * x))))


def _dot(a, b):
    return jnp.dot(a, b, preferred_element_type=F32)


def _mod_kernel(c_ref, w_ref, b_ref, o_ref):
    s = _silu(c_ref[...])
    o_ref[...] = _dot(s.astype(BF), w_ref[...].astype(BF)) + b_ref[...]


def _mod_call(cc, mod_w, mod_b):
    depth, d, n6 = mod_w.shape
    tn = 1536
    return pl.pallas_call(
        _mod_kernel,
        grid=(depth, n6 // tn),
        in_specs=[pl.BlockSpec((8, d), lambda l, j: (0, 0)),
                  pl.BlockSpec((None, d, tn), lambda l, j: (l, 0, j)),
                  pl.BlockSpec((None, 1, tn), lambda l, j: (l, 0, j))],
        out_specs=pl.BlockSpec((None, 8, tn), lambda l, j: (l, 0, j)),
        out_shape=jax.ShapeDtypeStruct((depth, 8, n6), F32),
        compiler_params=_cparams("parallel", "arbitrary"),
        name="mod",
    )(cc, mod_w, mod_b.reshape(depth, 1, n6))


def _inproj_kernel(x_ref, mod_ref, g_ref, w_ref, z_ref, h_scr):
    @pl.when(pl.program_id(1) == 0)
    def _():
        h = _rms(x_ref[...], g_ref[...])
        h = h * (1.0 + mod_ref[1:2, :]) + mod_ref[0:1, :]
        h_scr[...] = h.astype(BF)

    z_ref[...] = _dot(h_scr[...], w_ref[...]).astype(BF)


def _inproj_call(x, mod, g, w, tm, mod_idx):
    m, d = x.shape
    n = w.shape[1]
    tn = 768
    return pl.pallas_call(
        _inproj_kernel,
        grid=(m // tm, n // tn),
        in_specs=[pl.BlockSpec((tm, d), lambda i, j: (i, 0)),
                  pl.BlockSpec((None, 6, d), lambda i, j: (mod_idx(i), 0, 0)),
                  pl.BlockSpec((1, d), lambda i, j: (0, 0)),
                  pl.BlockSpec((d, tn), lambda i, j: (0, j))],
        out_specs=pl.BlockSpec((tm, tn), lambda i, j: (i, j)),
        out_shape=jax.ShapeDtypeStruct((m, n), BF),
        scratch_shapes=[pltpu.VMEM((tm, d), BF)],
        compiler_params=_cparams("parallel", "arbitrary"),
        name="inproj",
    )(x, mod, g, w)


def _rope(x, cos, sin):
    return x * cos + pltpu.roll(x, LANE // 2, 1) * sin


_NT = (((1,), (1,)), ((), ()))


def _qkv_kernel(z_ref, cos_ref, sin_ref, cost_ref, sint_ref, qg_ref, kvg_ref, wuqt_ref, wk_ref, wvt_ref,
                qt_ref, k_ref, vt_ref):
    z = z_ref[...]
    cq = z[:, :Q_LORA].astype(F32)
    ckv = z[:, Q_LORA:Q_LORA + KV_LORA].astype(F32)
    kr = z[:, Q_LORA + KV_LORA:].astype(F32)
    cqn = _rms(cq, qg_ref[...]).astype(BF)
    ckvn = _rms(ckv, kvg_ref[...]).astype(BF)
    qt = lax.dot_general(wuqt_ref[...], cqn, _NT, preferred_element_type=F32)
    kn = _dot(ckvn, wk_ref[...])
    vt = lax.dot_general(wvt_ref[...], ckvn, _NT, preferred_element_type=F32)
    krot = _rope(kr, cos_ref[...], sin_ref[...]).astype(BF)
    cost = cost_ref[...]
    sint = sint_ref[...]
    for h in range(HEADS):
        c0 = h * HEAD_W
        c1 = c0 + QK_NOPE
        c2 = c1 + QK_ROPE
        qt_ref[c0:c1, :] = (qt[c0:c1, :] * Q_SCALE).astype(BF)
        qt_ref[c1:c2, :] = ((qt[c1:c2, :] * cost + qt[c2:c0 + HEAD_W, :] * sint) * Q_SCALE).astype(BF)
        qt_ref[c2:c0 + HEAD_W, :] = jnp.zeros((QK_ROPE, qt.shape[1]), BF)
        k_ref[h, :, 0:QK_NOPE] = kn[:, h * QK_NOPE:(h + 1) * QK_NOPE].astype(BF)
        k_ref[h, :, QK_NOPE:HEAD_W] = krot
        vt_ref[h] = vt[h * V_DIM:(h + 1) * V_DIM, :].astype(BF)


def _qkv_call(z, cos_t, sin_t, qg, kvg, wuqt, wk, wvt, bsz, seq, ctx_len):
    m = z.shape[0]
    cos_tt = cos_t[:, :QK_ROPE].T
    sin_tt = sin_t[:, :QK_ROPE].T
    ts = SEQ_TILE
    nt = seq // ts
    nlat = bsz * nt
    lk = seq + ctx_len
    zw = Q_LORA + KV_LORA + LANE
    zcol = (z.shape[1] - zw) // zw

    def tab_idx(i):
        return (jnp.where(i < nlat, i % nt, nt), 0)

    def kv_idx(i):
        return (jnp.where(i < nlat, i // nt, i - nlat), 0, jnp.where(i < nlat, i % nt, nt), 0)

    def vt_idx(i):
        return (jnp.where(i < nlat, i // nt, i - nlat), 0, 0, jnp.where(i < nlat, i % nt, nt))

    return pl.pallas_call(
        _qkv_kernel,
        grid=(m // ts,),
        in_specs=[pl.BlockSpec((ts, zw), lambda i: (i, zcol)),
                  pl.BlockSpec((ts, LANE), tab_idx),
                  pl.BlockSpec((ts, LANE), tab_idx),
                  pl.BlockSpec((QK_ROPE, ts), lambda i: tab_idx(i)[::-1]),
                  pl.BlockSpec((QK_ROPE, ts), lambda i: tab_idx(i)[::-1]),
                  pl.BlockSpec((1, Q_LORA), lambda i: (0, 0)),
                  pl.BlockSpec((1, KV_LORA), lambda i: (0, 0)),
                  pl.BlockSpec(wuqt.shape, lambda i: (0, 0)),
                  pl.BlockSpec(wk.shape, lambda i: (0, 0)),
                  pl.BlockSpec(wvt.shape, lambda i: (0, 0))],
        out_specs=[pl.BlockSpec((HEADS * HEAD_W, ts), lambda i: (0, i)),
                   pl.BlockSpec((None, HEADS, ts, HEAD_W), kv_idx),
                   pl.BlockSpec((None, HEADS, V_DIM, ts), vt_idx)],
        out_shape=[jax.ShapeDtypeStruct((HEADS * HEAD_W, m), BF),
                   jax.ShapeDtypeStruct((bsz, HEADS, lk, HEAD_W), BF),
                   jax.ShapeDtypeStruct((bsz, HEADS, V_DIM, lk), BF)],
        compiler_params=_cparams("parallel"),
        name="qkv_up",
    )(z, cos_t, sin_t, cos_tt, sin_tt, qg, kvg, wuqt, wk, wvt)


def _col_reduce(x, pair, red):
    n = x.shape[0] // 4
    a = pair(pair(x[0:n], x[n:2 * n]), pair(x[2 * n:3 * n], x[3 * n:4 * n]))
    return red(a, axis=0, keepdims=True)


def _attn_kernel(qt_ref, k_ref, vt_ref, *rest, kp, g, nk):
    o_ref, s_scr, p_scr = rest[-3:]
    qt = qt_ref[...]
    tq = qt.shape[1]

    def scores(c, r):
        r0 = (c * g + r) * kp
        s = _dot(k_ref[r0:r0 + kp, :], qt)
        s_scr[c % 2, r * kp:(r + 1) * kp, :] = s
        return _col_reduce(s, jnp.maximum, jnp.max)

    cur = [scores(0, r) for r in range(g)]
    m = jnp.full((1, tq), -1e30, F32)
    l = jnp.zeros((1, tq), F32)
    acc = jnp.zeros((V_DIM, tq), F32)
    for c in range(nk):
        m_new = m
        for mx in cur:
            m_new = jnp.maximum(m_new, mx)
        alpha = jnp.exp2(m - m_new)
        nxt = []
        lsum = None
        for r in range(g):
            if c + 1 < nk:
                nxt.append(scores(c + 1, r))
            p = jnp.exp2(s_scr[c % 2, r * kp:(r + 1) * kp, :] - m_new)
            part = _col_reduce(p, jnp.add, jnp.sum)
            lsum = part if lsum is None else lsum + part
            p_scr[c % 2, r * kp:(r + 1) * kp, :] = p.astype(BF)
        acc = alpha * acc + _dot(vt_ref[:, c * g * kp:(c + 1) * g * kp], p_scr[c % 2])
        l = alpha * l + lsum
        m = m_new
        cur = nxt
    o_ref[...] = (acc / l).T.astype(o_ref.dtype)


def _attn_call(q, k, v, o_prev, *, bsz, row_blk0, nq, tq, kblk, klen, tk, kp):
    m = q.shape[1]
    kern = functools.partial(_attn_kernel, kp=kp, g=tk // kp, nk=klen // tk)
    in_specs = [pl.BlockSpec((HEAD_W, tq), lambda b, h, i: (h, row_blk0 + b * nq + i)),
                pl.BlockSpec((None, None, klen, HEAD_W), lambda b, h, i: (b, h, kblk, 0)),
                pl.BlockSpec((None, None, V_DIM, klen), lambda b, h, i: (b, h, 0, kblk))]
    args = [q, k, v]
    aliases = {}
    if o_prev is not None:
        in_specs.append(pl.BlockSpec(memory_space=pl.ANY))
        args.append(o_prev)
        aliases = {3: 0}
    return pl.pallas_call(
        kern,
        grid=(bsz, HEADS, nq),
        in_specs=in_specs,
        out_specs=pl.BlockSpec((tq, V_DIM), lambda b, h, i: (row_blk0 + b * nq + i, h)),
        out_shape=jax.ShapeDtypeStruct((m, HEADS * V_DIM), BF),
        scratch_shapes=[pltpu.VMEM((2, tk, tq), F32), pltpu.VMEM((2, tk, tq), BF)],
        input_output_aliases=aliases,
        compiler_params=_cparams("parallel", "parallel", "arbitrary"),
        name="attn",
    )(*args)


def _seq_flags(i, nlat, nt):
    is_ctx = i >= nlat
    t = jnp.where(is_ctx, 0, i % nt)
    first = jnp.logical_or(is_ctx, t == 0)
    last = jnp.logical_or(is_ctx, t == nt - 1)
    return is_ctx, t, first, last


def _pool_kernel(x_ref, xp_ref, xn_ref, pw_ref, ps_ref, o_ref, *, nlat, nt, seq, ctx_len):
    ts = x_ref.shape[0]
    is_ctx, t, first, last = _seq_flags(pl.program_id(0), nlat, nt)
    seq_len = jnp.where(is_ctx, ctx_len, seq)
    x = x_ref[...]
    xp = jnp.where(first, jnp.zeros_like(xp_ref[...]), xp_ref[...])
    xn = jnp.where(last, jnp.zeros_like(xn_ref[...]), xn_ref[...])
    xe = jnp.concatenate([xp, x, xn], axis=0)
    tpos = t * ts + lax.broadcasted_iota(jnp.int32, (ts, 1), 0)
    rel = (lax.broadcasted_iota(jnp.int32, (ts, ts + 2 * HALO), 1) - HALO
           - lax.broadcasted_iota(jnp.int32, (ts, ts + 2 * HALO), 0))
    gw = x.shape[1] // len(POOL_WINDOWS)
    for g, w in enumerate(POOL_WINDOWS):
        cs = slice(g * gw, (g + 1) * gw)
        band = jnp.where(rel >= -(w // 2), jnp.where(rel < w - w // 2, 1.0, 0.0), 0.0).astype(BF)
        s = _dot(band, xe[:, cs])
        cnt = (jnp.minimum(tpos + (w - w // 2), seq_len) - jnp.maximum(tpos - w // 2, 0)).astype(F32)
        mean_minus = s / cnt - x[:, cs].astype(F32)
        o_ref[:, cs] = (_dot(mean_minus.astype(BF), pw_ref[g]) * ps_ref[:, cs]).astype(BF)


def _halo_specs(ts, width, col, row_blk, m):
    r = ts // HALO
    nh = m // HALO
    prev = pl.BlockSpec((HALO, width), lambda *a: (jnp.maximum(row_blk(*a) * r - 1, 0), col))
    nxt = pl.BlockSpec((HALO, width), lambda *a: (jnp.minimum((row_blk(*a) + 1) * r, nh - 1), col))
    return prev, nxt


def _pool_call(z, pool_w, pool_scale, bsz, seq, ctx_len, n_rows):
    m = z.shape[0]
    ts = SEQ_TILE
    nt = seq // ts
    nlat = bsz * nt
    width = pool_scale.shape[1]
    prev, nxt = _halo_specs(ts, width, 0, lambda i: i, m)
    kern = functools.partial(_pool_kernel, nlat=nlat, nt=nt, seq=seq, ctx_len=ctx_len)
    return pl.pallas_call(
        kern,
        grid=(n_rows // ts,),
        in_specs=[pl.BlockSpec((ts, width), lambda i: (i, 0)), prev, nxt,
                  pl.BlockSpec(pool_w.shape, lambda i: (0, 0, 0)),
                  pl.BlockSpec((1, width), lambda i: (0, 0))],
        out_specs=pl.BlockSpec((ts, width), lambda i: (i, 0)),
        out_shape=jax.ShapeDtypeStruct((m, width), BF),
        compiler_params=_cparams("parallel"),
        name="pool",
    )(z, z, z, pool_w, pool_scale)


def _lru_kernel(x_ref, xp_ref, xn_ref, cw_ref, cb_ref, wa_ref, ba_ref, wx_ref, bx_ref, lam_ref, o_ref,
                xe_scr, a_scr, b_scr, h_scr, *, reverse, nt):
    ts, width = x_ref.shape
    s = pl.program_id(1)
    t = (nt - s) if reverse else (s - 1)
    first = jnp.logical_or(s == 0, t == 0)
    last = jnp.logical_or(s == 0, t == nt - 1)

    @pl.when(s == 0)
    def _():
        h_scr[...] = jnp.zeros_like(h_scr)

    xp = xp_ref[...].astype(F32)
    xn = xn_ref[...].astype(F32)
    xe_scr[0:HALO, :] = jnp.where(first, jnp.zeros_like(xp), xp)
    xe_scr[HALO:HALO + ts, :] = x_ref[...].astype(F32)
    xe_scr[HALO + ts:, :] = jnp.where(last, jnp.zeros_like(xn), xn)
    left = CONV_W // 2
    u = cb_ref[...] + cw_ref[0:1, :] * xe_scr[pl.ds(HALO - left, ts), :]
    for k in range(1, CONV_W):
        u = u + cw_ref[k:k + 1, :] * xe_scr[pl.ds(HALO - left + k, ts), :]

    lam = lam_ref[...]
    neg = -lam
    softplus = jnp.maximum(neg, 0.0) + jnp.log1p(jnp.exp(-jnp.abs(neg)))
    bw = width // LRU_BLOCKS
    for n in range(LRU_BLOCKS):
        cs = slice(n * bw, (n + 1) * bw)
        un = u[:, cs]
        ub = un.astype(BF)
        r = _sigmoid(_dot(ub, wa_ref[n]) + ba_ref[:, cs])
        gi = _sigmoid(_dot(ub, wx_ref[n]) + bx_ref[:, cs])
        log_a = -LRU_C * r * softplus[:, cs]
        a = jnp.exp(log_a)
        a_scr[:, cs] = a
        b_scr[:, cs] = jnp.sqrt(1.0 - a * a) * (gi * un)

    ng = ts // 8
    row = lax.broadcasted_iota(jnp.int32, (8, width), 0)

    def group(gi_, h):
        g = (ng - 1 - gi_) if reverse else gi_
        r0 = pl.multiple_of(g * 8, 8)
        a = a_scr[pl.ds(r0, 8), :]
        b = b_scr[pl.ds(r0, 8), :]
        for sft in (1, 2, 4):
            if reverse:
                a_sh = pltpu.roll(a, 8 - sft, 0)
                b_sh = pltpu.roll(b, 8 - sft, 0)
                valid = row < 8 - sft
            else:
                a_sh = pltpu.roll(a, sft, 0)
                b_sh = pltpu.roll(b, sft, 0)
                valid = row >= sft
            b = jnp.where(valid, a * b_sh + b, b)
            a = jnp.where(valid, a * a_sh, a)
        hrows = a * h + b
        b_scr[pl.ds(r0, 8), :] = hrows
        edge = hrows[0:1, :] if reverse else hrows[7:8, :]
        return jnp.broadcast_to(edge, (8, width))

    h_scr[...] = lax.fori_loop(0, ng, group, h_scr[...])
    o_ref[...] = b_scr[...].astype(o_ref.dtype)


def _lru_call(z, conv_w, conv_b, wa, ba, wx, bx, lam, bsz, seq, reverse):
    m = z.shape[0]
    ts = SEQ_TILE
    nt = seq // ts
    nlat = bsz * nt
    width = conv_b.shape[1]

    def row_blk(b, s):
        t = (nt - s) if reverse else (s - 1)
        return jnp.where(s == 0, nlat + b, b * nt + t)

    prev, nxt = _halo_specs(ts, width, 1, row_blk, m)
    vec = lambda shape: pl.BlockSpec(shape, lambda b, s: (0,) * len(shape))
    kern = functools.partial(_lru_kernel, reverse=reverse, nt=nt)
    return pl.pallas_call(
        kern,
        grid=(bsz, nt + 1),
        in_specs=[pl.BlockSpec((ts, width), lambda b, s: (row_blk(b, s), 1)), prev, nxt,
                  vec(conv_w.shape), vec(conv_b.shape), vec(wa.shape), vec(ba.shape),
                  vec(wx.shape), vec(bx.shape), vec(lam.shape)],
        out_specs=pl.BlockSpec((ts, width), lambda b, s: (row_blk(b, s), 0)),
        out_shape=jax.ShapeDtypeStruct((m, width), BF),
        scratch_shapes=[pltpu.VMEM((ts + 2 * HALO, width), F32), pltpu.VMEM((ts, width), F32),
                        pltpu.VMEM((ts, width), F32), pltpu.VMEM((8, width), F32)],
        compiler_params=_cparams("parallel", "arbitrary"),
        name="lru_bwd" if reverse else "lru_fwd",
    )(z, z, z, conv_w, conv_b, wa, ba, wx, bx, lam)


def _merge_kernel(x_ref, mod_ref, mp_ref, hf_ref, hb_ref, lg_ref, o_ref, gt_ref, wp_ref, wl_ref, wm_ref, wo_ref,
                  g1_ref, g2_ref, *rest, route):
    if route:
        rw_ref, x1_ref, h2_ref, rt_ref = rest
    else:
        x1_ref, h2_ref = rest
    d = x_ref.shape[1]
    y_pool = _dot(mp_ref[...], wp_ref[...])
    lru_in = (hf_ref[...].astype(F32) + hb_ref[...].astype(F32)) * _gelu_tanh(lg_ref[...].astype(F32))
    y_lru = _dot(lru_in.astype(BF), wl_ref[...])
    y_mla = _dot(o_ref[...], wm_ref[...])
    mix = (_sigmoid(gt_ref[:, 0:d].astype(F32)) * y_pool
           + _sigmoid(gt_ref[:, d:2 * d].astype(F32)) * y_lru
           + _sigmoid(gt_ref[:, 2 * d:3 * d].astype(F32)) * y_mla)
    y = _dot(mix.astype(BF), wo_ref[...])
    x1 = x_ref[...] + mod_ref[2:3, :] * _rms(y, g1_ref[...])
    x1_ref[...] = x1
    h2 = _rms(x1, g2_ref[...]) * (1.0 + mod_ref[4:5, :]) + mod_ref[3:4, :]
    h2_ref[...] = h2.astype(BF)
    if route:
        logits = jnp.dot(h2, rw_ref[...], preferred_element_type=F32, precision=lax.Precision.HIGHEST)
        col = lax.broadcasted_iota(jnp.int32, logits.shape, 1).astype(F32)
        logits = jnp.where(col < N_EXPERTS, logits, -jnp.inf)
        v1 = jnp.max(logits, axis=1, keepdims=True)
        i1 = jnp.min(jnp.where(logits == v1, col, float(LANE)), axis=1, keepdims=True)
        rest_l = jnp.where(col == i1, -jnp.inf, logits)
        v2 = jnp.max(rest_l, axis=1, keepdims=True)
        i2 = jnp.min(jnp.where(rest_l == v2, col, float(LANE)), axis=1, keepdims=True)
        e = jnp.exp(v2 - v1)
        gate1 = 1.0 / (1.0 + e)
        gate2 = e / (1.0 + e)
        rt_ref[...] = jnp.where(col == 0, i1, jnp.where(col == 1, i2,
                                jnp.where(col == 2, gate1, jnp.where(col == 3, gate2, 0.0))))


def _merge_call(x, mod, mp, hf, hb, z, o, wp, wl, wm, wo, g1, g2, rw, tm, n_rows, mod_idx):
    d = x.shape[1]
    route = rw is not None
    row = lambda c: pl.BlockSpec((tm, d), lambda i: (i, c))
    full = lambda a: pl.BlockSpec(a.shape, lambda i: (0,) * a.ndim)
    in_specs = [row(0), pl.BlockSpec((None, 6, d), lambda i: (mod_idx(i), 0, 0)),
                row(0), row(0), row(0), row(2), row(0), pl.BlockSpec((tm, 3 * d), lambda i: (i, 1)),
                full(wp), full(wl), full(wm), full(wo), full(g1), full(g2)]
    args = [x, mod, mp, hf, hb, z, o, z, wp, wl, wm, wo, g1, g2]
    out_specs = [row(0), row(0)]
    out_shape = [jax.ShapeDtypeStruct((n_rows, d), F32), jax.ShapeDtypeStruct((n_rows, d), BF)]
    if route:
        in_specs.append(full(rw))
        args.append(rw)
        out_specs.append(pl.BlockSpec((tm, LANE), lambda i: (i, 0)))
        out_shape.append(jax.ShapeDtypeStruct((n_rows, LANE), F32))
    return pl.pallas_call(
        functools.partial(_merge_kernel, route=route),
        grid=(n_rows // tm,),
        in_specs=in_specs, out_specs=out_specs, out_shape=out_shape,
        compiler_params=_cparams("parallel"),
        name="merge",
    )(*args)


def _swiglu_step(x, w1, w3, w2):
    a = _dot(x, w1)
    b = _dot(x, w3)
    return _dot((_silu(a) * b).astype(BF), w2)


def _ffn_kernel(h_ref, x_ref, mod_ref, g_ref, w1_ref, w3_ref, w2_ref, o_ref, acc):
    f = pl.program_id(1)

    @pl.when(f == 0)
    def _():
        acc[...] = jnp.zeros_like(acc)

    acc[...] += _swiglu_step(h_ref[...], w1_ref[...].astype(BF), w3_ref[...].astype(BF), w2_ref[...].astype(BF))

    @pl.when(f == pl.num_programs(1) - 1)
    def _():
        o_ref[...] = x_ref[...] + mod_ref[5:6, :] * _rms(acc[...], g_ref[...])


def _ffn_call(h2, x1, mod, g, w1, w3, w2, tm, mod_idx):
    m, d = x1.shape
    ff = w1.shape[1]
    tf = 256
    return pl.pallas_call(
        _ffn_kernel,
        grid=(m // tm, ff // tf),
        in_specs=[pl.BlockSpec((tm, d), lambda i, f: (i, 0)),
                  pl.BlockSpec((tm, d), lambda i, f: (i, 0)),
                  pl.BlockSpec((None, 6, d), lambda i, f: (mod_idx(i), 0, 0)),
                  pl.BlockSpec((1, d), lambda i, f: (0, 0)),
                  pl.BlockSpec((d, tf), lambda i, f: (0, f)),
                  pl.BlockSpec((d, tf), lambda i, f: (0, f)),
                  pl.BlockSpec((tf, d), lambda i, f: (f, 0))],
        out_specs=pl.BlockSpec((tm, d), lambda i, f: (i, 0)),
        out_shape=jax.ShapeDtypeStruct((m, d), F32),
        scratch_shapes=[pltpu.VMEM((tm, d), F32)],
        compiler_params=_cparams("parallel", "arbitrary"),
        name="ffn",
    )(h2, x1, mod, g, w1, w3, w2)


def _moe_kernel(blk_e_ref, nused_ref, x_ref, w1_ref, w3_ref, w2_ref, o_ref, acc):
    i = pl.program_id(0)
    f = pl.program_id(1)
    last = f == pl.num_programs(1) - 1
    used = i < nused_ref[0]

    @pl.when(jnp.logical_and(used, f == 0))
    def _():
        acc[...] = jnp.zeros_like(acc)

    @pl.when(used)
    def _():
        acc[...] += _swiglu_step(x_ref[...], w1_ref[...].astype(BF), w3_ref[...].astype(BF),
                                 w2_ref[...].astype(BF))

    @pl.when(jnp.logical_and(used, last))
    def _():
        o_ref[...] = acc[...].astype(o_ref.dtype)

    @pl.when(jnp.logical_and(jnp.logical_not(used), last))
    def _():
        o_ref[...] = jnp.zeros_like(o_ref)


def _moe_call(blk_e, nused, xs, w1, w3, w2, tme):
    n_rows, d = xs.shape
    ff = w1.shape[2]
    tf = 512
    return pl.pallas_call(
        _moe_kernel,
        grid_spec=pltpu.PrefetchScalarGridSpec(
            num_scalar_prefetch=2,
            grid=(n_rows // tme, ff // tf),
            in_specs=[pl.BlockSpec((tme, d), lambda i, f, be, nu: (i, 0)),
                      pl.BlockSpec((None, d, tf), lambda i, f, be, nu: (be[i], 0, f)),
                      pl.BlockSpec((None, d, tf), lambda i, f, be, nu: (be[i], 0, f)),
                      pl.BlockSpec((None, tf, d), lambda i, f, be, nu: (be[i], f, 0))],
            out_specs=pl.BlockSpec((tme, d), lambda i, f, be, nu: (i, 0)),
            scratch_shapes=[pltpu.VMEM((tme, d), F32)]),
        out_shape=jax.ShapeDtypeStruct((n_rows, d), BF),
        compiler_params=_cparams("arbitrary", "arbitrary"),
        name="moe",
    )(blk_e, nused, xs, w1, w3, w2)


def _combine_kernel(ya_ref, yb_ref, rt_ref, x_ref, mod_ref, g_ref, o_ref):
    g1 = rt_ref[:, 2:3]
    g2 = rt_ref[:, 3:4]
    f = g1 * ya_ref[...].astype(F32) + g2 * yb_ref[...].astype(F32)
    o_ref[...] = x_ref[...] + mod_ref[5:6, :] * _rms(f, g_ref[...])


def _combine_call(ya, yb, rt, x1, mod, g, tm, mod_idx):
    n, d = x1.shape
    return pl.pallas_call(
        _combine_kernel,
        grid=(n // tm,),
        in_specs=[pl.BlockSpec((tm, d), lambda i: (i, 0)),
                  pl.BlockSpec((tm, d), lambda i: (i, 0)),
                  pl.BlockSpec((tm, LANE), lambda i: (i, 0)),
                  pl.BlockSpec((tm, d), lambda i: (i, 0)),
                  pl.BlockSpec((None, 6, d), lambda i: (mod_idx(i), 0, 0)),
                  pl.BlockSpec((1, d), lambda i: (0, 0))],
        out_specs=pl.BlockSpec((tm, d), lambda i: (i, 0)),
        out_shape=jax.ShapeDtypeStruct((n, d), F32),
        compiler_params=_cparams("parallel"),
        name="combine",
    )(ya, yb, rt, x1, mod, g)


def _route_plan(rt, tme):
    n = rt.shape[0]
    e_flat = rt[:, :TOP_K].astype(jnp.int32).reshape(-1)
    onehot = (e_flat[:, None] == jnp.arange(N_EXPERTS, dtype=jnp.int32)[None, :]).astype(jnp.int32)
    csum = jnp.cumsum(onehot, axis=0)
    rank = jnp.sum(onehot * csum, axis=1) - 1
    counts = csum[-1]
    padded = (counts + tme - 1) // tme * tme
    pad_end = jnp.cumsum(padded)
    pad_start = pad_end - padded
    dest = pad_start[e_flat] + rank
    n_rows = n * TOP_K + N_EXPERTS * tme
    n_blk = n_rows // tme
    src_tok = jnp.zeros((n_rows,), jnp.int32).at[dest].set(jnp.arange(n * TOP_K, dtype=jnp.int32) // TOP_K)
    blk_start = jnp.arange(n_blk, dtype=jnp.int32) * tme
    blk_e = jnp.minimum(jnp.sum((pad_end[None, :] <= blk_start[:, None]).astype(jnp.int32), axis=1), N_EXPERTS - 1)
    nused = (pad_end[-1] // tme).astype(jnp.int32).reshape(1)
    return dest, src_tok, blk_e, nused


def _swap_perm():
    j = np.arange(QK_ROPE)
    axis, half, f = j // (2 * ROPE_FREQS), (j % (2 * ROPE_FREQS)) // ROPE_FREQS, j % ROPE_FREQS
    return axis * 2 * ROPE_FREQS + (1 - half) * ROPE_FREQS + f


def _rope_tables(seq, ctx_len):
    rows = seq // GRID_W
    row = jnp.repeat(jnp.arange(rows, dtype=F32), GRID_W)
    col = (jnp.arange(rows * GRID_W) % GRID_W).astype(F32)
    inv = ROPE_THETA ** (-jnp.arange(ROPE_FREQS, dtype=F32) / ROPE_FREQS)
    ang = jnp.stack([row[:, None] * inv, col[:, None] * inv], axis=1)
    cos, sin = jnp.cos(ang), jnp.sin(ang)
    cos64 = jnp.stack([cos, cos], axis=2).reshape(seq, QK_ROPE)
    sin64 = jnp.stack([-sin, sin], axis=2).reshape(seq, QK_ROPE)
    pad = jnp.zeros((seq, LANE - QK_ROPE), F32)
    cos_t = jnp.concatenate([cos64, pad], axis=1)
    sin_t = jnp.concatenate([sin64, pad], axis=1)
    ident = jnp.concatenate([jnp.ones((ctx_len, QK_ROPE), F32), jnp.zeros((ctx_len, LANE - QK_ROPE), F32)], axis=1)
    return jnp.concatenate([cos_t, ident], axis=0), jnp.concatenate([sin_t, jnp.zeros((ctx_len, LANE), F32)], axis=0)


def _prep_w_in(w):
    d = w.shape[0]
    pw = lw = d
    o = np.cumsum([0, pw, lw, lw, Q_LORA, KV_LORA, QK_ROPE, 3 * d])
    pool, lx, lg, cq, ckv, kr, gt = (w[:, o[i]:o[i + 1]] for i in range(7))
    return jnp.concatenate([pool, lx, lg, gt, cq, ckv, kr, kr[:, _swap_perm()]], axis=1).astype(BF)


def _prep_w_uq(w):
    qk = QK_NOPE + QK_ROPE
    w = w.reshape(w.shape[0], HEADS, qk)
    rope = w[:, :, QK_NOPE:]
    return jnp.concatenate([w, rope[:, :, _swap_perm()]], axis=2).reshape(w.shape[0], HEADS * HEAD_W).T.astype(BF)


def kernel(x, c, ctx, c_ctx, mod_w, mod_b, pre_mix_g, post_mix_g, pre_ffn_g, post_ffn_g, w_in, pool_w, pool_scale,
           pool_proj, conv_w, conv_b, gate_a_w, gate_a_b, gate_x_w, gate_x_b, lru_lambda, lru_proj, q_norm_g, w_uq,
           kv_norm_g, w_ukv, mla_proj, w_out, ffn_w1, ffn_w3, ffn_w2, router_w, moe_w1, moe_w3, moe_w2):
    bsz, seq, d = x.shape
    ctx_len = ctx.shape[1]
    depth = mod_w.shape[0]
    assert ctx_len == SEQ_TILE and seq % SEQ_TILE == 0 and seq % GRID_W == 0
    n_lat = bsz * seq
    n_ctx = bsz * ctx_len
    n_all = n_lat + n_ctx
    tm = min(1024, seq, n_ctx)
    assert seq % tm == 0 and n_ctx % tm == 0
    tm_merge = tm // 2

    def mod_idx_for(rows):
        return lambda i: jnp.where(i < n_lat // rows, i // (seq // rows), bsz)

    mod_idx = mod_idx_for(tm)

    cc = jnp.concatenate([c, c_ctx[None, :], jnp.zeros((8 - bsz - 1, d), F32)], axis=0)
    mods = _mod_call(cc, mod_w, mod_b)[:, :bsz + 1].reshape(depth, bsz + 1, 6, d)
    cos_t, sin_t = _rope_tables(seq, ctx_len)
    x_all = jnp.concatenate([x.reshape(n_lat, d), ctx.reshape(n_ctx, d)], axis=0)
    row1 = lambda v: v.reshape(1, -1)

    for l in range(depth):
        last = l == depth - 1
        n_out = n_lat if last else n_all
        mod = mods[l]
        z = _inproj_call(x_all, mod, row1(pre_mix_g[l]), _prep_w_in(w_in[l]), tm, mod_idx)
        wkv = w_ukv[l].reshape(KV_LORA, HEADS, QK_NOPE + V_DIM)
        wk = wkv[:, :, :QK_NOPE].reshape(KV_LORA, HEADS * QK_NOPE).astype(BF)
        wvt = wkv[:, :, QK_NOPE:].reshape(KV_LORA, HEADS * V_DIM).T.astype(BF)
        q, k, v = _qkv_call(z, cos_t, sin_t, row1(q_norm_g[l]), row1(kv_norm_g[l]), _prep_w_uq(w_uq[l]),
                            wk, wvt, bsz, seq, ctx_len)
        tq = min(512, seq)
        tk = 768 if (seq + ctx_len) % 768 == 0 else SEQ_TILE
        o = _attn_call(q, k, v, None, bsz=bsz, row_blk0=0, nq=seq // tq, tq=tq, kblk=0, klen=seq + ctx_len, tk=tk,
                       kp=tk)
        if not last:
            o = _attn_call(q, k, v, o, bsz=bsz, row_blk0=n_lat // ctx_len, nq=1, tq=ctx_len,
                           kblk=seq // ctx_len, klen=ctx_len, tk=ctx_len, kp=ctx_len)
        mp = _pool_call(z, pool_w[l].astype(BF), row1(pool_scale[l]), bsz, seq, ctx_len, n_out)
        hs = []
        for dr in range(2):
            hs.append(_lru_call(z, conv_w[l], row1(conv_b[l]), gate_a_w[l, dr].astype(BF), row1(gate_a_b[l, dr]),
                                gate_x_w[l, dr].astype(BF), row1(gate_x_b[l, dr]), row1(lru_lambda[l, dr]),
                                bsz, seq, dr == 1))
        moe_layer = l % 2 == 1
        rw = None
        if moe_layer:
            rw = jnp.concatenate([router_w[l // 2], jnp.zeros((d, LANE - N_EXPERTS), F32)], axis=1)
        outs = _merge_call(x_all, mod, mp, hs[0], hs[1], z, o, pool_proj[l].astype(BF), lru_proj[l].astype(BF),
                           mla_proj[l].astype(BF), w_out[l].astype(BF), row1(post_mix_g[l]), row1(pre_ffn_g[l]),
                           rw, tm_merge, n_out, mod_idx_for(tm_merge))
        if not moe_layer:
            x1, h2 = outs
            x_all = _ffn_call(h2, x1, mod, row1(post_ffn_g[l]), ffn_w1[l // 2], ffn_w3[l // 2], ffn_w2[l // 2],
                              tm, mod_idx)
        else:
            x1, h2, rt = outs
            tme = min(1024, n_out * TOP_K // N_EXPERTS)
            dest, src_tok, blk_e, nused = _route_plan(rt, tme)
            xs = jnp.take(h2, src_tok, axis=0)
            y = _moe_call(blk_e, nused, xs, moe_w1[l // 2], moe_w3[l // 2], moe_w2[l // 2], tme)
            dest2 = dest.reshape(n_out, TOP_K)
            ya = jnp.take(y, dest2[:, 0], axis=0)
            yb = jnp.take(y, dest2[:, 1], axis=0)
            x_all = _combine_call(ya, yb, rt, x1, mod, row1(post_ffn_g[l]), tm, mod_idx)
    return x_all[:n_lat].reshape(bsz, seq, d)
```

```python
import functools
import math

import numpy as np
import jax
import jax.numpy as jnp
from jax import lax
from jax.experimental import pallas as pl
from jax.experimental.pallas import tpu as pltpu
from jax.experimental.pallas import tpu_sc as plsc

BF = jnp.bfloat16
F32 = jnp.float32

RMS_EPS = 1e-6
GRID_W = 64
POOL_WINDOWS = (2, 4, 8, 16)
LRU_BLOCKS = 8
CONV_W = 4
LRU_C = 8.0
HEADS = 8
Q_LORA = 384
KV_LORA = 256
QK_NOPE = 128
QK_ROPE = 64
V_DIM = 128
MLA_SCALE = (QK_NOPE + QK_ROPE) ** -0.5
Q_SCALE = MLA_SCALE * math.log2(math.e)
ROPE_FREQS = QK_ROPE // 4
ROPE_THETA = 10000.0
N_EXPERTS = 8
TOP_K = 2

LANE = 128
HALO = 16
SEQ_TILE = 256
HEAD_W = 256
INPROJ_TN = 1024
FFN_TF = 768
VMEM_LIMIT = 48 * 1024 * 1024


def _cparams(*sem):
    return pltpu.CompilerParams(dimension_semantics=sem, vmem_limit_bytes=VMEM_LIMIT)


def _rms(x, g):
    ms = jnp.mean(x * x, axis=-1, keepdims=True)
    return x * lax.rsqrt(ms + RMS_EPS) * g


def _sigmoid(x):
    return 0.5 * jnp.tanh(0.5 * x) + 0.5


def _silu(x):
    return x * _sigmoid(x)


def _gelu_tanh(x):
    return 0.5 * x * (1.0 + jnp.tanh(math.sqrt(2.0 / math.pi) * (x + 0.044715 * (x * x * x))))


def _dot(a, b):
    return jnp.dot(a, b, preferred_element_type=F32)


def _mod_kernel(c_ref, w_ref, b_ref, o_ref):
    s = _silu(c_ref[...])
    o_ref[...] = _dot(s.astype(BF), w_ref[...].astype(BF)) + b_ref[...]


def _mod_call(cc, mod_w, mod_b):
    depth, d, n6 = mod_w.shape
    tn = 1536
    return pl.pallas_call(
        _mod_kernel,
        grid=(depth, n6 // tn),
        in_specs=[pl.BlockSpec((8, d), lambda l, j: (0, 0)),
                  pl.BlockSpec((None, d, tn), lambda l, j: (l, 0, j)),
                  pl.BlockSpec((None, 1, tn), lambda l, j: (l, 0, j))],
        out_specs=pl.BlockSpec((None, 8, tn), lambda l, j: (l, 0, j)),
        out_shape=jax.ShapeDtypeStruct((depth, 8, n6), F32),
        compiler_params=_cparams("parallel", "arbitrary"),
        name="mod",
    )(cc, mod_w, mod_b.reshape(depth, 1, n6))


def _inproj_kernel(x_ref, mod_ref, g_ref, w_ref, z_ref, h_scr):
    @pl.when(pl.program_id(1) == 0)
    def _():
        h = _rms(x_ref[...], g_ref[...])
        h = h * (1.0 + mod_ref[1:2, :]) + mod_ref[0:1, :]
        h_scr[...] = h.astype(BF)

    z_ref[...] = _dot(h_scr[...], w_ref[...]).astype(BF)


def _inproj_call(x, mod, g, w, tm, mod_idx):
    m, d = x.shape
    n = w.shape[1]
    tn = INPROJ_TN
    return pl.pallas_call(
        _inproj_kernel,
        grid=(m // tm, n // tn),
        in_specs=[pl.BlockSpec((tm, d), lambda i, j: (i, 0)),
                  pl.BlockSpec((None, 6, d), lambda i, j: (mod_idx(i), 0, 0)),
                  pl.BlockSpec((1, d), lambda i, j: (0, 0)),
                  pl.BlockSpec((d, tn), lambda i, j: (0, j))],
        out_specs=pl.BlockSpec((tm, tn), lambda i, j: (i, j)),
        out_shape=jax.ShapeDtypeStruct((m, n), BF),
        scratch_shapes=[pltpu.VMEM((tm, d), BF)],
        compiler_params=_cparams("parallel", "arbitrary"),
        name="inproj",
    )(x, mod, g, w)


def _rope(x, cos, sin):
    return x * cos + pltpu.roll(x, LANE // 2, 1) * sin


_NT = (((1,), (1,)), ((), ()))


def _qkv_kernel(z_ref, cos_ref, sin_ref, cost_ref, sint_ref, qg_ref, kvg_ref, wuqt_ref, wk_ref, wvt_ref,
                qt_ref, k_ref, vt_ref):
    z = z_ref[...]
    cq = z[:, :Q_LORA].astype(F32)
    ckv = z[:, Q_LORA:Q_LORA + KV_LORA].astype(F32)
    kr = z[:, Q_LORA + KV_LORA:].astype(F32)
    cqn = _rms(cq, qg_ref[...]).astype(BF)
    ckvn = _rms(ckv, kvg_ref[...]).astype(BF)
    qt = lax.dot_general(wuqt_ref[...], cqn, _NT, preferred_element_type=F32)
    kn = _dot(ckvn, wk_ref[...])
    vt = lax.dot_general(wvt_ref[...], ckvn, _NT, preferred_element_type=F32)
    krot = _rope(kr, cos_ref[...], sin_ref[...]).astype(BF)
    cost = cost_ref[...]
    sint = sint_ref[...]
    for h in range(HEADS):
        c0 = h * HEAD_W
        c1 = c0 + QK_NOPE
        c2 = c1 + QK_ROPE
        qt_ref[c0:c1, :] = (qt[c0:c1, :] * Q_SCALE).astype(BF)
        qt_ref[c1:c2, :] = ((qt[c1:c2, :] * cost + qt[c2:c0 + HEAD_W, :] * sint) * Q_SCALE).astype(BF)
        qt_ref[c2:c0 + HEAD_W, :] = jnp.zeros((QK_ROPE, qt.shape[1]), BF)
        k_ref[h, :, 0:QK_NOPE] = kn[:, h * QK_NOPE:(h + 1) * QK_NOPE].astype(BF)
        k_ref[h, :, QK_NOPE:HEAD_W] = krot
        vt_ref[h] = vt[h * V_DIM:(h + 1) * V_DIM, :].astype(BF)


def _qkv_call(z, cos_t, sin_t, qg, kvg, wuqt, wk, wvt, bsz, seq, ctx_len, z_off):
    m = z.shape[0]
    cos_tt = cos_t[:, :QK_ROPE].T
    sin_tt = sin_t[:, :QK_ROPE].T
    ts = SEQ_TILE
    nt = seq // ts
    nlat = bsz * nt
    lk = seq + ctx_len
    zw = Q_LORA + KV_LORA + LANE
    assert z_off % zw == 0
    zcol = z_off // zw

    def tab_idx(i):
        return (jnp.where(i < nlat, i % nt, nt), 0)

    def kv_idx(i):
        return (jnp.where(i < nlat, i // nt, i - nlat), 0, jnp.where(i < nlat, i % nt, nt), 0)

    def vt_idx(i):
        return (jnp.where(i < nlat, i // nt, i - nlat), 0, 0, jnp.where(i < nlat, i % nt, nt))

    return pl.pallas_call(
        _qkv_kernel,
        grid=(m // ts,),
        in_specs=[pl.BlockSpec((ts, zw), lambda i: (i, zcol)),
                  pl.BlockSpec((ts, LANE), tab_idx),
                  pl.BlockSpec((ts, LANE), tab_idx),
                  pl.BlockSpec((QK_ROPE, ts), lambda i: tab_idx(i)[::-1]),
                  pl.BlockSpec((QK_ROPE, ts), lambda i: tab_idx(i)[::-1]),
                  pl.BlockSpec((1, Q_LORA), lambda i: (0, 0)),
                  pl.BlockSpec((1, KV_LORA), lambda i: (0, 0)),
                  pl.BlockSpec(wuqt.shape, lambda i: (0, 0)),
                  pl.BlockSpec(wk.shape, lambda i: (0, 0)),
                  pl.BlockSpec(wvt.shape, lambda i: (0, 0))],
        out_specs=[pl.BlockSpec((HEADS * HEAD_W, ts), lambda i: (0, i)),
                   pl.BlockSpec((None, HEADS, ts, HEAD_W), kv_idx),
                   pl.BlockSpec((None, HEADS, V_DIM, ts), vt_idx)],
        out_shape=[jax.ShapeDtypeStruct((HEADS * HEAD_W, m), BF),
                   jax.ShapeDtypeStruct((bsz, HEADS, lk, HEAD_W), BF),
                   jax.ShapeDtypeStruct((bsz, HEADS, V_DIM, lk), BF)],
        compiler_params=_cparams("parallel"),
        name="qkv_up",
    )(z, cos_t, sin_t, cos_tt, sin_tt, qg, kvg, wuqt, wk, wvt)


def _col_reduce(x, pair, red):
    n = x.shape[0] // 4
    a = pair(pair(x[0:n], x[n:2 * n]), pair(x[2 * n:3 * n], x[3 * n:4 * n]))
    return red(a, axis=0, keepdims=True)


def _attn_kernel(qt_ref, k_ref, vt_ref, *rest, kp, g, nk):
    o_ref, s_scr, p_scr = rest[-3:]
    qt = qt_ref[...]
    tq = qt.shape[1]

    def scores(c, r):
        r0 = (c * g + r) * kp
        s = _dot(k_ref[r0:r0 + kp, :], qt)
        s_scr[c % 2, r * kp:(r + 1) * kp, :] = s
        return _col_reduce(s, jnp.maximum, jnp.max)

    cur = [scores(0, r) for r in range(g)]
    m = jnp.full((1, tq), -1e30, F32)
    l = jnp.zeros((1, tq), F32)
    acc = jnp.zeros((V_DIM, tq), F32)
    for c in range(nk):
        m_new = m
        for mx in cur:
            m_new = jnp.maximum(m_new, mx)
        alpha = jnp.exp2(m - m_new)
        nxt = []
        lsum = None
        for r in range(g):
            if c + 1 < nk:
                nxt.append(scores(c + 1, r))
            p = jnp.exp2(s_scr[c % 2, r * kp:(r + 1) * kp, :] - m_new)
            part = _col_reduce(p, jnp.add, jnp.sum)
            lsum = part if lsum is None else lsum + part
            p_scr[c % 2, r * kp:(r + 1) * kp, :] = p.astype(BF)
        acc = alpha * acc + _dot(vt_ref[:, c * g * kp:(c + 1) * g * kp], p_scr[c % 2])
        l = alpha * l + lsum
        m = m_new
        cur = nxt
    o_ref[...] = (acc / l).T.astype(o_ref.dtype)


def _attn_call(q, k, v, *, bsz, row_blk0, nq, tq, kblk, klen, tk, kp):
    kern = functools.partial(_attn_kernel, kp=kp, g=tk // kp, nk=klen // tk)
    return pl.pallas_call(
        kern,
        grid=(bsz, HEADS, nq),
        in_specs=[pl.BlockSpec((HEAD_W, tq), lambda b, h, i: (h, row_blk0 + b * nq + i)),
                  pl.BlockSpec((None, None, klen, HEAD_W), lambda b, h, i: (b, h, kblk, 0)),
                  pl.BlockSpec((None, None, V_DIM, klen), lambda b, h, i: (b, h, 0, kblk))],
        out_specs=pl.BlockSpec((tq, V_DIM), lambda b, h, i: (b * nq + i, h)),
        out_shape=jax.ShapeDtypeStruct((bsz * nq * tq, HEADS * V_DIM), BF),
        scratch_shapes=[pltpu.VMEM((2, tk, tq), F32), pltpu.VMEM((2, tk, tq), BF)],
        compiler_params=_cparams("parallel", "parallel", "arbitrary"),
        name="attn",
    )(q, k, v)


def _seq_flags(i, nlat, nt):
    is_ctx = i >= nlat
    t = jnp.where(is_ctx, 0, i % nt)
    first = jnp.logical_or(is_ctx, t == 0)
    last = jnp.logical_or(is_ctx, t == nt - 1)
    return is_ctx, t, first, last


def _pool_kernel(x_ref, xp_ref, xn_ref, pw_ref, ps_ref, o_ref, *, nlat, nt, seq, ctx_len):
    ts = x_ref.shape[0]
    is_ctx, t, first, last = _seq_flags(pl.program_id(0), nlat, nt)
    seq_len = jnp.where(is_ctx, ctx_len, seq)
    x = x_ref[...]
    xp = jnp.where(first, jnp.zeros_like(xp_ref[...]), xp_ref[...])
    xn = jnp.where(last, jnp.zeros_like(xn_ref[...]), xn_ref[...])
    xe = jnp.concatenate([xp, x, xn], axis=0)
    tpos = t * ts + lax.broadcasted_iota(jnp.int32, (ts, 1), 0)
    rel = (lax.broadcasted_iota(jnp.int32, (ts, ts + 2 * HALO), 1) - HALO
           - lax.broadcasted_iota(jnp.int32, (ts, ts + 2 * HALO), 0))
    gw = x.shape[1] // len(POOL_WINDOWS)
    for g, w in enumerate(POOL_WINDOWS):
        cs = slice(g * gw, (g + 1) * gw)
        band = jnp.where(rel >= -(w // 2), jnp.where(rel < w - w // 2, 1.0, 0.0), 0.0).astype(BF)
        s = _dot(band, xe[:, cs])
        cnt = (jnp.minimum(tpos + (w - w // 2), seq_len) - jnp.maximum(tpos - w // 2, 0)).astype(F32)
        mean_minus = s / cnt - x[:, cs].astype(F32)
        o_ref[:, cs] = (_dot(mean_minus.astype(BF), pw_ref[g]) * ps_ref[:, cs]).astype(BF)


def _halo_specs(ts, width, col, row_blk, m):
    r = ts // HALO
    nh = m // HALO
    prev = pl.BlockSpec((HALO, width), lambda *a: (jnp.maximum(row_blk(*a) * r - 1, 0), col))
    nxt = pl.BlockSpec((HALO, width), lambda *a: (jnp.minimum((row_blk(*a) + 1) * r, nh - 1), col))
    return prev, nxt


def _pool_call(z, pool_w, pool_scale, bsz, seq, ctx_len, n_rows):
    m = z.shape[0]
    ts = SEQ_TILE
    nt = seq // ts
    nlat = bsz * nt
    width = pool_scale.shape[1]
    prev, nxt = _halo_specs(ts, width, 0, lambda i: i, m)
    kern = functools.partial(_pool_kernel, nlat=nlat, nt=nt, seq=seq, ctx_len=ctx_len)
    return pl.pallas_call(
        kern,
        grid=(n_rows // ts,),
        in_specs=[pl.BlockSpec((ts, width), lambda i: (i, 0)), prev, nxt,
                  pl.BlockSpec(pool_w.shape, lambda i: (0, 0, 0)),
                  pl.BlockSpec((1, width), lambda i: (0, 0))],
        out_specs=pl.BlockSpec((ts, width), lambda i: (i, 0)),
        out_shape=jax.ShapeDtypeStruct((n_rows, width), BF),
        compiler_params=_cparams("parallel"),
        name="pool",
    )(z, z, z, pool_w, pool_scale)


def _lru_kernel(x_ref, xp_ref, xn_ref, cw_ref, cb_ref, wa_ref, ba_ref, wx_ref, bx_ref, lam_ref, o_ref,
                xe_scr, a_scr, b_scr, ga_scr, gb_scr, hp_scr, h_scr, *, reverse, nt):
    ts, width = x_ref.shape
    s = pl.program_id(1)
    t = (nt - s) if reverse else (s - 1)
    first = jnp.logical_or(s == 0, t == 0)
    last = jnp.logical_or(s == 0, t == nt - 1)

    @pl.when(s == 0)
    def _():
        h_scr[...] = jnp.zeros_like(h_scr)

    xp = xp_ref[...].astype(F32)
    xn = xn_ref[...].astype(F32)
    xe_scr[0:HALO, :] = jnp.where(first, jnp.zeros_like(xp), xp)
    xe_scr[HALO:HALO + ts, :] = x_ref[...].astype(F32)
    xe_scr[HALO + ts:, :] = jnp.where(last, jnp.zeros_like(xn), xn)
    left = CONV_W // 2
    u = cb_ref[...] + cw_ref[0:1, :] * xe_scr[pl.ds(HALO - left, ts), :]
    for k in range(1, CONV_W):
        u = u + cw_ref[k:k + 1, :] * xe_scr[pl.ds(HALO - left + k, ts), :]

    lam = lam_ref[...]
    neg = -lam
    softplus = jnp.maximum(neg, 0.0) + jnp.log1p(jnp.exp(-jnp.abs(neg)))
    coef = (-LRU_C * math.log2(math.e)) * softplus
    bw = width // LRU_BLOCKS
    for n in range(LRU_BLOCKS):
        cs = slice(n * bw, (n + 1) * bw)
        un = u[:, cs]
        ub = un.astype(BF)
        r = _sigmoid(_dot(ub, wa_ref[n]) + ba_ref[:, cs])
        gi = _sigmoid(_dot(ub, wx_ref[n]) + bx_ref[:, cs])
        a = jnp.exp2(r * coef[:, cs])
        a_scr[n] = a
        b_scr[n] = jnp.sqrt(1.0 - a * a) * (gi * un)

    ng = ts // 8
    slab = lambda n, j: (n, pl.ds(j, ng, stride=8), slice(None))
    for n in range(LRU_BLOCKS):
        hl = ca = None
        for j in (range(7, -1, -1) if reverse else range(8)):
            aj = a_scr[slab(n, j)]
            bj = b_scr[slab(n, j)]
            if hl is None:
                hl, ca = bj, aj
            else:
                hl = aj * hl + bj
                ca = aj * ca
                b_scr[slab(n, j)] = hl
                a_scr[slab(n, j)] = ca
        ga_scr[n] = ca
        gb_scr[n] = hl
    hs = [h_scr[n] for n in range(LRU_BLOCKS)]
    for g in (range(ng - 1, -1, -1) if reverse else range(ng)):
        for n in range(LRU_BLOCKS):
            hp_scr[n, g:g + 1, :] = hs[n]
            hs[n] = ga_scr[n, g:g + 1, :] * hs[n] + gb_scr[n, g:g + 1, :]
    for n in range(LRU_BLOCKS):
        h_scr[n] = hs[n]
        hp = hp_scr[n]
        for j in range(8):
            b_scr[slab(n, j)] = b_scr[slab(n, j)] + a_scr[slab(n, j)] * hp
        o_ref[:, n * bw:(n + 1) * bw] = b_scr[n].astype(o_ref.dtype)


def _lru_call(z, conv_w, conv_b, wa, ba, wx, bx, lam, bsz, seq, reverse):
    m = z.shape[0]
    ts = SEQ_TILE
    nt = seq // ts
    nlat = bsz * nt
    width = conv_b.shape[1]

    def row_blk(b, s):
        t = (nt - s) if reverse else (s - 1)
        return jnp.where(s == 0, nlat + b, b * nt + t)

    prev, nxt = _halo_specs(ts, width, 1, row_blk, m)
    vec = lambda shape: pl.BlockSpec(shape, lambda b, s: (0,) * len(shape))
    kern = functools.partial(_lru_kernel, reverse=reverse, nt=nt)
    return pl.pallas_call(
        kern,
        grid=(bsz, nt + 1),
        in_specs=[pl.BlockSpec((ts, width), lambda b, s: (row_blk(b, s), 1)), prev, nxt,
                  vec(conv_w.shape), vec(conv_b.shape), vec(wa.shape), vec(ba.shape),
                  vec(wx.shape), vec(bx.shape), vec(lam.shape)],
        out_specs=pl.BlockSpec((ts, width), lambda b, s: (row_blk(b, s), 0)),
        out_shape=jax.ShapeDtypeStruct((m, width), BF),
        scratch_shapes=[pltpu.VMEM((ts + 2 * HALO, width), F32)]
        + [pltpu.VMEM((LRU_BLOCKS, ts, width // LRU_BLOCKS), F32)] * 2
        + [pltpu.VMEM((LRU_BLOCKS, ts // 8, width // LRU_BLOCKS), F32)] * 3
        + [pltpu.VMEM((LRU_BLOCKS, 1, width // LRU_BLOCKS), F32)],
        compiler_params=_cparams("parallel", "arbitrary"),
        name="lru_bwd" if reverse else "lru_fwd",
    )(z, z, z, conv_w, conv_b, wa, ba, wx, bx, lam)


def _merge_kernel(x_ref, mod_ref, mp_ref, hf_ref, hb_ref, lg_ref, o_ref, gt_ref, wp_ref, wl_ref, wm_ref, wo_ref,
                  g1_ref, g2_ref, *rest, route):
    if route:
        rw_ref, x1_ref, h2_ref, rt_ref = rest
    else:
        x1_ref, h2_ref = rest
    d = x_ref.shape[1]
    y_pool = _dot(mp_ref[...], wp_ref[...])
    lru_in = (hf_ref[...].astype(F32) + hb_ref[...].astype(F32)) * _gelu_tanh(lg_ref[...].astype(F32))
    y_lru = _dot(lru_in.astype(BF), wl_ref[...])
    y_mla = _dot(o_ref[...], wm_ref[...])
    mix = (_sigmoid(gt_ref[:, 0:d].astype(F32)) * y_pool
           + _sigmoid(gt_ref[:, d:2 * d].astype(F32)) * y_lru
           + _sigmoid(gt_ref[:, 2 * d:3 * d].astype(F32)) * y_mla)
    y = _dot(mix.astype(BF), wo_ref[...])
    x1 = x_ref[...] + mod_ref[2:3, :] * _rms(y, g1_ref[...])
    x1_ref[...] = x1
    h2 = _rms(x1, g2_ref[...]) * (1.0 + mod_ref[4:5, :]) + mod_ref[3:4, :]
    h2_ref[...] = h2.astype(h2_ref.dtype)
    if route:
        logit = [jnp.sum(h2 * rw_ref[e:e + 1, :], axis=1, keepdims=True) for e in range(N_EXPERTS)]
        v1, i1 = logit[0], jnp.zeros_like(logit[0])
        for e in range(1, N_EXPERTS):
            upd = logit[e] > v1
            v1 = jnp.where(upd, logit[e], v1)
            i1 = jnp.where(upd, float(e), i1)
        v2, i2 = jnp.full_like(v1, -jnp.inf), jnp.zeros_like(v1)
        for e in range(N_EXPERTS):
            cand = jnp.where(i1 == float(e), -jnp.inf, logit[e])
            upd = cand > v2
            v2 = jnp.where(upd, cand, v2)
            i2 = jnp.where(upd, float(e), i2)
        ex = jnp.exp(v2 - v1)
        gate1 = 1.0 / (1.0 + ex)
        gate2 = ex / (1.0 + ex)
        col = lax.broadcasted_iota(jnp.int32, rt_ref.shape, 1)
        rt_ref[...] = jnp.where(col == 0, i1, jnp.where(col == 1, i2,
                                jnp.where(col == 2, gate1, jnp.where(col == 3, gate2, 0.0))))


def _merge_call(x, mod, mp, hf, hb, z, o, wp, wl, wm, wo, g1, g2, rw, tm, n_rows, mod_idx):
    d = x.shape[1]
    route = rw is not None
    row = lambda c: pl.BlockSpec((tm, d), lambda i: (i, c))
    full = lambda a: pl.BlockSpec(a.shape, lambda i: (0,) * a.ndim)
    in_specs = [row(0), pl.BlockSpec((None, 6, d), lambda i: (mod_idx(i), 0, 0)),
                row(0), row(0), row(0), row(2), row(0), pl.BlockSpec((tm, 3 * d), lambda i: (i, 1)),
                full(wp), full(wl), full(wm), full(wo), full(g1), full(g2)]
    args = [x, mod, mp, hf, hb, z, o, z, wp, wl, wm, wo, g1, g2]
    out_specs = [row(0), row(0)]
    out_shape = [jax.ShapeDtypeStruct((n_rows, d), F32), jax.ShapeDtypeStruct((n_rows, d), F32 if route else BF)]
    if route:
        in_specs.append(full(rw))
        args.append(rw)
        out_specs.append(pl.BlockSpec((tm, LANE), lambda i: (i, 0)))
        out_shape.append(jax.ShapeDtypeStruct((n_rows, LANE), F32))
    return pl.pallas_call(
        functools.partial(_merge_kernel, route=route),
        grid=(n_rows // tm,),
        in_specs=in_specs, out_specs=out_specs, out_shape=out_shape,
        compiler_params=_cparams("parallel"),
        name="merge",
    )(*args)


def _swiglu_step(x, w1, w3, w2):
    a = _dot(x, w1)
    b = _dot(x, w3)
    return _dot((_silu(a) * b).astype(BF), w2)


def _ffn_kernel(h_ref, x_ref, mod_ref, g_ref, w13_ref, w2_ref, o_ref):
    f = pl.program_id(1)
    tf = w2_ref.shape[0]

    @pl.when(f == 0)
    def _():
        o_ref[...] = jnp.zeros_like(o_ref)

    ab = _dot(h_ref[...], w13_ref[...])
    o_ref[...] += _dot((_silu(ab[:, :tf]) * ab[:, tf:]).astype(BF), w2_ref[...])

    @pl.when(f == pl.num_programs(1) - 1)
    def _():
        o_ref[...] = x_ref[...] + mod_ref[5:6, :] * _rms(o_ref[...], g_ref[...])


def _ffn_call(h2, x1, mod, g, w1, w3, w2, tm, mod_idx):
    m, d = x1.shape
    tf = FFN_TF
    pad = -w1.shape[1] % tf
    w1, w3 = (jnp.pad(w, ((0, 0), (0, pad))) for w in (w1, w3))
    ff = w1.shape[1]
    w13 = jnp.stack([w1.reshape(d, ff // tf, tf), w3.reshape(d, ff // tf, tf)], axis=2).reshape(d, 2 * ff).astype(BF)
    w2 = jnp.pad(w2, ((0, pad), (0, 0))).astype(BF)
    return pl.pallas_call(
        _ffn_kernel,
        grid=(m // tm, ff // tf),
        in_specs=[pl.BlockSpec((tm, d), lambda i, f: (i, 0)),
                  pl.BlockSpec((tm, d), lambda i, f: (i, 0)),
                  pl.BlockSpec((None, 6, d), lambda i, f: (mod_idx(i), 0, 0)),
                  pl.BlockSpec((1, d), lambda i, f: (0, 0)),
                  pl.BlockSpec((d, 2 * tf), lambda i, f: (0, f)),
                  pl.BlockSpec((tf, d), lambda i, f: (f, 0))],
        out_specs=pl.BlockSpec((tm, d), lambda i, f: (i, 0)),
        out_shape=jax.ShapeDtypeStruct((m, d), F32),
        compiler_params=_cparams("parallel", "arbitrary"),
        name="ffn",
    )(h2, x1, mod, g, w13, w2)


def _moe_kernel(blk_e_ref, nused_ref, x_ref, w1_ref, w3_ref, w2_ref, o_ref, xb):
    i = pl.program_id(0)
    f = pl.program_id(1)
    used = i < nused_ref[0]

    @pl.when(f == 0)
    def _():
        o_ref[...] = jnp.zeros_like(o_ref)
        xb[...] = x_ref[...].astype(BF)

    @pl.when(used)
    def _():
        o_ref[...] += _swiglu_step(xb[...], w1_ref[...].astype(BF), w3_ref[...].astype(BF), w2_ref[...].astype(BF))


def _moe_call(blk_e, nused, xs, w1, w3, w2, tme):
    n_rows, d = xs.shape
    ff = w1.shape[2]
    tf = 512
    return pl.pallas_call(
        _moe_kernel,
        grid_spec=pltpu.PrefetchScalarGridSpec(
            num_scalar_prefetch=2,
            grid=(n_rows // tme, ff // tf),
            in_specs=[pl.BlockSpec((tme, d), lambda i, f, be, nu: (i, 0)),
                      pl.BlockSpec((None, d, tf), lambda i, f, be, nu: (be[i], 0, f)),
                      pl.BlockSpec((None, d, tf), lambda i, f, be, nu: (be[i], 0, f)),
                      pl.BlockSpec((None, tf, d), lambda i, f, be, nu: (be[i], f, 0))],
            out_specs=pl.BlockSpec((tme, d), lambda i, f, be, nu: (i, 0)),
            scratch_shapes=[pltpu.VMEM((tme, d), BF)]),
        out_shape=jax.ShapeDtypeStruct((n_rows, d), F32),
        compiler_params=_cparams("arbitrary", "arbitrary"),
        name="moe",
    )(blk_e, nused, xs, w1, w3, w2)


SC_CORES = 2
SC_SUBCORES = 16
SC_GATHER_ROWS = 64


def _sc_gather_rows(table, idx):
    n_idx = idx.shape[0]
    d = table.shape[1]
    workers = SC_CORES * SC_SUBCORES
    per_w = n_idx // workers
    rows = min(SC_GATHER_ROWS, per_w)
    assert n_idx % (8 * workers) == 0 and per_w % rows == 0
    mesh = plsc.VectorSubcoreMesh(core_axis_name="c", subcore_axis_name="s",
                                  num_cores=SC_CORES, num_subcores=SC_SUBCORES)

    def body(table_hbm, idx_hbm, out_hbm, idx_v, rows_v, sem):
        wid = lax.axis_index("s") * SC_CORES + lax.axis_index("c")
        base = wid * per_w

        @pl.loop(0, per_w // rows)
        def _(it):
            off = pl.multiple_of(base + it * rows, 8)
            pltpu.sync_copy(idx_hbm.at[pl.ds(off, rows)], idx_v)
            pltpu.async_copy(table_hbm.at[idx_v], rows_v, sem).wait()
            pltpu.sync_copy(rows_v, out_hbm.at[pl.ds(off, rows)])

    return pl.kernel(
        body,
        out_type=jax.ShapeDtypeStruct((n_idx, d), table.dtype),
        mesh=mesh,
        scratch_types=[pltpu.VMEM((rows,), jnp.int32), pltpu.VMEM((rows, d), table.dtype),
                       pltpu.SemaphoreType.DMA],
        name="sc_gather",
    )(table, idx)


def _combine_kernel(ya_ref, yb_ref, rt_ref, x_ref, mod_ref, g_ref, o_ref):
    g1 = rt_ref[:, 2:3]
    g2 = rt_ref[:, 3:4]
    f = g1 * ya_ref[...].astype(F32) + g2 * yb_ref[...].astype(F32)
    o_ref[...] = x_ref[...] + mod_ref[5:6, :] * _rms(f, g_ref[...])


def _combine_call(ya, yb, rt, x1, mod, g, tm, mod_idx):
    n, d = x1.shape
    return pl.pallas_call(
        _combine_kernel,
        grid=(n // tm,),
        in_specs=[pl.BlockSpec((tm, d), lambda i: (i, 0)),
                  pl.BlockSpec((tm, d), lambda i: (i, 0)),
                  pl.BlockSpec((tm, LANE), lambda i: (i, 0)),
                  pl.BlockSpec((tm, d), lambda i: (i, 0)),
                  pl.BlockSpec((None, 6, d), lambda i: (mod_idx(i), 0, 0)),
                  pl.BlockSpec((1, d), lambda i: (0, 0))],
        out_specs=pl.BlockSpec((tm, d), lambda i: (i, 0)),
        out_shape=jax.ShapeDtypeStruct((n, d), F32),
        compiler_params=_cparams("parallel"),
        name="combine",
    )(ya, yb, rt, x1, mod, g)


def _route_plan(rt, tme):
    n = rt.shape[0]
    e_flat = rt[:, :TOP_K].astype(jnp.int32).reshape(-1)
    onehot = (e_flat[:, None] == jnp.arange(N_EXPERTS, dtype=jnp.int32)[None, :]).astype(jnp.int32)
    csum = jnp.cumsum(onehot, axis=0)
    rank = jnp.sum(onehot * csum, axis=1) - 1
    counts = csum[-1]
    padded = (counts + tme - 1) // tme * tme
    pad_end = jnp.cumsum(padded)
    pad_start = pad_end - padded
    dest = pad_start[e_flat] + rank
    n_rows = n * TOP_K + N_EXPERTS * tme
    n_blk = n_rows // tme
    src_tok = jnp.zeros((n_rows,), jnp.int32).at[dest].set(jnp.arange(n * TOP_K, dtype=jnp.int32) // TOP_K)
    blk_start = jnp.arange(n_blk, dtype=jnp.int32) * tme
    blk_e = jnp.minimum(jnp.sum((pad_end[None, :] <= blk_start[:, None]).astype(jnp.int32), axis=1), N_EXPERTS - 1)
    nused = (pad_end[-1] // tme).astype(jnp.int32).reshape(1)
    return dest, src_tok, blk_e, nused


def _swap_perm():
    j = np.arange(QK_ROPE)
    axis, half, f = j // (2 * ROPE_FREQS), (j % (2 * ROPE_FREQS)) // ROPE_FREQS, j % ROPE_FREQS
    return axis * 2 * ROPE_FREQS + (1 - half) * ROPE_FREQS + f


def _rope_tables(seq, ctx_len):
    rows = seq // GRID_W
    row = jnp.repeat(jnp.arange(rows, dtype=F32), GRID_W)
    col = (jnp.arange(rows * GRID_W) % GRID_W).astype(F32)
    inv = ROPE_THETA ** (-jnp.arange(ROPE_FREQS, dtype=F32) / ROPE_FREQS)
    ang = jnp.stack([row[:, None] * inv, col[:, None] * inv], axis=1)
    cos, sin = jnp.cos(ang), jnp.sin(ang)
    cos64 = jnp.stack([cos, cos], axis=2).reshape(seq, QK_ROPE)
    sin64 = jnp.stack([-sin, sin], axis=2).reshape(seq, QK_ROPE)
    pad = jnp.zeros((seq, LANE - QK_ROPE), F32)
    cos_t = jnp.concatenate([cos64, pad], axis=1)
    sin_t = jnp.concatenate([sin64, pad], axis=1)
    ident = jnp.concatenate([jnp.ones((ctx_len, QK_ROPE), F32), jnp.zeros((ctx_len, LANE - QK_ROPE), F32)], axis=1)
    return jnp.concatenate([cos_t, ident], axis=0), jnp.concatenate([sin_t, jnp.zeros((ctx_len, LANE), F32)], axis=0)


def _prep_w_in(w):
    d = w.shape[0]
    pw = lw = d
    o = np.cumsum([0, pw, lw, lw, Q_LORA, KV_LORA, QK_ROPE, 3 * d])
    pool, lx, lg, cq, ckv, kr, gt = (w[:, o[i]:o[i + 1]] for i in range(7))
    cols = [pool, lx, lg, gt, cq, ckv, kr, kr[:, _swap_perm()]]
    n = sum(c.shape[1] for c in cols)
    cols.append(jnp.zeros((d, -n % INPROJ_TN), w.dtype))
    return jnp.concatenate(cols, axis=1).astype(BF)


def _prep_w_uq(w):
    qk = QK_NOPE + QK_ROPE
    w = w.reshape(w.shape[0], HEADS, qk)
    rope = w[:, :, QK_NOPE:]
    return jnp.concatenate([w, rope[:, :, _swap_perm()]], axis=2).reshape(w.shape[0], HEADS * HEAD_W).T.astype(BF)


def kernel(x, c, ctx, c_ctx, mod_w, mod_b, pre_mix_g, post_mix_g, pre_ffn_g, post_ffn_g, w_in, pool_w, pool_scale,
           pool_proj, conv_w, conv_b, gate_a_w, gate_a_b, gate_x_w, gate_x_b, lru_lambda, lru_proj, q_norm_g, w_uq,
           kv_norm_g, w_ukv, mla_proj, w_out, ffn_w1, ffn_w3, ffn_w2, router_w, moe_w1, moe_w3, moe_w2):
    bsz, seq, d = x.shape
    ctx_len = ctx.shape[1]
    depth = mod_w.shape[0]
    assert ctx_len == SEQ_TILE and seq % SEQ_TILE == 0 and seq % GRID_W == 0
    n_lat = bsz * seq
    n_ctx = bsz * ctx_len
    n_all = n_lat + n_ctx
    tm = min(1024, seq, n_ctx)
    assert seq % tm == 0 and n_ctx % tm == 0
    tm_merge = tm // 2

    def mod_idx_for(rows):
        return lambda i: jnp.where(i < n_lat // rows, i // (seq // rows), bsz)

    mod_idx = mod_idx_for(tm)

    cc = jnp.concatenate([c, c_ctx[None, :], jnp.zeros((8 - bsz - 1, d), F32)], axis=0)
    mods = _mod_call(cc, mod_w, mod_b)[:, :bsz + 1].reshape(depth, bsz + 1, 6, d)
    cos_t, sin_t = _rope_tables(seq, ctx_len)
    x_all = jnp.concatenate([x.reshape(n_lat, d), ctx.reshape(n_ctx, d)], axis=0)
    row1 = lambda v: v.reshape(1, -1)

    for l in range(depth):
        last = l == depth - 1
        n_out = n_lat if last else n_all
        mod = mods[l]
        z = _inproj_call(x_all, mod, row1(pre_mix_g[l]), _prep_w_in(w_in[l]), tm, mod_idx)
        wkv = w_ukv[l].reshape(KV_LORA, HEADS, QK_NOPE + V_DIM)
        wk = wkv[:, :, :QK_NOPE].reshape(KV_LORA, HEADS * QK_NOPE).astype(BF)
        wvt = wkv[:, :, QK_NOPE:].reshape(KV_LORA, HEADS * V_DIM).T.astype(BF)
        q, k, v = _qkv_call(z, cos_t, sin_t, row1(q_norm_g[l]), row1(kv_norm_g[l]), _prep_w_uq(w_uq[l]),
                            wk, wvt, bsz, seq, ctx_len, 6 * d)
        tq = min(512, seq)
        tk = 768 if (seq + ctx_len) % 768 == 0 else SEQ_TILE
        o = _attn_call(q, k, v, bsz=bsz, row_blk0=0, nq=seq // tq, tq=tq, kblk=0, klen=seq + ctx_len, tk=tk, kp=tk)
        if not last:
            o_ctx = _attn_call(q, k, v, bsz=bsz, row_blk0=n_lat // ctx_len, nq=1, tq=ctx_len,
                               kblk=seq // ctx_len, klen=ctx_len, tk=ctx_len, kp=ctx_len)
            o = jnp.concatenate([o, o_ctx], axis=0)
        mp = _pool_call(z, pool_w[l].astype(BF), row1(pool_scale[l]), bsz, seq, ctx_len, n_out)
        hs = []
        for dr in range(2):
            hs.append(_lru_call(z, conv_w[l], row1(conv_b[l]), gate_a_w[l, dr].astype(BF), row1(gate_a_b[l, dr]),
                                gate_x_w[l, dr].astype(BF), row1(gate_x_b[l, dr]), row1(lru_lambda[l, dr]),
                                bsz, seq, dr == 1))
        moe_layer = l % 2 == 1
        rw = None
        if moe_layer:
            rw = router_w[l // 2].T
        outs = _merge_call(x_all, mod, mp, hs[0], hs[1], z, o, pool_proj[l].astype(BF), lru_proj[l].astype(BF),
                           mla_proj[l].astype(BF), w_out[l].astype(BF), row1(post_mix_g[l]), row1(pre_ffn_g[l]),
                           rw, tm_merge, n_out, mod_idx_for(tm_merge))
        if not moe_layer:
            x1, h2 = outs
            x_all = _ffn_call(h2, x1, mod, row1(post_ffn_g[l]), ffn_w1[l // 2], ffn_w3[l // 2], ffn_w2[l // 2],
                              tm, mod_idx)
        else:
            x1, h2, rt = outs
            tme = min(1024, n_out * TOP_K // N_EXPERTS)
            dest, src_tok, blk_e, nused = _route_plan(rt, tme)
            xs = _sc_gather_rows(h2, src_tok)
            y = _moe_call(blk_e, nused, xs, moe_w1[l // 2], moe_w3[l // 2], moe_w2[l // 2], tme)
            dest2 = dest.reshape(n_out, TOP_K)
            ya = _sc_gather_rows(y, dest2[:, 0])
            yb = _sc_gather_rows(y, dest2[:, 1])
            x_all = _combine_call(ya, yb, rt, x1, mod, row1(post_ffn_g[l]), tm, mod_idx)
    return x_all[:n_lat].reshape(bsz, seq, d)
```

```python
import functools
import math

import numpy as np
import jax
import jax.numpy as jnp
from jax import lax
from jax.experimental import pallas as pl
from jax.experimental.pallas import tpu as pltpu
from jax.experimental.pallas import tpu_sc as plsc

BF = jnp.bfloat16
F32 = jnp.float32

RMS_EPS = 1e-6
GRID_W = 64
POOL_WINDOWS = (2, 4, 8, 16)
LRU_BLOCKS = 8
CONV_W = 4
LRU_C = 8.0
HEADS = 8
Q_LORA = 384
KV_LORA = 256
QK_NOPE = 128
QK_ROPE = 64
V_DIM = 128
MLA_SCALE = (QK_NOPE + QK_ROPE) ** -0.5
Q_SCALE = MLA_SCALE * math.log2(math.e)
ROPE_FREQS = QK_ROPE // 4
ROPE_THETA = 10000.0
N_EXPERTS = 8
TOP_K = 2

LANE = 128
HALO = 16
SEQ_TILE = 256
HEAD_W = 256
INPROJ_TN = 1024
FFN_TF = 768
VMEM_LIMIT = 48 * 1024 * 1024


def _cparams(*sem, flags=None):
    return pltpu.CompilerParams(dimension_semantics=sem, vmem_limit_bytes=VMEM_LIMIT, flags=flags)


def _rms(x, g):
    ms = jnp.mean(x * x, axis=-1, keepdims=True)
    return x * lax.rsqrt(ms + RMS_EPS) * g


def _sigmoid(x):
    return 0.5 * jnp.tanh(0.5 * x) + 0.5


def _silu(x):
    return x * _sigmoid(x)


def _gelu_tanh(x):
    return 0.5 * x * (1.0 + jnp.tanh(math.sqrt(2.0 / math.pi) * (x + 0.044715 * (x * x * x))))


def _dot(a, b):
    return jnp.dot(a, b, preferred_element_type=F32)


def _mod_kernel(c_ref, w_ref, b_ref, o_ref):
    s = _silu(c_ref[...])
    o_ref[...] = _dot(s.astype(BF), w_ref[...].astype(BF)) + b_ref[...]


def _mod_call(cc, mod_w, mod_b):
    depth, d, n6 = mod_w.shape
    tn = 1536
    return pl.pallas_call(
        _mod_kernel,
        grid=(depth, n6 // tn),
        in_specs=[pl.BlockSpec((8, d), lambda l, j: (0, 0)),
                  pl.BlockSpec((None, d, tn), lambda l, j: (l, 0, j)),
                  pl.BlockSpec((None, 1, tn), lambda l, j: (l, 0, j))],
        out_specs=pl.BlockSpec((None, 8, tn), lambda l, j: (l, 0, j)),
        out_shape=jax.ShapeDtypeStruct((depth, 8, n6), F32),
        compiler_params=_cparams("parallel", "arbitrary"),
        name="mod",
    )(cc, mod_w, mod_b.reshape(depth, 1, n6))


def _inproj_kernel(x_ref, mod_ref, g_ref, w_ref, z_ref, h_scr):
    @pl.when(pl.program_id(1) == 0)
    def _():
        h = _rms(x_ref[...], g_ref[...])
        h = h * (1.0 + mod_ref[1:2, :]) + mod_ref[0:1, :]
        h_scr[...] = h.astype(BF)

    z_ref[...] = _dot(h_scr[...], w_ref[...]).astype(BF)


def _inproj_call(x, mod, g, w, tm, mod_idx):
    m, d = x.shape
    n = w.shape[1]
    tn = INPROJ_TN
    return pl.pallas_call(
        _inproj_kernel,
        grid=(m // tm, n // tn),
        in_specs=[pl.BlockSpec((tm, d), lambda i, j: (i, 0)),
                  pl.BlockSpec((None, 6, d), lambda i, j: (mod_idx(i), 0, 0)),
                  pl.BlockSpec((1, d), lambda i, j: (0, 0)),
                  pl.BlockSpec((d, tn), lambda i, j: (0, j))],
        out_specs=pl.BlockSpec((tm, tn), lambda i, j: (i, j)),
        out_shape=jax.ShapeDtypeStruct((m, n), BF),
        scratch_shapes=[pltpu.VMEM((tm, d), BF)],
        compiler_params=_cparams("parallel", "arbitrary"),
        name="inproj",
    )(x, mod, g, w)


def _rope(x, cos, sin):
    return x * cos + pltpu.roll(x, LANE // 2, 1) * sin


_NT = (((1,), (1,)), ((), ()))


def _qkv_kernel(z_ref, cos_ref, sin_ref, cost_ref, sint_ref, qg_ref, kvg_ref, wuqt_ref, wk_ref, wvt_ref,
                qt_ref, k_ref, vt_ref):
    z = z_ref[...]
    cq = z[:, :Q_LORA].astype(F32)
    ckv = z[:, Q_LORA:Q_LORA + KV_LORA].astype(F32)
    kr = z[:, Q_LORA + KV_LORA:].astype(F32)
    cqn = _rms(cq, qg_ref[...]).astype(BF)
    ckvn = _rms(ckv, kvg_ref[...]).astype(BF)
    qt = lax.dot_general(wuqt_ref[...], cqn, _NT, preferred_element_type=F32)
    kn = _dot(ckvn, wk_ref[...])
    vt = lax.dot_general(wvt_ref[...], ckvn, _NT, preferred_element_type=F32)
    krot = _rope(kr, cos_ref[...], sin_ref[...]).astype(BF)
    cost = cost_ref[...]
    sint = sint_ref[...]
    for h in range(HEADS):
        c0 = h * HEAD_W
        c1 = c0 + QK_NOPE
        c2 = c1 + QK_ROPE
        qt_ref[c0:c1, :] = (qt[c0:c1, :] * Q_SCALE).astype(BF)
        qt_ref[c1:c2, :] = ((qt[c1:c2, :] * cost + qt[c2:c0 + HEAD_W, :] * sint) * Q_SCALE).astype(BF)
        qt_ref[c2:c0 + HEAD_W, :] = jnp.zeros((QK_ROPE, qt.shape[1]), BF)
        k_ref[h, :, 0:QK_NOPE] = kn[:, h * QK_NOPE:(h + 1) * QK_NOPE].astype(BF)
        k_ref[h, :, QK_NOPE:HEAD_W] = krot
        vt_ref[h] = vt[h * V_DIM:(h + 1) * V_DIM, :].astype(BF)


def _qkv_call(z, cos_t, sin_t, qg, kvg, wuqt, wk, wvt, bsz, seq, ctx_len, z_off):
    m = z.shape[0]
    cos_tt = cos_t[:, :QK_ROPE].T
    sin_tt = sin_t[:, :QK_ROPE].T
    ts = SEQ_TILE
    nt = seq // ts
    nlat = bsz * nt
    lk = seq + ctx_len
    zw = Q_LORA + KV_LORA + LANE
    assert z_off % zw == 0
    zcol = z_off // zw

    def tab_idx(i):
        return (jnp.where(i < nlat, i % nt, nt), 0)

    def kv_idx(i):
        return (jnp.where(i < nlat, i // nt, i - nlat), 0, jnp.where(i < nlat, i % nt, nt), 0)

    def vt_idx(i):
        return (jnp.where(i < nlat, i // nt, i - nlat), 0, 0, jnp.where(i < nlat, i % nt, nt))

    return pl.pallas_call(
        _qkv_kernel,
        grid=(m // ts,),
        in_specs=[pl.BlockSpec((ts, zw), lambda i: (i, zcol)),
                  pl.BlockSpec((ts, LANE), tab_idx),
                  pl.BlockSpec((ts, LANE), tab_idx),
                  pl.BlockSpec((QK_ROPE, ts), lambda i: tab_idx(i)[::-1]),
                  pl.BlockSpec((QK_ROPE, ts), lambda i: tab_idx(i)[::-1]),
                  pl.BlockSpec((1, Q_LORA), lambda i: (0, 0)),
                  pl.BlockSpec((1, KV_LORA), lambda i: (0, 0)),
                  pl.BlockSpec(wuqt.shape, lambda i: (0, 0)),
                  pl.BlockSpec(wk.shape, lambda i: (0, 0)),
                  pl.BlockSpec(wvt.shape, lambda i: (0, 0))],
        out_specs=[pl.BlockSpec((HEADS * HEAD_W, ts), lambda i: (0, i)),
                   pl.BlockSpec((None, HEADS, ts, HEAD_W), kv_idx),
                   pl.BlockSpec((None, HEADS, V_DIM, ts), vt_idx)],
        out_shape=[jax.ShapeDtypeStruct((HEADS * HEAD_W, m), BF),
                   jax.ShapeDtypeStruct((bsz, HEADS, lk, HEAD_W), BF),
                   jax.ShapeDtypeStruct((bsz, HEADS, V_DIM, lk), BF)],
        compiler_params=_cparams("parallel"),
        name="qkv_up",
    )(z, cos_t, sin_t, cos_tt, sin_tt, qg, kvg, wuqt, wk, wvt)


def _col_reduce(x, pair, red):
    n = x.shape[0] // 4
    a = pair(pair(x[0:n], x[n:2 * n]), pair(x[2 * n:3 * n], x[3 * n:4 * n]))
    return red(a, axis=0, keepdims=True)


ATTN_LAG_LIMIT = 20.0


def _attn_kernel(qt_ref, k_ref, vt_ref, o_ref, p_scr, *, kp, nk):
    qt = qt_ref[...]
    tq = qt.shape[1]

    def scores(c):
        return _dot(k_ref[c * kp:(c + 1) * kp, :], qt)

    def values(c):
        return _dot(vt_ref[:, c * kp:(c + 1) * kp], p_scr[c % 2])

    s = scores(0)
    ref = _col_reduce(s, jnp.maximum, jnp.max)
    p = jnp.exp2(s - ref)
    l = _col_reduce(p, jnp.add, jnp.sum)
    p_scr[0] = p.astype(BF)
    acc = alpha = None
    lag = jnp.zeros((1, tq), F32)
    for c in range(1, nk):
        s = scores(c)
        p = jnp.exp2(s - ref)
        mc = _col_reduce(s, jnp.maximum, jnp.max)
        lc = _col_reduce(p, jnp.add, jnp.sum)
        p_scr[c % 2] = p.astype(BF)
        pv = values(c - 1)
        acc = pv if acc is None else acc + pv
        if alpha is not None:
            acc = acc * alpha
            l = l * alpha
        l = l + lc
        lag = jnp.maximum(lag, mc - ref)
        new_ref = jnp.maximum(ref, mc)
        alpha = jnp.exp2(ref - new_ref)
        ref = new_ref
    pv = values(nk - 1)
    acc = pv if acc is None else acc + pv
    over = jnp.max(lag) > ATTN_LAG_LIMIT

    @pl.when(jnp.logical_not(over))
    def _():
        o_ref[...] = (acc / l).T.astype(o_ref.dtype)

    @pl.when(over)
    def _():
        def body(j, carry):
            m, lsum, a = carry
            r0 = pl.multiple_of(j * kp, kp)
            sj = _dot(k_ref[pl.ds(r0, kp), :], qt)
            m_new = jnp.maximum(m, jnp.max(sj, axis=0, keepdims=True))
            al = jnp.exp2(m - m_new)
            pj = jnp.exp2(sj - m_new)
            lsum = al * lsum + jnp.sum(pj, axis=0, keepdims=True)
            a = al * a + _dot(vt_ref[:, pl.ds(r0, kp)], pj.astype(BF))
            return m_new, lsum, a

        init = (jnp.full((1, tq), -1e30, F32), jnp.zeros((1, tq), F32), jnp.zeros((V_DIM, tq), F32))
        _, lsum, a = lax.fori_loop(0, nk, body, init)
        o_ref[...] = (a / lsum).T.astype(o_ref.dtype)


def _attn_call(q, k, v, *, bsz, row_blk0, nq, tq, kblk, klen, tk, kp):
    assert kp == tk
    kern = functools.partial(_attn_kernel, kp=kp, nk=klen // tk)
    return pl.pallas_call(
        kern,
        grid=(bsz, HEADS, nq),
        in_specs=[pl.BlockSpec((HEAD_W, tq), lambda b, h, i: (h, row_blk0 + b * nq + i)),
                  pl.BlockSpec((None, None, klen, HEAD_W), lambda b, h, i: (b, h, kblk, 0)),
                  pl.BlockSpec((None, None, V_DIM, klen), lambda b, h, i: (b, h, 0, kblk))],
        out_specs=pl.BlockSpec((tq, V_DIM), lambda b, h, i: (b * nq + i, h)),
        out_shape=jax.ShapeDtypeStruct((bsz * nq * tq, HEADS * V_DIM), BF),
        scratch_shapes=[pltpu.VMEM((2, tk, tq), BF)],
        compiler_params=_cparams("parallel", "parallel", "arbitrary"),
        name="attn",
    )(q, k, v)


def _seq_flags(i, nlat, nt):
    is_ctx = i >= nlat
    t = jnp.where(is_ctx, 0, i % nt)
    first = jnp.logical_or(is_ctx, t == 0)
    last = jnp.logical_or(is_ctx, t == nt - 1)
    return is_ctx, t, first, last


def _pool_kernel(x_ref, xp_ref, xn_ref, pw_ref, ps_ref, o_ref, *, nlat, nt, seq, ctx_len):
    ts = x_ref.shape[0]
    is_ctx, t, first, last = _seq_flags(pl.program_id(0), nlat, nt)
    seq_len = jnp.where(is_ctx, ctx_len, seq)
    x = x_ref[...]
    xp = jnp.where(first, jnp.zeros_like(xp_ref[...]), xp_ref[...])
    xn = jnp.where(last, jnp.zeros_like(xn_ref[...]), xn_ref[...])
    xe = jnp.concatenate([xp, x, xn], axis=0)
    tpos = t * ts + lax.broadcasted_iota(jnp.int32, (ts, 1), 0)
    rel = (lax.broadcasted_iota(jnp.int32, (ts, ts + 2 * HALO), 1) - HALO
           - lax.broadcasted_iota(jnp.int32, (ts, ts + 2 * HALO), 0))
    gw = x.shape[1] // len(POOL_WINDOWS)
    for g, w in enumerate(POOL_WINDOWS):
        cs = slice(g * gw, (g + 1) * gw)
        band = jnp.where(rel >= -(w // 2), jnp.where(rel < w - w // 2, 1.0, 0.0), 0.0).astype(BF)
        s = _dot(band, xe[:, cs])
        cnt = (jnp.minimum(tpos + (w - w // 2), seq_len) - jnp.maximum(tpos - w // 2, 0)).astype(F32)
        mean_minus = s / cnt - x[:, cs].astype(F32)
        o_ref[:, cs] = (_dot(mean_minus.astype(BF), pw_ref[g]) * ps_ref[:, cs]).astype(BF)


def _halo_specs(ts, width, col, row_blk, m):
    r = ts // HALO
    nh = m // HALO
    prev = pl.BlockSpec((HALO, width), lambda *a: (jnp.maximum(row_blk(*a) * r - 1, 0), col))
    nxt = pl.BlockSpec((HALO, width), lambda *a: (jnp.minimum((row_blk(*a) + 1) * r, nh - 1), col))
    return prev, nxt


def _pool_call(z, pool_w, pool_scale, bsz, seq, ctx_len, n_rows):
    m = z.shape[0]
    ts = SEQ_TILE
    nt = seq // ts
    nlat = bsz * nt
    width = pool_scale.shape[1]
    prev, nxt = _halo_specs(ts, width, 0, lambda i: i, m)
    kern = functools.partial(_pool_kernel, nlat=nlat, nt=nt, seq=seq, ctx_len=ctx_len)
    return pl.pallas_call(
        kern,
        grid=(n_rows // ts,),
        in_specs=[pl.BlockSpec((ts, width), lambda i: (i, 0)), prev, nxt,
                  pl.BlockSpec(pool_w.shape, lambda i: (0, 0, 0)),
                  pl.BlockSpec((1, width), lambda i: (0, 0))],
        out_specs=pl.BlockSpec((ts, width), lambda i: (i, 0)),
        out_shape=jax.ShapeDtypeStruct((n_rows, width), BF),
        compiler_params=_cparams("parallel"),
        name="pool",
    )(z, z, z, pool_w, pool_scale)


def _lru_kernel(x_ref, xp_ref, xn_ref, cw_ref, cb_ref, wa_ref, ba_ref, wx_ref, bx_ref, lam_ref, o_ref,
                xf_scr, hl_scr, ca_scr, ga_scr, gb_scr, hp_scr, h_scr, *, reverse, nt):
    ts, width = x_ref.shape
    ng = ts // 8
    bw = width // LRU_BLOCKS
    s = pl.program_id(1)
    t = (nt - s) if reverse else (s - 1)
    first = jnp.logical_or(s == 0, t == 0)
    last = jnp.logical_or(s == 0, t == nt - 1)

    @pl.when(s == 0)
    def _():
        h_scr[...] = jnp.zeros_like(h_scr)

    left = CONV_W // 2
    xp = xp_ref[...].astype(F32)[HALO - 8:HALO]
    xn = xn_ref[...].astype(F32)[0:8]
    xp = jnp.where(first, jnp.zeros_like(xp), xp)
    xn = jnp.where(last, jnp.zeros_like(xn), xn)
    lam = lam_ref[...]
    neg = -lam
    softplus = jnp.maximum(neg, 0.0) + jnp.log1p(jnp.exp(-jnp.abs(neg)))
    coef = (-LRU_C * math.log2(math.e)) * softplus
    row = lax.broadcasted_iota(jnp.int32, (ng, bw), 0)
    slab = lambda n, j: (n, pl.ds(j, ng, stride=8), slice(None))
    order = range(7, -1, -1) if reverse else range(8)

    for n in range(LRU_BLOCKS):
        cs = slice(n * bw, (n + 1) * bw)
        xf_scr[n] = x_ref[:, cs].astype(F32)
        xs = {j: xf_scr[slab(n, j)] for j in range(8)}
        for j in range(-left, 0):
            xs[j] = jnp.where(row == 0, xp[8 + j:9 + j, cs], pltpu.roll(xs[8 + j], 1, 0))
        for j in range(8, 8 + CONV_W - 1 - left):
            xs[j] = jnp.where(row == ng - 1, xn[j - 8:j - 7, cs], pltpu.roll(xs[j - 8], ng - 1, 0))
        us = []
        for j in range(8):
            u = cb_ref[:, cs] + cw_ref[0:1, cs] * xs[j - left]
            for k in range(1, CONV_W):
                u = u + cw_ref[k:k + 1, cs] * xs[j - left + k]
            us.append(u)
        un = jnp.concatenate(us, axis=0)
        ub = un.astype(BF)
        r = _sigmoid(_dot(ub, wa_ref[n]) + ba_ref[:, cs])
        gi = _sigmoid(_dot(ub, wx_ref[n]) + bx_ref[:, cs])
        a = jnp.exp2(r * coef[:, cs])
        om = 1.0 - a * a
        b = (om * lax.rsqrt(jnp.maximum(om, 1e-30))) * (gi * un)
        hl = ca = None
        for j in order:
            aj = a[j * ng:(j + 1) * ng]
            bj = b[j * ng:(j + 1) * ng]
            hl, ca = (bj, aj) if hl is None else (aj * hl + bj, aj * ca)
            hl_scr[n, j * ng:(j + 1) * ng, :] = hl
            ca_scr[n, j * ng:(j + 1) * ng, :] = ca
        ga_scr[pl.ds(n, ng, stride=LRU_BLOCKS), :] = ca
        gb_scr[pl.ds(n, ng, stride=LRU_BLOCKS), :] = hl

    h = h_scr[...]
    for g in (range(ng - 1, -1, -1) if reverse else range(ng)):
        gs = slice(g * LRU_BLOCKS, (g + 1) * LRU_BLOCKS)
        hp_scr[gs, :] = h
        h = ga_scr[gs, :] * h + gb_scr[gs, :]
    h_scr[...] = h

    for n in range(LRU_BLOCKS):
        hp = hp_scr[pl.ds(n, ng, stride=LRU_BLOCKS), :]
        for j in range(8):
            xf_scr[slab(n, j)] = hl_scr[n, j * ng:(j + 1) * ng, :] + ca_scr[n, j * ng:(j + 1) * ng, :] * hp
        o_ref[:, n * bw:(n + 1) * bw] = xf_scr[n].astype(o_ref.dtype)


def _lru_call(z, conv_w, conv_b, wa, ba, wx, bx, lam, bsz, seq, reverse):
    m = z.shape[0]
    ts = SEQ_TILE
    nt = seq // ts
    nlat = bsz * nt
    width = conv_b.shape[1]

    def row_blk(b, s):
        t = (nt - s) if reverse else (s - 1)
        return jnp.where(s == 0, nlat + b, b * nt + t)

    prev, nxt = _halo_specs(ts, width, 1, row_blk, m)
    vec = lambda shape: pl.BlockSpec(shape, lambda b, s: (0,) * len(shape))
    kern = functools.partial(_lru_kernel, reverse=reverse, nt=nt)
    return pl.pallas_call(
        kern,
        grid=(bsz, nt + 1),
        in_specs=[pl.BlockSpec((ts, width), lambda b, s: (row_blk(b, s), 1)), prev, nxt,
                  vec(conv_w.shape), vec(conv_b.shape), vec(wa.shape), vec(ba.shape),
                  vec(wx.shape), vec(bx.shape), vec(lam.shape)],
        out_specs=pl.BlockSpec((ts, width), lambda b, s: (row_blk(b, s), 0)),
        out_shape=jax.ShapeDtypeStruct((m, width), BF),
        scratch_shapes=[pltpu.VMEM((LRU_BLOCKS, ts, width // LRU_BLOCKS), F32)] * 3
        + [pltpu.VMEM((ts // 8 * LRU_BLOCKS, width // LRU_BLOCKS), F32)] * 3
        + [pltpu.VMEM((LRU_BLOCKS, width // LRU_BLOCKS), F32)],
        compiler_params=_cparams("parallel", "arbitrary"),
        name="lru_bwd" if reverse else "lru_fwd",
    )(z, z, z, conv_w, conv_b, wa, ba, wx, bx, lam)


def _merge_kernel(x_ref, mod_ref, mp_ref, hf_ref, hb_ref, lg_ref, o_ref, gt_ref, wp_ref, wl_ref, wm_ref, wo_ref,
                  g1_ref, g2_ref, *rest, route):
    if route:
        rw_ref, x1_ref, h2_ref, rt_ref = rest
    else:
        x1_ref, h2_ref = rest
    d = x_ref.shape[1]
    y_pool = _dot(mp_ref[...], wp_ref[...])
    lru_in = (hf_ref[...].astype(F32) + hb_ref[...].astype(F32)) * _gelu_tanh(lg_ref[...].astype(F32))
    y_lru = _dot(lru_in.astype(BF), wl_ref[...])
    y_mla = _dot(o_ref[...], wm_ref[...])
    mix = (_sigmoid(gt_ref[:, 0:d].astype(F32)) * y_pool
           + _sigmoid(gt_ref[:, d:2 * d].astype(F32)) * y_lru
           + _sigmoid(gt_ref[:, 2 * d:3 * d].astype(F32)) * y_mla)
    y = _dot(mix.astype(BF), wo_ref[...])
    x1 = x_ref[...] + mod_ref[2:3, :] * _rms(y, g1_ref[...])
    x1_ref[...] = x1
    h2 = _rms(x1, g2_ref[...]) * (1.0 + mod_ref[4:5, :]) + mod_ref[3:4, :]
    h2_ref[...] = h2.astype(h2_ref.dtype)
    if route:
        logit = [jnp.sum(h2 * rw_ref[e:e + 1, :], axis=1, keepdims=True) for e in range(N_EXPERTS)]
        v1, i1 = logit[0], jnp.zeros_like(logit[0])
        for e in range(1, N_EXPERTS):
            upd = logit[e] > v1
            v1 = jnp.where(upd, logit[e], v1)
            i1 = jnp.where(upd, float(e), i1)
        v2, i2 = jnp.full_like(v1, -jnp.inf), jnp.zeros_like(v1)
        for e in range(N_EXPERTS):
            cand = jnp.where(i1 == float(e), -jnp.inf, logit[e])
            upd = cand > v2
            v2 = jnp.where(upd, cand, v2)
            i2 = jnp.where(upd, float(e), i2)
        ex = jnp.exp(v2 - v1)
        gate1 = 1.0 / (1.0 + ex)
        gate2 = ex / (1.0 + ex)
        col = lax.broadcasted_iota(jnp.int32, rt_ref.shape, 1)
        rt_ref[...] = jnp.where(col == 0, i1, jnp.where(col == 1, i2,
                                jnp.where(col == 2, gate1, jnp.where(col == 3, gate2, 0.0))))


def _merge_call(x, mod, mp, hf, hb, z, o, wp, wl, wm, wo, g1, g2, rw, tm, n_rows, mod_idx):
    d = x.shape[1]
    route = rw is not None
    row = lambda c: pl.BlockSpec((tm, d), lambda i: (i, c))
    full = lambda a: pl.BlockSpec(a.shape, lambda i: (0,) * a.ndim)
    in_specs = [row(0), pl.BlockSpec((None, 6, d), lambda i: (mod_idx(i), 0, 0)),
                row(0), row(0), row(0), row(2), row(0), pl.BlockSpec((tm, 3 * d), lambda i: (i, 1)),
                full(wp), full(wl), full(wm), full(wo), full(g1), full(g2)]
    args = [x, mod, mp, hf, hb, z, o, z, wp, wl, wm, wo, g1, g2]
    out_specs = [row(0), row(0)]
    out_shape = [jax.ShapeDtypeStruct((n_rows, d), F32), jax.ShapeDtypeStruct((n_rows, d), F32 if route else BF)]
    if route:
        in_specs.append(full(rw))
        args.append(rw)
        out_specs.append(pl.BlockSpec((tm, LANE), lambda i: (i, 0)))
        out_shape.append(jax.ShapeDtypeStruct((n_rows, LANE), F32))
    return pl.pallas_call(
        functools.partial(_merge_kernel, route=route),
        grid=(n_rows // tm,),
        in_specs=in_specs, out_specs=out_specs, out_shape=out_shape,
        compiler_params=_cparams("parallel"),
        name="merge",
    )(*args)


def _swiglu_step(x, w1, w3, w2):
    a = _dot(x, w1)
    b = _dot(x, w3)
    return _dot((_silu(a) * b).astype(BF), w2)


def _ffn_kernel(h_ref, x_ref, mod_ref, g_ref, w13_ref, w2_ref, o_ref):
    f = pl.program_id(1)
    tf = w2_ref.shape[0]

    @pl.when(f == 0)
    def _():
        o_ref[...] = jnp.zeros_like(o_ref)

    ab = _dot(h_ref[...], w13_ref[...])
    o_ref[...] += _dot((_silu(ab[:, :tf]) * ab[:, tf:]).astype(BF), w2_ref[...])

    @pl.when(f == pl.num_programs(1) - 1)
    def _():
        o_ref[...] = x_ref[...] + mod_ref[5:6, :] * _rms(o_ref[...], g_ref[...])


def _ffn_call(h2, x1, mod, g, w1, w3, w2, tm, mod_idx):
    m, d = x1.shape
    tf = FFN_TF
    pad = -w1.shape[1] % tf
    w1, w3 = (jnp.pad(w, ((0, 0), (0, pad))) for w in (w1, w3))
    ff = w1.shape[1]
    w13 = jnp.stack([w1.reshape(d, ff // tf, tf), w3.reshape(d, ff // tf, tf)], axis=2).reshape(d, 2 * ff).astype(BF)
    w2 = jnp.pad(w2, ((0, pad), (0, 0))).astype(BF)
    return pl.pallas_call(
        _ffn_kernel,
        grid=(m // tm, ff // tf),
        in_specs=[pl.BlockSpec((tm, d), lambda i, f: (i, 0)),
                  pl.BlockSpec((tm, d), lambda i, f: (i, 0)),
                  pl.BlockSpec((None, 6, d), lambda i, f: (mod_idx(i), 0, 0)),
                  pl.BlockSpec((1, d), lambda i, f: (0, 0)),
                  pl.BlockSpec((d, 2 * tf), lambda i, f: (0, f)),
                  pl.BlockSpec((tf, d), lambda i, f: (f, 0))],
        out_specs=pl.BlockSpec((tm, d), lambda i, f: (i, 0)),
        out_shape=jax.ShapeDtypeStruct((m, d), F32),
        compiler_params=_cparams("parallel", "arbitrary"),
        name="ffn",
    )(h2, x1, mod, g, w13, w2)


def _moe_kernel(blk_e_ref, nused_ref, x_ref, w1_ref, w3_ref, w2_ref, o_ref, xb):
    i = pl.program_id(0)
    f = pl.program_id(1)
    used = i < nused_ref[0]

    @pl.when(f == 0)
    def _():
        o_ref[...] = jnp.zeros_like(o_ref)
        xb[...] = x_ref[...].astype(BF)

    @pl.when(used)
    def _():
        o_ref[...] += _swiglu_step(xb[...], w1_ref[...].astype(BF), w3_ref[...].astype(BF), w2_ref[...].astype(BF))


def _moe_call(blk_e, nused, xs, w1, w3, w2, tme):
    n_rows, d = xs.shape
    ff = w1.shape[2]
    tf = 512
    return pl.pallas_call(
        _moe_kernel,
        grid_spec=pltpu.PrefetchScalarGridSpec(
            num_scalar_prefetch=2,
            grid=(n_rows // tme, ff // tf),
            in_specs=[pl.BlockSpec((tme, d), lambda i, f, be, nu: (i, 0)),
                      pl.BlockSpec((None, d, tf), lambda i, f, be, nu: (be[i], 0, f)),
                      pl.BlockSpec((None, d, tf), lambda i, f, be, nu: (be[i], 0, f)),
                      pl.BlockSpec((None, tf, d), lambda i, f, be, nu: (be[i], f, 0))],
            out_specs=pl.BlockSpec((tme, d), lambda i, f, be, nu: (i, 0)),
            scratch_shapes=[pltpu.VMEM((tme, d), BF)]),
        out_shape=jax.ShapeDtypeStruct((n_rows, d), F32),
        compiler_params=_cparams("arbitrary", "arbitrary"),
        name="moe",
    )(blk_e, nused, xs, w1, w3, w2)


SC_CORES = 2
SC_SUBCORES = 16
SC_GATHER_ROWS = 64


def _sc_gather_rows(table, idx):
    n_idx = idx.shape[0]
    d = table.shape[1]
    workers = SC_CORES * SC_SUBCORES
    per_w = n_idx // workers
    rows = min(SC_GATHER_ROWS, per_w)
    assert n_idx % (8 * workers) == 0 and per_w % rows == 0
    mesh = plsc.VectorSubcoreMesh(core_axis_name="c", subcore_axis_name="s",
                                  num_cores=SC_CORES, num_subcores=SC_SUBCORES)

    def body(table_hbm, idx_hbm, out_hbm, idx_v, rows_v, sem):
        wid = lax.axis_index("s") * SC_CORES + lax.axis_index("c")
        base = wid * per_w

        @pl.loop(0, per_w // rows)
        def _(it):
            off = pl.multiple_of(base + it * rows, 8)
            pltpu.sync_copy(idx_hbm.at[pl.ds(off, rows)], idx_v)
            pltpu.async_copy(table_hbm.at[idx_v], rows_v, sem).wait()
            pltpu.sync_copy(rows_v, out_hbm.at[pl.ds(off, rows)])

    return pl.kernel(
        body,
        out_type=jax.ShapeDtypeStruct((n_idx, d), table.dtype),
        mesh=mesh,
        scratch_types=[pltpu.VMEM((rows,), jnp.int32), pltpu.VMEM((rows, d), table.dtype),
                       pltpu.SemaphoreType.DMA],
        name="sc_gather",
    )(table, idx)


def _combine_kernel(ya_ref, yb_ref, rt_ref, x_ref, mod_ref, g_ref, o_ref):
    g1 = rt_ref[:, 2:3]
    g2 = rt_ref[:, 3:4]
    f = g1 * ya_ref[...].astype(F32) + g2 * yb_ref[...].astype(F32)
    o_ref[...] = x_ref[...] + mod_ref[5:6, :] * _rms(f, g_ref[...])


def _combine_call(ya, yb, rt, x1, mod, g, tm, mod_idx):
    n, d = x1.shape
    return pl.pallas_call(
        _combine_kernel,
        grid=(n // tm,),
        in_specs=[pl.BlockSpec((tm, d), lambda i: (i, 0)),
                  pl.BlockSpec((tm, d), lambda i: (i, 0)),
                  pl.BlockSpec((tm, LANE), lambda i: (i, 0)),
                  pl.BlockSpec((tm, d), lambda i: (i, 0)),
                  pl.BlockSpec((None, 6, d), lambda i: (mod_idx(i), 0, 0)),
                  pl.BlockSpec((1, d), lambda i: (0, 0))],
        out_specs=pl.BlockSpec((tm, d), lambda i: (i, 0)),
        out_shape=jax.ShapeDtypeStruct((n, d), F32),
        compiler_params=_cparams("parallel"),
        name="combine",
    )(ya, yb, rt, x1, mod, g)


def _route_plan(rt, tme):
    n = rt.shape[0]
    e_flat = rt[:, :TOP_K].astype(jnp.int32).reshape(-1)
    onehot = (e_flat[:, None] == jnp.arange(N_EXPERTS, dtype=jnp.int32)[None, :]).astype(jnp.int32)
    csum = jnp.cumsum(onehot, axis=0)
    rank = jnp.sum(onehot * csum, axis=1) - 1
    counts = csum[-1]
    padded = (counts + tme - 1) // tme * tme
    pad_end = jnp.cumsum(padded)
    pad_start = pad_end - padded
    dest = pad_start[e_flat] + rank
    n_rows = n * TOP_K + N_EXPERTS * tme
    n_blk = n_rows // tme
    src_tok = (jnp.arange(n_rows, dtype=jnp.int32) % n).at[dest].set(jnp.arange(n * TOP_K, dtype=jnp.int32) // TOP_K)
    blk_start = jnp.arange(n_blk, dtype=jnp.int32) * tme
    blk_e = jnp.minimum(jnp.sum((pad_end[None, :] <= blk_start[:, None]).astype(jnp.int32), axis=1), N_EXPERTS - 1)
    nused = (pad_end[-1] // tme).astype(jnp.int32).reshape(1)
    return dest, src_tok, blk_e, nused


def _swap_perm():
    j = np.arange(QK_ROPE)
    axis, half, f = j // (2 * ROPE_FREQS), (j % (2 * ROPE_FREQS)) // ROPE_FREQS, j % ROPE_FREQS
    return axis * 2 * ROPE_FREQS + (1 - half) * ROPE_FREQS + f


def _rope_tables(seq, ctx_len):
    rows = seq // GRID_W
    row = jnp.repeat(jnp.arange(rows, dtype=F32), GRID_W)
    col = (jnp.arange(rows * GRID_W) % GRID_W).astype(F32)
    inv = ROPE_THETA ** (-jnp.arange(ROPE_FREQS, dtype=F32) / ROPE_FREQS)
    ang = jnp.stack([row[:, None] * inv, col[:, None] * inv], axis=1)
    cos, sin = jnp.cos(ang), jnp.sin(ang)
    cos64 = jnp.stack([cos, cos], axis=2).reshape(seq, QK_ROPE)
    sin64 = jnp.stack([-sin, sin], axis=2).reshape(seq, QK_ROPE)
    pad = jnp.zeros((seq, LANE - QK_ROPE), F32)
    cos_t = jnp.concatenate([cos64, pad], axis=1)
    sin_t = jnp.concatenate([sin64, pad], axis=1)
    ident = jnp.concatenate([jnp.ones((ctx_len, QK_ROPE), F32), jnp.zeros((ctx_len, LANE - QK_ROPE), F32)], axis=1)
    return jnp.concatenate([cos_t, ident], axis=0), jnp.concatenate([sin_t, jnp.zeros((ctx_len, LANE), F32)], axis=0)


def _prep_w_in(w):
    d = w.shape[0]
    pw = lw = d
    o = np.cumsum([0, pw, lw, lw, Q_LORA, KV_LORA, QK_ROPE, 3 * d])
    pool, lx, lg, cq, ckv, kr, gt = (w[:, o[i]:o[i + 1]] for i in range(7))
    cols = [pool, lx, lg, gt, cq, ckv, kr, kr[:, _swap_perm()]]
    n = sum(c.shape[1] for c in cols)
    cols.append(jnp.zeros((d, -n % INPROJ_TN), w.dtype))
    return jnp.concatenate(cols, axis=1).astype(BF)


def _prep_w_uq(w):
    qk = QK_NOPE + QK_ROPE
    w = w.reshape(w.shape[0], HEADS, qk)
    rope = w[:, :, QK_NOPE:]
    return jnp.concatenate([w, rope[:, :, _swap_perm()]], axis=2).reshape(w.shape[0], HEADS * HEAD_W).T.astype(BF)


def kernel(x, c, ctx, c_ctx, mod_w, mod_b, pre_mix_g, post_mix_g, pre_ffn_g, post_ffn_g, w_in, pool_w, pool_scale,
           pool_proj, conv_w, conv_b, gate_a_w, gate_a_b, gate_x_w, gate_x_b, lru_lambda, lru_proj, q_norm_g, w_uq,
           kv_norm_g, w_ukv, mla_proj, w_out, ffn_w1, ffn_w3, ffn_w2, router_w, moe_w1, moe_w3, moe_w2):
    bsz, seq, d = x.shape
    ctx_len = ctx.shape[1]
    depth = mod_w.shape[0]
    assert ctx_len == SEQ_TILE and seq % SEQ_TILE == 0 and seq % GRID_W == 0
    n_lat = bsz * seq
    n_ctx = bsz * ctx_len
    n_all = n_lat + n_ctx
    tm = min(1024, seq, n_ctx)
    assert seq % tm == 0 and n_ctx % tm == 0
    tm_merge = tm // 2

    def mod_idx_for(rows):
        return lambda i: jnp.where(i < n_lat // rows, i // (seq // rows), bsz)

    mod_idx = mod_idx_for(tm)

    cc = jnp.concatenate([c, c_ctx[None, :], jnp.zeros((8 - bsz - 1, d), F32)], axis=0)
    mods = _mod_call(cc, mod_w, mod_b)[:, :bsz + 1].reshape(depth, bsz + 1, 6, d)
    cos_t, sin_t = _rope_tables(seq, ctx_len)
    x_all = jnp.concatenate([x.reshape(n_lat, d), ctx.reshape(n_ctx, d)], axis=0)
    row1 = lambda v: v.reshape(1, -1)

    for l in range(depth):
        last = l == depth - 1
        n_out = n_lat if last else n_all
        mod = mods[l]
        z = _inproj_call(x_all, mod, row1(pre_mix_g[l]), _prep_w_in(w_in[l]), tm, mod_idx)
        wkv = w_ukv[l].reshape(KV_LORA, HEADS, QK_NOPE + V_DIM)
        wk = wkv[:, :, :QK_NOPE].reshape(KV_LORA, HEADS * QK_NOPE).astype(BF)
        wvt = wkv[:, :, QK_NOPE:].reshape(KV_LORA, HEADS * V_DIM).T.astype(BF)
        q, k, v = _qkv_call(z, cos_t, sin_t, row1(q_norm_g[l]), row1(kv_norm_g[l]), _prep_w_uq(w_uq[l]),
                            wk, wvt, bsz, seq, ctx_len, 6 * d)
        tq = min(512, seq)
        tk = 768 if (seq + ctx_len) % 768 == 0 else SEQ_TILE
        o = _attn_call(q, k, v, bsz=bsz, row_blk0=0, nq=seq // tq, tq=tq, kblk=0, klen=seq + ctx_len, tk=tk, kp=tk)
        if not last:
            o_ctx = _attn_call(q, k, v, bsz=bsz, row_blk0=n_lat // ctx_len, nq=1, tq=ctx_len,
                               kblk=seq // ctx_len, klen=ctx_len, tk=ctx_len, kp=ctx_len)
            o = jnp.concatenate([o, o_ctx], axis=0)
        mp = _pool_call(z, pool_w[l].astype(BF), row1(pool_scale[l]), bsz, seq, ctx_len, n_out)
        hs = []
        for dr in range(2):
            hs.append(_lru_call(z, conv_w[l], row1(conv_b[l]), gate_a_w[l, dr].astype(BF), row1(gate_a_b[l, dr]),
                                gate_x_w[l, dr].astype(BF), row1(gate_x_b[l, dr]), row1(lru_lambda[l, dr]),
                                bsz, seq, dr == 1))
        moe_layer = l % 2 == 1
        rw = None
        if moe_layer:
            rw = router_w[l // 2].T
        outs = _merge_call(x_all, mod, mp, hs[0], hs[1], z, o, pool_proj[l].astype(BF), lru_proj[l].astype(BF),
                           mla_proj[l].astype(BF), w_out[l].astype(BF), row1(post_mix_g[l]), row1(pre_ffn_g[l]),
                           rw, tm_merge, n_out, mod_idx_for(tm_merge))
        if not moe_layer:
            x1, h2 = outs
            x_all = _ffn_call(h2, x1, mod, row1(post_ffn_g[l]), ffn_w1[l // 2], ffn_w3[l // 2], ffn_w2[l // 2],
                              tm, mod_idx)
        else:
            x1, h2, rt = outs
            tme = min(1024, n_out * TOP_K // N_EXPERTS)
            dest, src_tok, blk_e, nused = _route_plan(rt, tme)
            xs = _sc_gather_rows(h2, src_tok)
            y = _moe_call(blk_e, nused, xs, moe_w1[l // 2], moe_w3[l // 2], moe_w2[l // 2], tme)
            dest2 = dest.reshape(n_out, TOP_K)
            ya = _sc_gather_rows(y, dest2[:, 0])
            yb = _sc_gather_rows(y, dest2[:, 1])
            x_all = _combine_call(ya, yb, rt, x1, mod, row1(post_ffn_g[l]), tm, mod_idx)
    return x_all[:n_lat].reshape(bsz, seq, d)
```

```python
import functools
import math

import numpy as np
import jax
import jax.numpy as jnp
from jax import lax
from jax.experimental import pallas as pl
from jax.experimental.pallas import tpu as pltpu
from jax.experimental.pallas import tpu_sc as plsc

BF = jnp.bfloat16
F32 = jnp.float32

RMS_EPS = 1e-6
GRID_W = 64
POOL_WINDOWS = (2, 4, 8, 16)
LRU_BLOCKS = 8
CONV_W = 4
LRU_C = 8.0
HEADS = 8
Q_LORA = 384
KV_LORA = 256
QK_NOPE = 128
QK_ROPE = 64
V_DIM = 128
MLA_SCALE = (QK_NOPE + QK_ROPE) ** -0.5
Q_SCALE = MLA_SCALE * math.log2(math.e)
ROPE_FREQS = QK_ROPE // 4
ROPE_THETA = 10000.0
N_EXPERTS = 8
TOP_K = 2

LANE = 128
HALO = 16
SEQ_TILE = 256
HEAD_W = 256
VT_ROWS = V_DIM + HALO
INPROJ_TN = 1024
FFN_TF = 768
VMEM_LIMIT = 48 * 1024 * 1024


def _cparams(*sem, flags=None):
    return pltpu.CompilerParams(dimension_semantics=sem, vmem_limit_bytes=VMEM_LIMIT, flags=flags)


def _rms(x, g):
    ms = jnp.mean(x * x, axis=-1, keepdims=True)
    return x * lax.rsqrt(ms + RMS_EPS) * g


def _sigmoid(x):
    return 0.5 * jnp.tanh(0.5 * x) + 0.5


def _silu(x):
    return x * _sigmoid(x)


def _gelu_tanh(x):
    return 0.5 * x * (1.0 + jnp.tanh(math.sqrt(2.0 / math.pi) * (x + 0.044715 * (x * x * x))))


def _dot(a, b):
    return jnp.dot(a, b, preferred_element_type=F32)


def _mod_kernel(c_ref, w_ref, b_ref, o_ref):
    s = _silu(c_ref[...])
    o_ref[...] = _dot(s.astype(BF), w_ref[...].astype(BF)) + b_ref[...]


def _mod_call(cc, mod_w, mod_b):
    depth, d, n6 = mod_w.shape
    tn = 1536
    return pl.pallas_call(
        _mod_kernel,
        grid=(depth, n6 // tn),
        in_specs=[pl.BlockSpec((8, d), lambda l, j: (0, 0)),
                  pl.BlockSpec((None, d, tn), lambda l, j: (l, 0, j)),
                  pl.BlockSpec((None, 1, tn), lambda l, j: (l, 0, j))],
        out_specs=pl.BlockSpec((None, 8, tn), lambda l, j: (l, 0, j)),
        out_shape=jax.ShapeDtypeStruct((depth, 8, n6), F32),
        compiler_params=_cparams("parallel", "arbitrary"),
        name="mod",
    )(cc, mod_w, mod_b.reshape(depth, 1, n6))


def _inproj_kernel(x_ref, mod_ref, g_ref, w_ref, z_ref, h_scr):
    @pl.when(pl.program_id(1) == 0)
    def _():
        h = _rms(x_ref[...], g_ref[...])
        h = h * (1.0 + mod_ref[1:2, :]) + mod_ref[0:1, :]
        h_scr[...] = h.astype(BF)

    z_ref[...] = _dot(h_scr[...], w_ref[...]).astype(BF)


def _inproj_call(x, mod, g, w, tm, mod_idx):
    m, d = x.shape
    n = w.shape[1]
    tn = INPROJ_TN
    return pl.pallas_call(
        _inproj_kernel,
        grid=(m // tm, n // tn),
        in_specs=[pl.BlockSpec((tm, d), lambda i, j: (i, 0)),
                  pl.BlockSpec((None, 6, d), lambda i, j: (mod_idx(i), 0, 0)),
                  pl.BlockSpec((1, d), lambda i, j: (0, 0)),
                  pl.BlockSpec((d, tn), lambda i, j: (0, j))],
        out_specs=pl.BlockSpec((tm, tn), lambda i, j: (i, j)),
        out_shape=jax.ShapeDtypeStruct((m, n), BF),
        scratch_shapes=[pltpu.VMEM((tm, d), BF)],
        compiler_params=_cparams("parallel", "arbitrary"),
        name="inproj",
    )(x, mod, g, w)


def _rope(x, cos, sin):
    return x * cos + pltpu.roll(x, LANE // 2, 1) * sin


_NT = (((1,), (1,)), ((), ()))


def _qkv_kernel(z_ref, cos_ref, sin_ref, cost_ref, sint_ref, qg_ref, kvg_ref, wuqt_ref, wk_ref, wvt_ref,
                qt_ref, k_ref, vt_ref):
    z = z_ref[...]
    cq = z[:, :Q_LORA].astype(F32)
    ckv = z[:, Q_LORA:Q_LORA + KV_LORA].astype(F32)
    kr = z[:, Q_LORA + KV_LORA:].astype(F32)
    cqn = _rms(cq, qg_ref[...]).astype(BF)
    ckvn = _rms(ckv, kvg_ref[...]).astype(BF)
    qt = lax.dot_general(wuqt_ref[...], cqn, _NT, preferred_element_type=F32)
    kn = _dot(ckvn, wk_ref[...])
    vt = lax.dot_general(wvt_ref[...], ckvn, _NT, preferred_element_type=F32)
    krot = _rope(kr, cos_ref[...], sin_ref[...]).astype(BF)
    cost = cost_ref[...]
    sint = sint_ref[...]
    for h in range(HEADS):
        c0 = h * HEAD_W
        c1 = c0 + QK_NOPE
        c2 = c1 + QK_ROPE
        qt_ref[c0:c1, :] = (qt[c0:c1, :] * Q_SCALE).astype(BF)
        qt_ref[c1:c2, :] = ((qt[c1:c2, :] * cost + qt[c2:c0 + HEAD_W, :] * sint) * Q_SCALE).astype(BF)
        qt_ref[c2:c0 + HEAD_W, :] = jnp.zeros((QK_ROPE, qt.shape[1]), BF)
        k_ref[h, :, 0:QK_NOPE] = kn[:, h * QK_NOPE:(h + 1) * QK_NOPE].astype(BF)
        k_ref[h, :, QK_NOPE:HEAD_W] = krot
        vt_ref[h, 0:V_DIM, :] = vt[h * V_DIM:(h + 1) * V_DIM, :].astype(BF)
        vt_ref[h, V_DIM:VT_ROWS, :] = jnp.ones((VT_ROWS - V_DIM, vt.shape[1]), BF)


def _qkv_call(z, cos_t, sin_t, qg, kvg, wuqt, wk, wvt, bsz, seq, ctx_len, z_off):
    m = z.shape[0]
    cos_tt = cos_t[:, :QK_ROPE].T
    sin_tt = sin_t[:, :QK_ROPE].T
    ts = SEQ_TILE
    nt = seq // ts
    nlat = bsz * nt
    lk = seq + ctx_len
    zw = Q_LORA + KV_LORA + LANE
    assert z_off % zw == 0
    zcol = z_off // zw

    def tab_idx(i):
        return (jnp.where(i < nlat, i % nt, nt), 0)

    def kv_idx(i):
        return (jnp.where(i < nlat, i // nt, i - nlat), 0, jnp.where(i < nlat, i % nt, nt), 0)

    def vt_idx(i):
        return (jnp.where(i < nlat, i // nt, i - nlat), 0, 0, jnp.where(i < nlat, i % nt, nt))

    return pl.pallas_call(
        _qkv_kernel,
        grid=(m // ts,),
        in_specs=[pl.BlockSpec((ts, zw), lambda i: (i, zcol)),
                  pl.BlockSpec((ts, LANE), tab_idx),
                  pl.BlockSpec((ts, LANE), tab_idx),
                  pl.BlockSpec((QK_ROPE, ts), lambda i: tab_idx(i)[::-1]),
                  pl.BlockSpec((QK_ROPE, ts), lambda i: tab_idx(i)[::-1]),
                  pl.BlockSpec((1, Q_LORA), lambda i: (0, 0)),
                  pl.BlockSpec((1, KV_LORA), lambda i: (0, 0)),
                  pl.BlockSpec(wuqt.shape, lambda i: (0, 0)),
                  pl.BlockSpec(wk.shape, lambda i: (0, 0)),
                  pl.BlockSpec(wvt.shape, lambda i: (0, 0))],
        out_specs=[pl.BlockSpec((HEADS * HEAD_W, ts), lambda i: (0, i)),
                   pl.BlockSpec((None, HEADS, ts, HEAD_W), kv_idx),
                   pl.BlockSpec((None, HEADS, VT_ROWS, ts), vt_idx)],
        out_shape=[jax.ShapeDtypeStruct((HEADS * HEAD_W, m), BF),
                   jax.ShapeDtypeStruct((bsz, HEADS, lk, HEAD_W), BF),
                   jax.ShapeDtypeStruct((bsz, HEADS, VT_ROWS, lk), BF)],
        compiler_params=_cparams("parallel"),
        name="qkv_up",
    )(z, cos_t, sin_t, cos_tt, sin_tt, qg, kvg, wuqt, wk, wvt)


def _col_reduce(x, pair, red):
    n = x.shape[0] // 4
    a = pair(pair(x[0:n], x[n:2 * n]), pair(x[2 * n:3 * n], x[3 * n:4 * n]))
    return red(a, axis=0, keepdims=True)


ATTN_LAG_LIMIT = 20.0


def _attn_kernel(qt_ref, k_ref, vt_ref, o_ref, p_scr, *, chunks, kp):
    qt = qt_ref[...]
    tq = qt.shape[1]

    nk = len(chunks)

    def scores(c):
        k0, kn = chunks[c]
        return _dot(k_ref[k0:k0 + kn, :], qt)

    def values(c):
        k0, kn = chunks[c]
        return _dot(vt_ref[:, k0:k0 + kn], p_scr[c % 2, 0:kn, :])

    def finish(a):
        o_ref[...] = (a[0:V_DIM] / a[V_DIM:V_DIM + 1]).T.astype(o_ref.dtype)

    s = scores(0)
    ref = _col_reduce(s, jnp.maximum, jnp.max)
    p_scr[0, 0:chunks[0][1], :] = jnp.exp2((s - ref).astype(BF))
    acc = alpha = None
    lag = jnp.zeros((1, tq), F32)
    for c in range(1, nk):
        s = scores(c)
        p_scr[c % 2, 0:chunks[c][1], :] = jnp.exp2((s - ref).astype(BF))
        mc = _col_reduce(s, jnp.maximum, jnp.max)
        pv = values(c - 1)
        acc = pv if acc is None else acc + pv
        if alpha is not None:
            acc = acc * alpha
        lag = jnp.maximum(lag, mc - ref)
        new_ref = jnp.maximum(ref, mc)
        alpha = jnp.exp2(ref - new_ref)
        ref = new_ref
    pv = values(nk - 1)
    acc = pv if acc is None else acc + pv
    over = jnp.max(lag) > ATTN_LAG_LIMIT

    @pl.when(jnp.logical_not(over))
    def _():
        finish(acc)

    @pl.when(over)
    def _():
        def body(j, carry):
            m, a = carry
            r0 = pl.multiple_of(j * kp, kp)
            sj = _dot(k_ref[pl.ds(r0, kp), :], qt)
            m_new = jnp.maximum(m, jnp.max(sj, axis=0, keepdims=True))
            pj = jnp.exp2((sj - m_new).astype(BF))
            a = jnp.exp2(m - m_new) * a + _dot(vt_ref[:, pl.ds(r0, kp)], pj)
            return m_new, a

        init = (jnp.full((1, tq), -1e30, F32), jnp.zeros((VT_ROWS, tq), F32))
        finish(lax.fori_loop(0, k_ref.shape[0] // kp, body, init)[1])


ATTN_FIRST = 256
ATTN_CHUNK = 512


def _attn_call(q, k, v, *, bsz, row_blk0, nq, tq, kblk, klen):
    rest = klen - ATTN_FIRST
    step = ATTN_CHUNK if rest % ATTN_CHUNK == 0 else ATTN_FIRST
    assert rest % step == 0
    chunks = ((0, ATTN_FIRST),) + tuple((ATTN_FIRST + i * step, step) for i in range(rest // step))
    tk = max(n for _, n in chunks)
    kern = functools.partial(_attn_kernel, chunks=chunks, kp=ATTN_FIRST)
    return pl.pallas_call(
        kern,
        grid=(bsz, HEADS, nq),
        in_specs=[pl.BlockSpec((HEAD_W, tq), lambda b, h, i: (h, row_blk0 + b * nq + i)),
                  pl.BlockSpec((None, None, klen, HEAD_W), lambda b, h, i: (b, h, kblk, 0)),
                  pl.BlockSpec((None, None, VT_ROWS, klen), lambda b, h, i: (b, h, 0, kblk))],
        out_specs=pl.BlockSpec((tq, V_DIM), lambda b, h, i: (b * nq + i, h)),
        out_shape=jax.ShapeDtypeStruct((bsz * nq * tq, HEADS * V_DIM), BF),
        scratch_shapes=[pltpu.VMEM((2, tk, tq), BF)],
        compiler_params=_cparams("parallel", "parallel", "arbitrary"),
        name="attn",
    )(q, k, v)


def _seq_flags(i, nlat, nt):
    is_ctx = i >= nlat
    t = jnp.where(is_ctx, 0, i % nt)
    first = jnp.logical_or(is_ctx, t == 0)
    last = jnp.logical_or(is_ctx, t == nt - 1)
    return is_ctx, t, first, last


def _pool_kernel(x_ref, xp_ref, xn_ref, pw_ref, ps_ref, o_ref, *, nlat, nt, seq, ctx_len):
    ts = x_ref.shape[0]
    is_ctx, t, first, last = _seq_flags(pl.program_id(0), nlat, nt)
    seq_len = jnp.where(is_ctx, ctx_len, seq)
    x = x_ref[...]
    xp = jnp.where(first, jnp.zeros_like(xp_ref[...]), xp_ref[...])
    xn = jnp.where(last, jnp.zeros_like(xn_ref[...]), xn_ref[...])
    xe = jnp.concatenate([xp, x, xn], axis=0)
    tpos = t * ts + lax.broadcasted_iota(jnp.int32, (ts, 1), 0)
    rel = (lax.broadcasted_iota(jnp.int32, (ts, ts + 2 * HALO), 1) - HALO
           - lax.broadcasted_iota(jnp.int32, (ts, ts + 2 * HALO), 0))
    gw = x.shape[1] // len(POOL_WINDOWS)
    for g, w in enumerate(POOL_WINDOWS):
        cs = slice(g * gw, (g + 1) * gw)
        band = jnp.where(rel >= -(w // 2), jnp.where(rel < w - w // 2, 1.0, 0.0), 0.0).astype(BF)
        s = _dot(band, xe[:, cs])
        cnt = (jnp.minimum(tpos + (w - w // 2), seq_len) - jnp.maximum(tpos - w // 2, 0)).astype(F32)
        mean_minus = s / cnt - x[:, cs].astype(F32)
        o_ref[:, cs] = (_dot(mean_minus.astype(BF), pw_ref[g]) * ps_ref[:, cs]).astype(BF)


def _halo_specs(ts, width, col, row_blk, m):
    r = ts // HALO
    nh = m // HALO
    prev = pl.BlockSpec((HALO, width), lambda *a: (jnp.maximum(row_blk(*a) * r - 1, 0), col))
    nxt = pl.BlockSpec((HALO, width), lambda *a: (jnp.minimum((row_blk(*a) + 1) * r, nh - 1), col))
    return prev, nxt


def _pool_call(z, pool_w, pool_scale, bsz, seq, ctx_len, n_rows):
    m = z.shape[0]
    ts = SEQ_TILE
    nt = seq // ts
    nlat = bsz * nt
    width = pool_scale.shape[1]
    prev, nxt = _halo_specs(ts, width, 0, lambda i: i, m)
    kern = functools.partial(_pool_kernel, nlat=nlat, nt=nt, seq=seq, ctx_len=ctx_len)
    return pl.pallas_call(
        kern,
        grid=(n_rows // ts,),
        in_specs=[pl.BlockSpec((ts, width), lambda i: (i, 0)), prev, nxt,
                  pl.BlockSpec(pool_w.shape, lambda i: (0, 0, 0)),
                  pl.BlockSpec((1, width), lambda i: (0, 0))],
        out_specs=pl.BlockSpec((ts, width), lambda i: (i, 0)),
        out_shape=jax.ShapeDtypeStruct((n_rows, width), BF),
        compiler_params=_cparams("parallel"),
        name="pool",
    )(z, z, z, pool_w, pool_scale)


def _lru_kernel(x_ref, xp_ref, xn_ref, cw_ref, cb_ref, wa_ref, ba_ref, wx_ref, bx_ref, lam_ref, o_ref,
                xf_scr, hl_scr, ca_scr, ga_scr, gb_scr, hp_scr, h_scr, *, reverse, nt):
    ts, width = x_ref.shape
    ng = ts // 8
    bw = width // LRU_BLOCKS
    s = pl.program_id(1)
    t = (nt - s) if reverse else (s - 1)
    first = jnp.logical_or(s == 0, t == 0)
    last = jnp.logical_or(s == 0, t == nt - 1)

    @pl.when(s == 0)
    def _():
        h_scr[...] = jnp.zeros_like(h_scr)

    left = CONV_W // 2
    xp = xp_ref[...].astype(F32)[HALO - 8:HALO]
    xn = xn_ref[...].astype(F32)[0:8]
    xp = jnp.where(first, jnp.zeros_like(xp), xp)
    xn = jnp.where(last, jnp.zeros_like(xn), xn)
    lam = lam_ref[...]
    neg = -lam
    softplus = jnp.maximum(neg, 0.0) + jnp.log1p(jnp.exp(-jnp.abs(neg)))
    coef = (-LRU_C * math.log2(math.e)) * softplus
    row = lax.broadcasted_iota(jnp.int32, (ng, bw), 0)
    slab = lambda n, j: (n, pl.ds(j, ng, stride=8), slice(None))
    order = range(7, -1, -1) if reverse else range(8)

    for n in range(LRU_BLOCKS):
        cs = slice(n * bw, (n + 1) * bw)
        xf_scr[n] = x_ref[:, cs].astype(F32)
        xs = {j: xf_scr[slab(n, j)] for j in range(8)}
        for j in range(-left, 0):
            xs[j] = jnp.where(row == 0, xp[8 + j:9 + j, cs], pltpu.roll(xs[8 + j], 1, 0))
        for j in range(8, 8 + CONV_W - 1 - left):
            xs[j] = jnp.where(row == ng - 1, xn[j - 8:j - 7, cs], pltpu.roll(xs[j - 8], ng - 1, 0))
        us = []
        for j in range(8):
            u = cb_ref[:, cs] + cw_ref[0:1, cs] * xs[j - left]
            for k in range(1, CONV_W):
                u = u + cw_ref[k:k + 1, cs] * xs[j - left + k]
            us.append(u)
        un = jnp.concatenate(us, axis=0)
        ub = un.astype(BF)
        r = _sigmoid(_dot(ub, wa_ref[n]) + ba_ref[:, cs])
        gi = _sigmoid(_dot(ub, wx_ref[n]) + bx_ref[:, cs])
        a = jnp.exp2(r * coef[:, cs])
        om = 1.0 - a * a
        b = (om * lax.rsqrt(jnp.maximum(om, 1e-30))) * (gi * un)
        hl = ca = None
        for j in order:
            aj = a[j * ng:(j + 1) * ng]
            bj = b[j * ng:(j + 1) * ng]
            hl, ca = (bj, aj) if hl is None else (aj * hl + bj, aj * ca)
            hl_scr[n, j * ng:(j + 1) * ng, :] = hl
            ca_scr[n, j * ng:(j + 1) * ng, :] = ca
        ga_scr[pl.ds(n, ng, stride=LRU_BLOCKS), :] = ca
        gb_scr[pl.ds(n, ng, stride=LRU_BLOCKS), :] = hl

    h = h_scr[...]
    for g in (range(ng - 1, -1, -1) if reverse else range(ng)):
        gs = slice(g * LRU_BLOCKS, (g + 1) * LRU_BLOCKS)
        hp_scr[gs, :] = h
        h = ga_scr[gs, :] * h + gb_scr[gs, :]
    h_scr[...] = h

    for n in range(LRU_BLOCKS):
        hp = hp_scr[pl.ds(n, ng, stride=LRU_BLOCKS), :]
        for j in range(8):
            xf_scr[slab(n, j)] = hl_scr[n, j * ng:(j + 1) * ng, :] + ca_scr[n, j * ng:(j + 1) * ng, :] * hp
        o_ref[:, n * bw:(n + 1) * bw] = xf_scr[n].astype(o_ref.dtype)


def _lru_call(z, conv_w, conv_b, wa, ba, wx, bx, lam, bsz, seq, reverse):
    m = z.shape[0]
    ts = SEQ_TILE
    nt = seq // ts
    nlat = bsz * nt
    width = conv_b.shape[1]

    def row_blk(b, s):
        t = (nt - s) if reverse else (s - 1)
        return jnp.where(s == 0, nlat + b, b * nt + t)

    prev, nxt = _halo_specs(ts, width, 1, row_blk, m)
    vec = lambda shape: pl.BlockSpec(shape, lambda b, s: (0,) * len(shape))
    kern = functools.partial(_lru_kernel, reverse=reverse, nt=nt)
    return pl.pallas_call(
        kern,
        grid=(bsz, nt + 1),
        in_specs=[pl.BlockSpec((ts, width), lambda b, s: (row_blk(b, s), 1)), prev, nxt,
                  vec(conv_w.shape), vec(conv_b.shape), vec(wa.shape), vec(ba.shape),
                  vec(wx.shape), vec(bx.shape), vec(lam.shape)],
        out_specs=pl.BlockSpec((ts, width), lambda b, s: (row_blk(b, s), 0)),
        out_shape=jax.ShapeDtypeStruct((m, width), BF),
        scratch_shapes=[pltpu.VMEM((LRU_BLOCKS, ts, width // LRU_BLOCKS), F32)] * 3
        + [pltpu.VMEM((ts // 8 * LRU_BLOCKS, width // LRU_BLOCKS), F32)] * 3
        + [pltpu.VMEM((LRU_BLOCKS, width // LRU_BLOCKS), F32)],
        compiler_params=_cparams("parallel", "arbitrary"),
        name="lru_bwd" if reverse else "lru_fwd",
    )(z, z, z, conv_w, conv_b, wa, ba, wx, bx, lam)


def _merge_kernel(x_ref, mod_ref, mp_ref, hf_ref, hb_ref, lg_ref, o_ref, gt_ref, wp_ref, wl_ref, wm_ref, wo_ref,
                  g1_ref, g2_ref, *rest, route):
    if route:
        rw_ref, x1_ref, h2_ref, rt_ref = rest
    else:
        x1_ref, h2_ref = rest
    tm, d = x_ref.shape
    halves = [slice(0, tm // 2), slice(tm // 2, tm)]

    def branches(rs):
        y_pool = _dot(mp_ref[rs, :], wp_ref[...])
        lru_in = (hf_ref[rs, :].astype(F32) + hb_ref[rs, :].astype(F32)) * _gelu_tanh(lg_ref[rs, :].astype(F32))
        y_lru = _dot(lru_in.astype(BF), wl_ref[...])
        y_mla = _dot(o_ref[rs, :], wm_ref[...])
        return y_pool, y_lru, y_mla

    def mixed(rs, ys):
        mix = (_sigmoid(gt_ref[rs, 0:d].astype(F32)) * ys[0]
               + _sigmoid(gt_ref[rs, d:2 * d].astype(F32)) * ys[1]
               + _sigmoid(gt_ref[rs, 2 * d:3 * d].astype(F32)) * ys[2])
        return _dot(mix.astype(BF), wo_ref[...])

    ys = [branches(rs) for rs in halves]
    outs = [mixed(rs, y) for rs, y in zip(halves, ys)]
    for rs, y in zip(halves, outs):
        x1 = x_ref[rs, :] + mod_ref[2:3, :] * _rms(y, g1_ref[...])
        x1_ref[rs, :] = x1
        h2 = _rms(x1, g2_ref[...]) * (1.0 + mod_ref[4:5, :]) + mod_ref[3:4, :]
        h2_ref[rs, :] = h2.astype(h2_ref.dtype)
        if route:
            _route_rows(h2, rw_ref, rt_ref, rs)


def _route_rows(h2, rw_ref, rt_ref, rs):
    logit = [jnp.sum(h2 * rw_ref[e:e + 1, :], axis=1, keepdims=True) for e in range(N_EXPERTS)]
    v1, i1 = logit[0], jnp.zeros_like(logit[0])
    for e in range(1, N_EXPERTS):
        upd = logit[e] > v1
        v1 = jnp.where(upd, logit[e], v1)
        i1 = jnp.where(upd, float(e), i1)
    v2, i2 = jnp.full_like(v1, -jnp.inf), jnp.zeros_like(v1)
    for e in range(N_EXPERTS):
        cand = jnp.where(i1 == float(e), -jnp.inf, logit[e])
        upd = cand > v2
        v2 = jnp.where(upd, cand, v2)
        i2 = jnp.where(upd, float(e), i2)
    ex = jnp.exp(v2 - v1)
    gate1 = 1.0 / (1.0 + ex)
    gate2 = ex / (1.0 + ex)
    col = lax.broadcasted_iota(jnp.int32, (h2.shape[0], rt_ref.shape[1]), 1)
    rt_ref[rs, :] = jnp.where(col == 0, i1, jnp.where(col == 1, i2,
                              jnp.where(col == 2, gate1, jnp.where(col == 3, gate2, 0.0))))


def _merge_call(x, mod, mp, hf, hb, z, o, wp, wl, wm, wo, g1, g2, rw, tm, n_rows, mod_idx):
    d = x.shape[1]
    route = rw is not None
    row = lambda c: pl.BlockSpec((tm, d), lambda i: (i, c))
    full = lambda a: pl.BlockSpec(a.shape, lambda i: (0,) * a.ndim)
    in_specs = [row(0), pl.BlockSpec((None, 6, d), lambda i: (mod_idx(i), 0, 0)),
                row(0), row(0), row(0), row(2), row(0), pl.BlockSpec((tm, 3 * d), lambda i: (i, 1)),
                full(wp), full(wl), full(wm), full(wo), full(g1), full(g2)]
    args = [x, mod, mp, hf, hb, z, o, z, wp, wl, wm, wo, g1, g2]
    out_specs = [row(0), row(0)]
    out_shape = [jax.ShapeDtypeStruct((n_rows, d), F32), jax.ShapeDtypeStruct((n_rows, d), F32 if route else BF)]
    if route:
        in_specs.append(full(rw))
        args.append(rw)
        out_specs.append(pl.BlockSpec((tm, LANE), lambda i: (i, 0)))
        out_shape.append(jax.ShapeDtypeStruct((n_rows, LANE), F32))
    return pl.pallas_call(
        functools.partial(_merge_kernel, route=route),
        grid=(n_rows // tm,),
        in_specs=in_specs, out_specs=out_specs, out_shape=out_shape,
        compiler_params=_cparams("parallel"),
        name="merge",
    )(*args)


def _swiglu_step(x, w1, w3, w2):
    a = _dot(x, w1)
    b = _dot(x, w3)
    return _dot((_silu(a) * b).astype(BF), w2)


def _ffn_kernel(h_ref, x_ref, mod_ref, g_ref, w13_ref, w2_ref, o_ref):
    f = pl.program_id(1)
    tf = w2_ref.shape[0]

    @pl.when(f == 0)
    def _():
        o_ref[...] = jnp.zeros_like(o_ref)

    ab = _dot(h_ref[...], w13_ref[...])
    o_ref[...] += _dot((_silu(ab[:, :tf]) * ab[:, tf:]).astype(BF), w2_ref[...])

    @pl.when(f == pl.num_programs(1) - 1)
    def _():
        o_ref[...] = x_ref[...] + mod_ref[5:6, :] * _rms(o_ref[...], g_ref[...])


def _ffn_call(h2, x1, mod, g, w1, w3, w2, tm, mod_idx):
    m, d = x1.shape
    tf = FFN_TF
    pad = -w1.shape[1] % tf
    w1, w3 = (jnp.pad(w, ((0, 0), (0, pad))) for w in (w1, w3))
    ff = w1.shape[1]
    w13 = jnp.stack([w1.reshape(d, ff // tf, tf), w3.reshape(d, ff // tf, tf)], axis=2).reshape(d, 2 * ff).astype(BF)
    w2 = jnp.pad(w2, ((0, pad), (0, 0))).astype(BF)
    return pl.pallas_call(
        _ffn_kernel,
        grid=(m // tm, ff // tf),
        in_specs=[pl.BlockSpec((tm, d), lambda i, f: (i, 0)),
                  pl.BlockSpec((tm, d), lambda i, f: (i, 0)),
                  pl.BlockSpec((None, 6, d), lambda i, f: (mod_idx(i), 0, 0)),
                  pl.BlockSpec((1, d), lambda i, f: (0, 0)),
                  pl.BlockSpec((d, 2 * tf), lambda i, f: (0, f)),
                  pl.BlockSpec((tf, d), lambda i, f: (f, 0))],
        out_specs=pl.BlockSpec((tm, d), lambda i, f: (i, 0)),
        out_shape=jax.ShapeDtypeStruct((m, d), F32),
        compiler_params=_cparams("parallel", "arbitrary"),
        name="ffn",
    )(h2, x1, mod, g, w13, w2)


def _moe_kernel(blk_e_ref, nused_ref, x_ref, w1_ref, w3_ref, w2_ref, o_ref, xb):
    i = pl.program_id(0)
    f = pl.program_id(1)
    used = i < nused_ref[0]

    @pl.when(f == 0)
    def _():
        o_ref[...] = jnp.zeros_like(o_ref)
        xb[...] = x_ref[...].astype(BF)

    @pl.when(used)
    def _():
        o_ref[...] += _swiglu_step(xb[...], w1_ref[...].astype(BF), w3_ref[...].astype(BF), w2_ref[...].astype(BF))


def _moe_call(blk_e, nused, xs, w1, w3, w2, tme):
    n_rows, d = xs.shape
    ff = w1.shape[2]
    tf = 512
    return pl.pallas_call(
        _moe_kernel,
        grid_spec=pltpu.PrefetchScalarGridSpec(
            num_scalar_prefetch=2,
            grid=(n_rows // tme, ff // tf),
            in_specs=[pl.BlockSpec((tme, d), lambda i, f, be, nu: (i, 0)),
                      pl.BlockSpec((None, d, tf), lambda i, f, be, nu: (be[i], 0, f)),
                      pl.BlockSpec((None, d, tf), lambda i, f, be, nu: (be[i], 0, f)),
                      pl.BlockSpec((None, tf, d), lambda i, f, be, nu: (be[i], f, 0))],
            out_specs=pl.BlockSpec((tme, d), lambda i, f, be, nu: (i, 0)),
            scratch_shapes=[pltpu.VMEM((tme, d), BF)]),
        out_shape=jax.ShapeDtypeStruct((n_rows, d), F32),
        compiler_params=_cparams("arbitrary", "arbitrary"),
        name="moe",
    )(blk_e, nused, xs, w1, w3, w2)


SC_CORES = 2
SC_SUBCORES = 16
SC_GATHER_ROWS = 64


def _sc_gather_rows(table, idx):
    n_idx = idx.shape[0]
    d = table.shape[1]
    workers = SC_CORES * SC_SUBCORES
    per_w = n_idx // workers
    rows = min(SC_GATHER_ROWS, per_w)
    assert n_idx % (8 * workers) == 0 and per_w % rows == 0
    mesh = plsc.VectorSubcoreMesh(core_axis_name="c", subcore_axis_name="s",
                                  num_cores=SC_CORES, num_subcores=SC_SUBCORES)

    def body(table_hbm, idx_hbm, out_hbm, idx_v, rows_v, sem):
        wid = lax.axis_index("s") * SC_CORES + lax.axis_index("c")
        base = wid * per_w

        @pl.loop(0, per_w // rows)
        def _(it):
            off = pl.multiple_of(base + it * rows, 8)
            pltpu.sync_copy(idx_hbm.at[pl.ds(off, rows)], idx_v)
            pltpu.async_copy(table_hbm.at[idx_v], rows_v, sem).wait()
            pltpu.sync_copy(rows_v, out_hbm.at[pl.ds(off, rows)])

    return pl.kernel(
        body,
        out_type=jax.ShapeDtypeStruct((n_idx, d), table.dtype),
        mesh=mesh,
        scratch_types=[pltpu.VMEM((rows,), jnp.int32), pltpu.VMEM((rows, d), table.dtype),
                       pltpu.SemaphoreType.DMA],
        name="sc_gather",
    )(table, idx)


def _combine_kernel(ya_ref, yb_ref, rt_ref, x_ref, mod_ref, g_ref, o_ref):
    g1 = rt_ref[:, 2:3]
    g2 = rt_ref[:, 3:4]
    f = g1 * ya_ref[...].astype(F32) + g2 * yb_ref[...].astype(F32)
    o_ref[...] = x_ref[...] + mod_ref[5:6, :] * _rms(f, g_ref[...])


def _combine_call(ya, yb, rt, x1, mod, g, tm, mod_idx):
    n, d = x1.shape
    return pl.pallas_call(
        _combine_kernel,
        grid=(n // tm,),
        in_specs=[pl.BlockSpec((tm, d), lambda i: (i, 0)),
                  pl.BlockSpec((tm, d), lambda i: (i, 0)),
                  pl.BlockSpec((tm, LANE), lambda i: (i, 0)),
                  pl.BlockSpec((tm, d), lambda i: (i, 0)),
                  pl.BlockSpec((None, 6, d), lambda i: (mod_idx(i), 0, 0)),
                  pl.BlockSpec((1, d), lambda i: (0, 0))],
        out_specs=pl.BlockSpec((tm, d), lambda i: (i, 0)),
        out_shape=jax.ShapeDtypeStruct((n, d), F32),
        compiler_params=_cparams("parallel"),
        name="combine",
    )(ya, yb, rt, x1, mod, g)


def _route_plan(rt, tme):
    n = rt.shape[0]
    e_flat = rt[:, :TOP_K].astype(jnp.int32).reshape(-1)
    onehot = (e_flat[:, None] == jnp.arange(N_EXPERTS, dtype=jnp.int32)[None, :]).astype(jnp.int32)
    csum = jnp.cumsum(onehot, axis=0)
    rank = jnp.sum(onehot * csum, axis=1) - 1
    counts = csum[-1]
    padded = (counts + tme - 1) // tme * tme
    pad_end = jnp.cumsum(padded)
    pad_start = pad_end - padded
    dest = pad_start[e_flat] + rank
    n_rows = n * TOP_K + N_EXPERTS * tme
    n_blk = n_rows // tme
    src_tok = (jnp.arange(n_rows, dtype=jnp.int32) % n).at[dest].set(jnp.arange(n * TOP_K, dtype=jnp.int32) // TOP_K)
    blk_start = jnp.arange(n_blk, dtype=jnp.int32) * tme
    blk_e = jnp.minimum(jnp.sum((pad_end[None, :] <= blk_start[:, None]).astype(jnp.int32), axis=1), N_EXPERTS - 1)
    nused = (pad_end[-1] // tme).astype(jnp.int32).reshape(1)
    return dest, src_tok, blk_e, nused


def _swap_perm():
    j = np.arange(QK_ROPE)
    axis, half, f = j // (2 * ROPE_FREQS), (j % (2 * ROPE_FREQS)) // ROPE_FREQS, j % ROPE_FREQS
    return axis * 2 * ROPE_FREQS + (1 - half) * ROPE_FREQS + f


def _rope_tables(seq, ctx_len):
    rows = seq // GRID_W
    row = jnp.repeat(jnp.arange(rows, dtype=F32), GRID_W)
    col = (jnp.arange(rows * GRID_W) % GRID_W).astype(F32)
    inv = ROPE_THETA ** (-jnp.arange(ROPE_FREQS, dtype=F32) / ROPE_FREQS)
    ang = jnp.stack([row[:, None] * inv, col[:, None] * inv], axis=1)
    cos, sin = jnp.cos(ang), jnp.sin(ang)
    cos64 = jnp.stack([cos, cos], axis=2).reshape(seq, QK_ROPE)
    sin64 = jnp.stack([-sin, sin], axis=2).reshape(seq, QK_ROPE)
    pad = jnp.zeros((seq, LANE - QK_ROPE), F32)
    cos_t = jnp.concatenate([cos64, pad], axis=1)
    sin_t = jnp.concatenate([sin64, pad], axis=1)
    ident = jnp.concatenate([jnp.ones((ctx_len, QK_ROPE), F32), jnp.zeros((ctx_len, LANE - QK_ROPE), F32)], axis=1)
    return jnp.concatenate([cos_t, ident], axis=0), jnp.concatenate([sin_t, jnp.zeros((ctx_len, LANE), F32)], axis=0)


def _prep_w_in(w):
    d = w.shape[0]
    pw = lw = d
    o = np.cumsum([0, pw, lw, lw, Q_LORA, KV_LORA, QK_ROPE, 3 * d])
    pool, lx, lg, cq, ckv, kr, gt = (w[:, o[i]:o[i + 1]] for i in range(7))
    cols = [pool, lx, lg, gt, cq, ckv, kr, kr[:, _swap_perm()]]
    n = sum(c.shape[1] for c in cols)
    cols.append(jnp.zeros((d, -n % INPROJ_TN), w.dtype))
    return jnp.concatenate(cols, axis=1).astype(BF)


def _prep_w_uq(w):
    qk = QK_NOPE + QK_ROPE
    w = w.reshape(w.shape[0], HEADS, qk)
    rope = w[:, :, QK_NOPE:]
    return jnp.concatenate([w, rope[:, :, _swap_perm()]], axis=2).reshape(w.shape[0], HEADS * HEAD_W).T.astype(BF)


def kernel(x, c, ctx, c_ctx, mod_w, mod_b, pre_mix_g, post_mix_g, pre_ffn_g, post_ffn_g, w_in, pool_w, pool_scale,
           pool_proj, conv_w, conv_b, gate_a_w, gate_a_b, gate_x_w, gate_x_b, lru_lambda, lru_proj, q_norm_g, w_uq,
           kv_norm_g, w_ukv, mla_proj, w_out, ffn_w1, ffn_w3, ffn_w2, router_w, moe_w1, moe_w3, moe_w2):
    bsz, seq, d = x.shape
    ctx_len = ctx.shape[1]
    depth = mod_w.shape[0]
    assert ctx_len == SEQ_TILE and seq % SEQ_TILE == 0 and seq % GRID_W == 0
    n_lat = bsz * seq
    n_ctx = bsz * ctx_len
    n_all = n_lat + n_ctx
    tm = min(1024, seq, n_ctx)
    assert seq % tm == 0 and n_ctx % tm == 0
    tm_merge = tm // 2

    def mod_idx_for(rows):
        return lambda i: jnp.where(i < n_lat // rows, i // (seq // rows), bsz)

    mod_idx = mod_idx_for(tm)

    cc = jnp.concatenate([c, c_ctx[None, :], jnp.zeros((8 - bsz - 1, d), F32)], axis=0)
    mods = _mod_call(cc, mod_w, mod_b)[:, :bsz + 1].reshape(depth, bsz + 1, 6, d)
    cos_t, sin_t = _rope_tables(seq, ctx_len)
    x_all = jnp.concatenate([x.reshape(n_lat, d), ctx.reshape(n_ctx, d)], axis=0)
    row1 = lambda v: v.reshape(1, -1)

    for l in range(depth):
        last = l == depth - 1
        n_out = n_lat if last else n_all
        mod = mods[l]
        z = _inproj_call(x_all, mod, row1(pre_mix_g[l]), _prep_w_in(w_in[l]), tm, mod_idx)
        wkv = w_ukv[l].reshape(KV_LORA, HEADS, QK_NOPE + V_DIM)
        wk = wkv[:, :, :QK_NOPE].reshape(KV_LORA, HEADS * QK_NOPE).astype(BF)
        wvt = wkv[:, :, QK_NOPE:].reshape(KV_LORA, HEADS * V_DIM).T.astype(BF)
        q, k, v = _qkv_call(z, cos_t, sin_t, row1(q_norm_g[l]), row1(kv_norm_g[l]), _prep_w_uq(w_uq[l]),
                            wk, wvt, bsz, seq, ctx_len, 6 * d)
        tq = min(512, seq)
        o = _attn_call(q, k, v, bsz=bsz, row_blk0=0, nq=seq // tq, tq=tq, kblk=0, klen=seq + ctx_len)
        if not last:
            o_ctx = _attn_call(q, k, v, bsz=bsz, row_blk0=n_lat // ctx_len, nq=1, tq=ctx_len,
                               kblk=seq // ctx_len, klen=ctx_len)
            o = jnp.concatenate([o, o_ctx], axis=0)
        mp = _pool_call(z, pool_w[l].astype(BF), row1(pool_scale[l]), bsz, seq, ctx_len, n_out)
        hs = []
        for dr in range(2):
            hs.append(_lru_call(z, conv_w[l], row1(conv_b[l]), gate_a_w[l, dr].astype(BF), row1(gate_a_b[l, dr]),
                                gate_x_w[l, dr].astype(BF), row1(gate_x_b[l, dr]), row1(lru_lambda[l, dr]),
                                bsz, seq, dr == 1))
        moe_layer = l % 2 == 1
        rw = None
        if moe_layer:
            rw = router_w[l // 2].T
        outs = _merge_call(x_all, mod, mp, hs[0], hs[1], z, o, pool_proj[l].astype(BF), lru_proj[l].astype(BF),
                           mla_proj[l].astype(BF), w_out[l].astype(BF), row1(post_mix_g[l]), row1(pre_ffn_g[l]),
                           rw, tm_merge, n_out, mod_idx_for(tm_merge))
        if not moe_layer:
            x1, h2 = outs
            x_all = _ffn_call(h2, x1, mod, row1(post_ffn_g[l]), ffn_w1[l // 2], ffn_w3[l // 2], ffn_w2[l // 2],
                              tm, mod_idx)
        else:
            x1, h2, rt = outs
            tme = min(1024, n_out * TOP_K // N_EXPERTS)
            dest, src_tok, blk_e, nused = _route_plan(rt, tme)
            xs = _sc_gather_rows(h2, src_tok)
            y = _moe_call(blk_e, nused, xs, moe_w1[l // 2], moe_w3[l // 2], moe_w2[l // 2], tme)
            dest2 = dest.reshape(n_out, TOP_K)
            ya = _sc_gather_rows(y, dest2[:, 0])
            yb = _sc_gather_rows(y, dest2[:, 1])
            x_all = _combine_call(ya, yb, rt, x1, mod, row1(post_ffn_g[l]), tm, mod_idx)
    return x_all[:n_lat].reshape(bsz, seq, d)
```

```python
import functools
import math

import numpy as np
import jax
import jax.numpy as jnp
from jax import lax
from jax.experimental import pallas as pl
from jax.experimental.pallas import tpu as pltpu
from jax.experimental.pallas import tpu_sc as plsc

BF = jnp.bfloat16
F32 = jnp.float32

RMS_EPS = 1e-6
GRID_W = 64
POOL_WINDOWS = (2, 4, 8, 16)
LRU_BLOCKS = 8
CONV_W = 4
LRU_C = 8.0
HEADS = 8
Q_LORA = 384
KV_LORA = 256
QK_NOPE = 128
QK_ROPE = 64
V_DIM = 128
MLA_SCALE = (QK_NOPE + QK_ROPE) ** -0.5
Q_SCALE = MLA_SCALE * math.log2(math.e)
ROPE_FREQS = QK_ROPE // 4
ROPE_THETA = 10000.0
N_EXPERTS = 8
TOP_K = 2

LANE = 128
HALO = 16
SEQ_TILE = 256
HEAD_W = 256
VT_ROWS = V_DIM + HALO
INPROJ_TN = 1024
FFN_TF = 768
VMEM_LIMIT = 48 * 1024 * 1024


def _cparams(*sem, flags=None):
    return pltpu.CompilerParams(dimension_semantics=sem, vmem_limit_bytes=VMEM_LIMIT, flags=flags)


def _rms(x, g):
    ms = jnp.mean(x * x, axis=-1, keepdims=True)
    return x * lax.rsqrt(ms + RMS_EPS) * g


def _sigmoid(x):
    return 0.5 * jnp.tanh(0.5 * x) + 0.5


def _silu(x):
    return x * _sigmoid(x)


def _gelu_tanh(x):
    return 0.5 * x * (1.0 + jnp.tanh(math.sqrt(2.0 / math.pi) * (x + 0.044715 * (x * x * x))))


def _dot(a, b):
    return jnp.dot(a, b, preferred_element_type=F32)


def _pack_rows(x):
    w = x.shape[1] // 2
    lo = lax.bitcast_convert_type(x[:, :w].astype(BF).astype(F32), jnp.uint32)
    hi = lax.bitcast_convert_type(x[:, w:].astype(BF).astype(F32), jnp.uint32)
    return (lo >> 16) | hi


def _unpack_rows(u):
    lo = lax.bitcast_convert_type(u << 16, F32)
    hi = lax.bitcast_convert_type(u & jnp.uint32(0xFFFF0000), F32)
    return lo, hi


def _mod_kernel(c_ref, w_ref, b_ref, o_ref):
    s = _silu(c_ref[...])
    o_ref[...] = _dot(s.astype(BF), w_ref[...].astype(BF)) + b_ref[...]


def _mod_call(cc, mod_w, mod_b):
    depth, d, n6 = mod_w.shape
    tn = 1536
    return pl.pallas_call(
        _mod_kernel,
        grid=(depth, n6 // tn),
        in_specs=[pl.BlockSpec((8, d), lambda l, j: (0, 0)),
                  pl.BlockSpec((None, d, tn), lambda l, j: (l, 0, j)),
                  pl.BlockSpec((None, 1, tn), lambda l, j: (l, 0, j))],
        out_specs=pl.BlockSpec((None, 8, tn), lambda l, j: (l, 0, j)),
        out_shape=jax.ShapeDtypeStruct((depth, 8, n6), F32),
        compiler_params=_cparams("parallel", "arbitrary"),
        name="mod",
    )(cc, mod_w, mod_b.reshape(depth, 1, n6))


def _inproj_kernel(x_ref, mod_ref, g_ref, w_ref, z_ref, h_scr):
    @pl.when(pl.program_id(1) == 0)
    def _():
        h = _rms(x_ref[...], g_ref[...])
        h = h * (1.0 + mod_ref[1:2, :]) + mod_ref[0:1, :]
        h_scr[...] = h.astype(BF)

    z_ref[...] = _dot(h_scr[...], w_ref[...]).astype(BF)


def _inproj_call(x, mod, g, w, tm, mod_idx):
    m, d = x.shape
    n = w.shape[1]
    tn = INPROJ_TN
    return pl.pallas_call(
        _inproj_kernel,
        grid=(m // tm, n // tn),
        in_specs=[pl.BlockSpec((tm, d), lambda i, j: (i, 0)),
                  pl.BlockSpec((None, 6, d), lambda i, j: (mod_idx(i), 0, 0)),
                  pl.BlockSpec((1, d), lambda i, j: (0, 0)),
                  pl.BlockSpec((d, tn), lambda i, j: (0, j))],
        out_specs=pl.BlockSpec((tm, tn), lambda i, j: (i, j)),
        out_shape=jax.ShapeDtypeStruct((m, n), BF),
        scratch_shapes=[pltpu.VMEM((tm, d), BF)],
        compiler_params=_cparams("parallel", "arbitrary"),
        name="inproj",
    )(x, mod, g, w)


def _rope(x, cos, sin):
    return x * cos + pltpu.roll(x, LANE // 2, 1) * sin


_NT = (((1,), (1,)), ((), ()))


def _qkv_kernel(z_ref, cos_ref, sin_ref, cost_ref, sint_ref, qg_ref, kvg_ref, wuqt_ref, wk_ref, wvt_ref,
                qt_ref, k_ref, vt_ref):
    z = z_ref[...]
    cq = z[:, :Q_LORA].astype(F32)
    ckv = z[:, Q_LORA:Q_LORA + KV_LORA].astype(F32)
    kr = z[:, Q_LORA + KV_LORA:].astype(F32)
    cqn = _rms(cq, qg_ref[...]).astype(BF)
    ckvn = _rms(ckv, kvg_ref[...]).astype(BF)
    qt = lax.dot_general(wuqt_ref[...], cqn, _NT, preferred_element_type=F32)
    kn = _dot(ckvn, wk_ref[...])
    vt = lax.dot_general(wvt_ref[...], ckvn, _NT, preferred_element_type=F32)
    krot = _rope(kr, cos_ref[...], sin_ref[...]).astype(BF)
    cost = cost_ref[...]
    sint = sint_ref[...]
    for h in range(HEADS):
        c0 = h * HEAD_W
        c1 = c0 + QK_NOPE
        c2 = c1 + QK_ROPE
        qt_ref[c0:c1, :] = (qt[c0:c1, :] * Q_SCALE).astype(BF)
        qt_ref[c1:c2, :] = ((qt[c1:c2, :] * cost + qt[c2:c0 + HEAD_W, :] * sint) * Q_SCALE).astype(BF)
        qt_ref[c2:c0 + HEAD_W, :] = jnp.zeros((QK_ROPE, qt.shape[1]), BF)
        k_ref[h, :, 0:QK_NOPE] = kn[:, h * QK_NOPE:(h + 1) * QK_NOPE].astype(BF)
        k_ref[h, :, QK_NOPE:HEAD_W] = krot
        vt_ref[h, 0:V_DIM, :] = vt[h * V_DIM:(h + 1) * V_DIM, :].astype(BF)
        vt_ref[h, V_DIM:VT_ROWS, :] = jnp.ones((VT_ROWS - V_DIM, vt.shape[1]), BF)


def _qkv_call(z, cos_t, sin_t, qg, kvg, wuqt, wk, wvt, bsz, seq, ctx_len, z_off):
    m = z.shape[0]
    cos_tt = cos_t[:, :QK_ROPE].T
    sin_tt = sin_t[:, :QK_ROPE].T
    ts = SEQ_TILE
    nt = seq // ts
    nlat = bsz * nt
    lk = seq + ctx_len
    zw = Q_LORA + KV_LORA + LANE
    assert z_off % zw == 0
    zcol = z_off // zw

    def tab_idx(i):
        return (jnp.where(i < nlat, i % nt, nt), 0)

    def kv_idx(i):
        return (jnp.where(i < nlat, i // nt, i - nlat), 0, jnp.where(i < nlat, i % nt, nt), 0)

    def vt_idx(i):
        return (jnp.where(i < nlat, i // nt, i - nlat), 0, 0, jnp.where(i < nlat, i % nt, nt))

    return pl.pallas_call(
        _qkv_kernel,
        grid=(m // ts,),
        in_specs=[pl.BlockSpec((ts, zw), lambda i: (i, zcol)),
                  pl.BlockSpec((ts, LANE), tab_idx),
                  pl.BlockSpec((ts, LANE), tab_idx),
                  pl.BlockSpec((QK_ROPE, ts), lambda i: tab_idx(i)[::-1]),
                  pl.BlockSpec((QK_ROPE, ts), lambda i: tab_idx(i)[::-1]),
                  pl.BlockSpec((1, Q_LORA), lambda i: (0, 0)),
                  pl.BlockSpec((1, KV_LORA), lambda i: (0, 0)),
                  pl.BlockSpec(wuqt.shape, lambda i: (0, 0)),
                  pl.BlockSpec(wk.shape, lambda i: (0, 0)),
                  pl.BlockSpec(wvt.shape, lambda i: (0, 0))],
        out_specs=[pl.BlockSpec((HEADS * HEAD_W, ts), lambda i: (0, i)),
                   pl.BlockSpec((None, HEADS, ts, HEAD_W), kv_idx),
                   pl.BlockSpec((None, HEADS, VT_ROWS, ts), vt_idx)],
        out_shape=[jax.ShapeDtypeStruct((HEADS * HEAD_W, m), BF),
                   jax.ShapeDtypeStruct((bsz, HEADS, lk, HEAD_W), BF),
                   jax.ShapeDtypeStruct((bsz, HEADS, VT_ROWS, lk), BF)],
        compiler_params=_cparams("parallel"),
        name="qkv_up",
    )(z, cos_t, sin_t, cos_tt, sin_tt, qg, kvg, wuqt, wk, wvt)


def _col_reduce(x, pair, red):
    n = x.shape[0] // 4
    a = pair(pair(x[0:n], x[n:2 * n]), pair(x[2 * n:3 * n], x[3 * n:4 * n]))
    return red(a, axis=0, keepdims=True)


ATTN_LAG_LIMIT = 20.0


def _attn_kernel(qt_ref, k_ref, vt_ref, o_ref, p_scr, *, chunks, kp):
    qt = qt_ref[...]
    tq = qt.shape[1]

    nk = len(chunks)

    def scores(c):
        k0, kn = chunks[c]
        return _dot(k_ref[k0:k0 + kn, :], qt)

    def values(c):
        k0, kn = chunks[c]
        return _dot(vt_ref[:, k0:k0 + kn], p_scr[c % 2, 0:kn, :])

    def finish(a):
        o_ref[...] = (a[0:V_DIM] / a[V_DIM:V_DIM + 1]).T.astype(o_ref.dtype)

    s = scores(0)
    ref = _col_reduce(s, jnp.maximum, jnp.max)
    p_scr[0, 0:chunks[0][1], :] = jnp.exp2((s - ref).astype(BF))
    acc = alpha = None
    lag = jnp.zeros((1, tq), F32)
    for c in range(1, nk):
        s = scores(c)
        p_scr[c % 2, 0:chunks[c][1], :] = jnp.exp2((s - ref).astype(BF))
        mc = _col_reduce(s, jnp.maximum, jnp.max)
        pv = values(c - 1)
        acc = pv if acc is None else acc + pv
        if alpha is not None:
            acc = acc * alpha
        lag = jnp.maximum(lag, mc - ref)
        new_ref = jnp.maximum(ref, mc)
        alpha = jnp.exp2(ref - new_ref)
        ref = new_ref
    pv = values(nk - 1)
    acc = pv if acc is None else acc + pv
    over = jnp.max(lag) > ATTN_LAG_LIMIT

    @pl.when(jnp.logical_not(over))
    def _():
        finish(acc)

    @pl.when(over)
    def _():
        def body(j, carry):
            m, a = carry
            r0 = pl.multiple_of(j * kp, kp)
            sj = _dot(k_ref[pl.ds(r0, kp), :], qt)
            m_new = jnp.maximum(m, jnp.max(sj, axis=0, keepdims=True))
            pj = jnp.exp2((sj - m_new).astype(BF))
            a = jnp.exp2(m - m_new) * a + _dot(vt_ref[:, pl.ds(r0, kp)], pj)
            return m_new, a

        init = (jnp.full((1, tq), -1e30, F32), jnp.zeros((VT_ROWS, tq), F32))
        finish(lax.fori_loop(0, k_ref.shape[0] // kp, body, init)[1])


ATTN_FIRST = 256
ATTN_CHUNK = 512


def _attn_call(q, k, v, *, bsz, row_blk0, nq, tq, kblk, klen):
    rest = klen - ATTN_FIRST
    step = ATTN_CHUNK if rest % ATTN_CHUNK == 0 else ATTN_FIRST
    assert rest % step == 0
    chunks = ((0, ATTN_FIRST),) + tuple((ATTN_FIRST + i * step, step) for i in range(rest // step))
    tk = max(n for _, n in chunks)
    kern = functools.partial(_attn_kernel, chunks=chunks, kp=ATTN_FIRST)
    return pl.pallas_call(
        kern,
        grid=(bsz, HEADS, nq),
        in_specs=[pl.BlockSpec((HEAD_W, tq), lambda b, h, i: (h, row_blk0 + b * nq + i)),
                  pl.BlockSpec((None, None, klen, HEAD_W), lambda b, h, i: (b, h, kblk, 0)),
                  pl.BlockSpec((None, None, VT_ROWS, klen), lambda b, h, i: (b, h, 0, kblk))],
        out_specs=pl.BlockSpec((tq, V_DIM), lambda b, h, i: (b * nq + i, h)),
        out_shape=jax.ShapeDtypeStruct((bsz * nq * tq, HEADS * V_DIM), BF),
        scratch_shapes=[pltpu.VMEM((2, tk, tq), BF)],
        compiler_params=_cparams("parallel", "parallel", "arbitrary"),
        name="attn",
    )(q, k, v)


def _seq_flags(i, nlat, nt):
    is_ctx = i >= nlat
    t = jnp.where(is_ctx, 0, i % nt)
    first = jnp.logical_or(is_ctx, t == 0)
    last = jnp.logical_or(is_ctx, t == nt - 1)
    return is_ctx, t, first, last


def _pool_kernel(x_ref, xp_ref, xn_ref, pw_ref, ps_ref, o_ref, *, nlat, nt, seq, ctx_len):
    ts = x_ref.shape[0]
    is_ctx, t, first, last = _seq_flags(pl.program_id(0), nlat, nt)
    seq_len = jnp.where(is_ctx, ctx_len, seq)
    x = x_ref[...]
    xp = jnp.where(first, jnp.zeros_like(xp_ref[...]), xp_ref[...])
    xn = jnp.where(last, jnp.zeros_like(xn_ref[...]), xn_ref[...])
    xe = jnp.concatenate([xp, x, xn], axis=0)
    tpos = t * ts + lax.broadcasted_iota(jnp.int32, (ts, 1), 0)
    rel = (lax.broadcasted_iota(jnp.int32, (ts, ts + 2 * HALO), 1) - HALO
           - lax.broadcasted_iota(jnp.int32, (ts, ts + 2 * HALO), 0))
    gw = x.shape[1] // len(POOL_WINDOWS)
    cols = [slice(g * gw, (g + 1) * gw) for g in range(len(POOL_WINDOWS))]
    sums = []
    for cs, w in zip(cols, POOL_WINDOWS):
        band = jnp.where(rel >= -(w // 2), jnp.where(rel < w - w // 2, 1.0, 0.0), 0.0).astype(BF)
        sums.append(_dot(band, xe[:, cs]))
    for g, (cs, w) in enumerate(zip(cols, POOL_WINDOWS)):
        cnt = (jnp.minimum(tpos + (w - w // 2), seq_len) - jnp.maximum(tpos - w // 2, 0)).astype(F32)
        mean_minus = sums[g] / cnt - x[:, cs].astype(F32)
        o_ref[:, cs] = (_dot(mean_minus.astype(BF), pw_ref[g]) * ps_ref[:, cs]).astype(BF)


def _halo_specs(ts, width, col, row_blk, m):
    r = ts // HALO
    nh = m // HALO
    prev = pl.BlockSpec((HALO, width), lambda *a: (jnp.maximum(row_blk(*a) * r - 1, 0), col))
    nxt = pl.BlockSpec((HALO, width), lambda *a: (jnp.minimum((row_blk(*a) + 1) * r, nh - 1), col))
    return prev, nxt


def _pool_call(z, pool_w, pool_scale, bsz, seq, ctx_len, n_rows):
    m = z.shape[0]
    ts = SEQ_TILE
    nt = seq // ts
    nlat = bsz * nt
    width = pool_scale.shape[1]
    prev, nxt = _halo_specs(ts, width, 0, lambda i: i, m)
    kern = functools.partial(_pool_kernel, nlat=nlat, nt=nt, seq=seq, ctx_len=ctx_len)
    return pl.pallas_call(
        kern,
        grid=(n_rows // ts,),
        in_specs=[pl.BlockSpec((ts, width), lambda i: (i, 0)), prev, nxt,
                  pl.BlockSpec(pool_w.shape, lambda i: (0, 0, 0)),
                  pl.BlockSpec((1, width), lambda i: (0, 0))],
        out_specs=pl.BlockSpec((ts, width), lambda i: (i, 0)),
        out_shape=jax.ShapeDtypeStruct((n_rows, width), BF),
        compiler_params=_cparams("parallel"),
        name="pool",
    )(z, z, z, pool_w, pool_scale)


def _lru_kernel(x_ref, xp_ref, xn_ref, cw_ref, cb_ref, wa_ref, ba_ref, wx_ref, bx_ref, lam_ref, o_ref,
                xf_scr, hl_scr, ca_scr, ga_scr, gb_scr, hp_scr, h_scr, *, reverse, nt):
    ts, width = x_ref.shape
    ng = ts // 8
    bw = width // LRU_BLOCKS
    s = pl.program_id(1)
    t = (nt - s) if reverse else (s - 1)
    first = jnp.logical_or(s == 0, t == 0)
    last = jnp.logical_or(s == 0, t == nt - 1)

    @pl.when(s == 0)
    def _():
        h_scr[...] = jnp.zeros_like(h_scr)

    left = CONV_W // 2
    xp = xp_ref[...].astype(F32)[HALO - 8:HALO]
    xn = xn_ref[...].astype(F32)[0:8]
    xp = jnp.where(first, jnp.zeros_like(xp), xp)
    xn = jnp.where(last, jnp.zeros_like(xn), xn)
    lam = lam_ref[...]
    neg = -lam
    softplus = jnp.maximum(neg, 0.0) + jnp.log1p(jnp.exp(-jnp.abs(neg)))
    coef = (-LRU_C * math.log2(math.e)) * softplus
    row = lax.broadcasted_iota(jnp.int32, (ng, bw), 0)
    slab = lambda n, j: (n, pl.ds(j, ng, stride=8), slice(None))
    order = range(7, -1, -1) if reverse else range(8)

    for n in range(LRU_BLOCKS):
        cs = slice(n * bw, (n + 1) * bw)
        xf_scr[n] = x_ref[:, cs].astype(F32)
        xs = {j: xf_scr[slab(n, j)] for j in range(8)}
        for j in range(-left, 0):
            xs[j] = jnp.where(row == 0, xp[8 + j:9 + j, cs], pltpu.roll(xs[8 + j], 1, 0))
        for j in range(8, 8 + CONV_W - 1 - left):
            xs[j] = jnp.where(row == ng - 1, xn[j - 8:j - 7, cs], pltpu.roll(xs[j - 8], ng - 1, 0))
        us = []
        for j in range(8):
            u = cb_ref[:, cs] + cw_ref[0:1, cs] * xs[j - left]
            for k in range(1, CONV_W):
                u = u + cw_ref[k:k + 1, cs] * xs[j - left + k]
            us.append(u)
        un = jnp.concatenate(us, axis=0)
        ub = un.astype(BF)
        r = _sigmoid(_dot(ub, wa_ref[n]) + ba_ref[:, cs])
        gi = _sigmoid(_dot(ub, wx_ref[n]) + bx_ref[:, cs])
        a = jnp.exp2(r * coef[:, cs])
        om = 1.0 - a * a
        b = (om * lax.rsqrt(jnp.maximum(om, 1e-30))) * (gi * un)
        hl = ca = None
        for j in order:
            aj = a[j * ng:(j + 1) * ng]
            bj = b[j * ng:(j + 1) * ng]
            hl, ca = (bj, aj) if hl is None else (aj * hl + bj, aj * ca)
            hl_scr[n, j * ng:(j + 1) * ng, :] = hl
            ca_scr[n, j * ng:(j + 1) * ng, :] = ca
        ga_scr[pl.ds(n, ng, stride=LRU_BLOCKS), :] = ca
        gb_scr[pl.ds(n, ng, stride=LRU_BLOCKS), :] = hl

    h = h_scr[...]
    for g in (range(ng - 1, -1, -1) if reverse else range(ng)):
        gs = slice(g * LRU_BLOCKS, (g + 1) * LRU_BLOCKS)
        hp_scr[gs, :] = h
        h = ga_scr[gs, :] * h + gb_scr[gs, :]
    h_scr[...] = h

    for n in range(LRU_BLOCKS):
        hp = hp_scr[pl.ds(n, ng, stride=LRU_BLOCKS), :]
        for j in range(8):
            xf_scr[slab(n, j)] = hl_scr[n, j * ng:(j + 1) * ng, :] + ca_scr[n, j * ng:(j + 1) * ng, :] * hp
        o_ref[:, n * bw:(n + 1) * bw] = xf_scr[n].astype(o_ref.dtype)


def _lru_call(z, conv_w, conv_b, wa, ba, wx, bx, lam, bsz, seq, reverse):
    m = z.shape[0]
    ts = SEQ_TILE
    nt = seq // ts
    nlat = bsz * nt
    width = conv_b.shape[1]

    def row_blk(b, s):
        t = (nt - s) if reverse else (s - 1)
        return jnp.where(s == 0, nlat + b, b * nt + t)

    prev, nxt = _halo_specs(ts, width, 1, row_blk, m)
    vec = lambda shape: pl.BlockSpec(shape, lambda b, s: (0,) * len(shape))
    kern = functools.partial(_lru_kernel, reverse=reverse, nt=nt)
    return pl.pallas_call(
        kern,
        grid=(bsz, nt + 1),
        in_specs=[pl.BlockSpec((ts, width), lambda b, s: (row_blk(b, s), 1)), prev, nxt,
                  vec(conv_w.shape), vec(conv_b.shape), vec(wa.shape), vec(ba.shape),
                  vec(wx.shape), vec(bx.shape), vec(lam.shape)],
        out_specs=pl.BlockSpec((ts, width), lambda b, s: (row_blk(b, s), 0)),
        out_shape=jax.ShapeDtypeStruct((m, width), BF),
        scratch_shapes=[pltpu.VMEM((LRU_BLOCKS, ts, width // LRU_BLOCKS), F32)] * 3
        + [pltpu.VMEM((ts // 8 * LRU_BLOCKS, width // LRU_BLOCKS), F32)] * 3
        + [pltpu.VMEM((LRU_BLOCKS, width // LRU_BLOCKS), F32)],
        compiler_params=_cparams("parallel", "arbitrary"),
        name="lru_bwd" if reverse else "lru_fwd",
    )(z, z, z, conv_w, conv_b, wa, ba, wx, bx, lam)


def _merge_kernel(x_ref, mod_ref, mp_ref, hf_ref, hb_ref, lg_ref, o_ref, gt_ref, wp_ref, wl_ref, wm_ref, wo_ref,
                  g1_ref, g2_ref, *rest, route):
    if route:
        rw_ref, x1_ref, h2_ref, rt_ref = rest
    else:
        x1_ref, h2_ref = rest
    tm, d = x_ref.shape
    halves = [slice(0, tm // 2), slice(tm // 2, tm)]

    def branches(rs):
        y_pool = _dot(mp_ref[rs, :], wp_ref[...])
        lru_in = (hf_ref[rs, :].astype(F32) + hb_ref[rs, :].astype(F32)) * _gelu_tanh(lg_ref[rs, :].astype(F32))
        y_lru = _dot(lru_in.astype(BF), wl_ref[...])
        y_mla = _dot(o_ref[rs, :], wm_ref[...])
        return y_pool, y_lru, y_mla

    def mixed(rs, ys):
        mix = (_sigmoid(gt_ref[rs, 0:d].astype(F32)) * ys[0]
               + _sigmoid(gt_ref[rs, d:2 * d].astype(F32)) * ys[1]
               + _sigmoid(gt_ref[rs, 2 * d:3 * d].astype(F32)) * ys[2])
        return _dot(mix.astype(BF), wo_ref[...])

    ys = [branches(rs) for rs in halves]
    outs = [mixed(rs, y) for rs, y in zip(halves, ys)]
    for rs, y in zip(halves, outs):
        x1 = x_ref[rs, :] + mod_ref[2:3, :] * _rms(y, g1_ref[...])
        x1_ref[rs, :] = x1
        h2 = _rms(x1, g2_ref[...]) * (1.0 + mod_ref[4:5, :]) + mod_ref[3:4, :]
        if route:
            h2_ref[rs, :] = _pack_rows(h2)
            _route_rows(h2, rw_ref, rt_ref, rs)
        else:
            h2_ref[rs, :] = h2.astype(h2_ref.dtype)


def _route_rows(h2, rw_ref, rt_ref, rs):
    logit = [jnp.sum(h2 * rw_ref[e:e + 1, :], axis=1, keepdims=True) for e in range(N_EXPERTS)]
    v1, i1 = logit[0], jnp.zeros_like(logit[0])
    for e in range(1, N_EXPERTS):
        upd = logit[e] > v1
        v1 = jnp.where(upd, logit[e], v1)
        i1 = jnp.where(upd, float(e), i1)
    v2, i2 = jnp.full_like(v1, -jnp.inf), jnp.zeros_like(v1)
    for e in range(N_EXPERTS):
        cand = jnp.where(i1 == float(e), -jnp.inf, logit[e])
        upd = cand > v2
        v2 = jnp.where(upd, cand, v2)
        i2 = jnp.where(upd, float(e), i2)
    ex = jnp.exp(v2 - v1)
    gate1 = 1.0 / (1.0 + ex)
    gate2 = ex / (1.0 + ex)
    col = lax.broadcasted_iota(jnp.int32, (h2.shape[0], rt_ref.shape[1]), 1)
    rt_ref[rs, :] = jnp.where(col == 0, i1, jnp.where(col == 1, i2,
                              jnp.where(col == 2, gate1, jnp.where(col == 3, gate2, 0.0))))


def _merge_call(x, mod, mp, hf, hb, z, o, wp, wl, wm, wo, g1, g2, rw, tm, n_rows, mod_idx):
    d = x.shape[1]
    route = rw is not None
    row = lambda c: pl.BlockSpec((tm, d), lambda i: (i, c))
    full = lambda a: pl.BlockSpec(a.shape, lambda i: (0,) * a.ndim)
    in_specs = [row(0), pl.BlockSpec((None, 6, d), lambda i: (mod_idx(i), 0, 0)),
                row(0), row(0), row(0), row(2), row(0), pl.BlockSpec((tm, 3 * d), lambda i: (i, 1)),
                full(wp), full(wl), full(wm), full(wo), full(g1), full(g2)]
    args = [x, mod, mp, hf, hb, z, o, z, wp, wl, wm, wo, g1, g2]
    out_specs = [row(0), row(0)]
    out_shape = [jax.ShapeDtypeStruct((n_rows, d), F32), jax.ShapeDtypeStruct((n_rows, d), BF)]
    if route:
        out_specs[1] = pl.BlockSpec((tm, d // 2), lambda i: (i, 0))
        out_shape[1] = jax.ShapeDtypeStruct((n_rows, d // 2), jnp.uint32)
        in_specs.append(full(rw))
        args.append(rw)
        out_specs.append(pl.BlockSpec((tm, LANE), lambda i: (i, 0)))
        out_shape.append(jax.ShapeDtypeStruct((n_rows, LANE), F32))
    return pl.pallas_call(
        functools.partial(_merge_kernel, route=route),
        grid=(n_rows // tm,),
        in_specs=in_specs, out_specs=out_specs, out_shape=out_shape,
        compiler_params=_cparams("parallel"),
        name="merge",
    )(*args)


def _swiglu_step(x, w1, w3, w2):
    a = _dot(x, w1)
    b = _dot(x, w3)
    return _dot((_silu(a) * b).astype(BF), w2)


def _ffn_kernel(h_ref, x_ref, mod_ref, g_ref, w13_ref, w2_ref, o_ref):
    f = pl.program_id(1)
    tf = w2_ref.shape[0]

    @pl.when(f == 0)
    def _():
        o_ref[...] = jnp.zeros_like(o_ref)

    ab = _dot(h_ref[...], w13_ref[...])
    o_ref[...] += _dot((_silu(ab[:, :tf]) * ab[:, tf:]).astype(BF), w2_ref[...])

    @pl.when(f == pl.num_programs(1) - 1)
    def _():
        o_ref[...] = x_ref[...] + mod_ref[5:6, :] * _rms(o_ref[...], g_ref[...])


def _ffn_call(h2, x1, mod, g, w1, w3, w2, tm, mod_idx):
    m, d = x1.shape
    tf = FFN_TF
    pad = -w1.shape[1] % tf
    w1, w3 = (jnp.pad(w, ((0, 0), (0, pad))) for w in (w1, w3))
    ff = w1.shape[1]
    w13 = jnp.stack([w1.reshape(d, ff // tf, tf), w3.reshape(d, ff // tf, tf)], axis=2).reshape(d, 2 * ff).astype(BF)
    w2 = jnp.pad(w2, ((0, pad), (0, 0))).astype(BF)
    return pl.pallas_call(
        _ffn_kernel,
        grid=(m // tm, ff // tf),
        in_specs=[pl.BlockSpec((tm, d), lambda i, f: (i, 0)),
                  pl.BlockSpec((tm, d), lambda i, f: (i, 0)),
                  pl.BlockSpec((None, 6, d), lambda i, f: (mod_idx(i), 0, 0)),
                  pl.BlockSpec((1, d), lambda i, f: (0, 0)),
                  pl.BlockSpec((d, 2 * tf), lambda i, f: (0, f)),
                  pl.BlockSpec((tf, d), lambda i, f: (f, 0))],
        out_specs=pl.BlockSpec((tm, d), lambda i, f: (i, 0)),
        out_shape=jax.ShapeDtypeStruct((m, d), F32),
        compiler_params=_cparams("parallel", "arbitrary"),
        name="ffn",
    )(h2, x1, mod, g, w13, w2)


def _moe_kernel(blk_e_ref, nused_ref, x_ref, w1_ref, w3_ref, w2_ref, o_ref, xb, acc):
    i = pl.program_id(0)
    f = pl.program_id(1)
    used = i < nused_ref[0]
    w = x_ref.shape[1]

    @pl.when(f == 0)
    def _():
        acc[...] = jnp.zeros_like(acc)
        lo, hi = _unpack_rows(x_ref[...])
        xb[:, :w] = lo.astype(BF)
        xb[:, w:] = hi.astype(BF)

    @pl.when(used)
    def _():
        acc[...] += _swiglu_step(xb[...], w1_ref[...].astype(BF), w3_ref[...].astype(BF), w2_ref[...].astype(BF))

    @pl.when(f == pl.num_programs(1) - 1)
    def _():
        o_ref[...] = _pack_rows(acc[...])


def _moe_call(blk_e, nused, xs, w1, w3, w2, tme):
    n_rows, dp = xs.shape
    d = w1.shape[1]
    ff = w1.shape[2]
    tf = 512
    nf = ff // tf

    def ftile(i, f, nu):
        return jnp.where(i < nu[0], f, nf - 1)

    return pl.pallas_call(
        _moe_kernel,
        grid_spec=pltpu.PrefetchScalarGridSpec(
            num_scalar_prefetch=2,
            grid=(n_rows // tme, nf),
            in_specs=[pl.BlockSpec((tme, dp), lambda i, f, be, nu: (i, 0)),
                      pl.BlockSpec((None, d, tf), lambda i, f, be, nu: (be[i], 0, ftile(i, f, nu))),
                      pl.BlockSpec((None, d, tf), lambda i, f, be, nu: (be[i], 0, ftile(i, f, nu))),
                      pl.BlockSpec((None, tf, d), lambda i, f, be, nu: (be[i], ftile(i, f, nu), 0))],
            out_specs=pl.BlockSpec((tme, dp), lambda i, f, be, nu: (i, 0)),
            scratch_shapes=[pltpu.VMEM((tme, d), BF), pltpu.VMEM((tme, d), F32)]),
        out_shape=jax.ShapeDtypeStruct((n_rows, dp), jnp.uint32),
        compiler_params=_cparams("arbitrary", "arbitrary"),
        name="moe",
    )(blk_e, nused, xs, w1, w3, w2)


SC_CORES = 2
SC_SUBCORES = 16
SC_GATHER_ROWS = 128


def _sc_gather_rows(table, idx):
    n_idx = idx.shape[0]
    d = table.shape[1]
    workers = SC_CORES * SC_SUBCORES
    per_w = n_idx // workers
    rows = min(SC_GATHER_ROWS, per_w)
    assert n_idx % (8 * workers) == 0 and per_w % rows == 0
    mesh = plsc.VectorSubcoreMesh(core_axis_name="c", subcore_axis_name="s",
                                  num_cores=SC_CORES, num_subcores=SC_SUBCORES)

    def body(table_hbm, idx_hbm, out_hbm, idx_v, rows_v, sem):
        wid = lax.axis_index("s") * SC_CORES + lax.axis_index("c")
        base = wid * per_w

        @pl.loop(0, per_w // rows)
        def _(it):
            off = pl.multiple_of(base + it * rows, 8)
            pltpu.sync_copy(idx_hbm.at[pl.ds(off, rows)], idx_v)
            pltpu.async_copy(table_hbm.at[idx_v], rows_v, sem).wait()
            pltpu.sync_copy(rows_v, out_hbm.at[pl.ds(off, rows)])

    return pl.kernel(
        body,
        out_type=jax.ShapeDtypeStruct((n_idx, d), table.dtype),
        mesh=mesh,
        scratch_types=[pltpu.VMEM((rows,), jnp.int32), pltpu.VMEM((rows, d), table.dtype),
                       pltpu.SemaphoreType.DMA],
        name="sc_gather",
    )(table, idx)


def _combine_kernel(ya_ref, yb_ref, rt_ref, x_ref, mod_ref, g_ref, o_ref):
    g1 = rt_ref[:, 2:3]
    g2 = rt_ref[:, 3:4]
    a_lo, a_hi = _unpack_rows(ya_ref[...])
    b_lo, b_hi = _unpack_rows(yb_ref[...])
    f = jnp.concatenate([g1 * a_lo + g2 * b_lo, g1 * a_hi + g2 * b_hi], axis=1)
    o_ref[...] = x_ref[...] + mod_ref[5:6, :] * _rms(f, g_ref[...])


def _combine_call(ya, yb, rt, x1, mod, g, tm, mod_idx):
    n, d = x1.shape
    return pl.pallas_call(
        _combine_kernel,
        grid=(n // tm,),
        in_specs=[pl.BlockSpec((tm, d // 2), lambda i: (i, 0)),
                  pl.BlockSpec((tm, d // 2), lambda i: (i, 0)),
                  pl.BlockSpec((tm, LANE), lambda i: (i, 0)),
                  pl.BlockSpec((tm, d), lambda i: (i, 0)),
                  pl.BlockSpec((None, 6, d), lambda i: (mod_idx(i), 0, 0)),
                  pl.BlockSpec((1, d), lambda i: (0, 0))],
        out_specs=pl.BlockSpec((tm, d), lambda i: (i, 0)),
        out_shape=jax.ShapeDtypeStruct((n, d), F32),
        compiler_params=_cparams("parallel"),
        name="combine",
    )(ya, yb, rt, x1, mod, g)


def _route_plan(rt, tme):
    n = rt.shape[0]
    e_flat = jnp.concatenate([rt[:, k] for k in range(TOP_K)]).astype(jnp.int32)
    onehot = (e_flat[:, None] == jnp.arange(N_EXPERTS, dtype=jnp.int32)[None, :]).astype(jnp.int32)
    csum = jnp.cumsum(onehot, axis=0)
    counts = csum[-1]
    padded = (counts + tme - 1) // tme * tme
    pad_end = jnp.cumsum(padded)
    pad_start = pad_end - padded
    dest = jnp.sum(onehot * (csum + pad_start[None, :]), axis=1) - 1
    n_rows = n * TOP_K + N_EXPERTS * tme
    n_blk = n_rows // tme
    src_tok = (jnp.arange(n_rows, dtype=jnp.int32) % n).at[dest].set(
        jnp.arange(n * TOP_K, dtype=jnp.int32) % n, unique_indices=True)
    blk_start = jnp.arange(n_blk, dtype=jnp.int32) * tme
    blk_e = jnp.minimum(jnp.sum((pad_end[None, :] <= blk_start[:, None]).astype(jnp.int32), axis=1), N_EXPERTS - 1)
    nused = (pad_end[-1] // tme).astype(jnp.int32).reshape(1)
    return dest, src_tok, blk_e, nused


def _swap_perm():
    j = np.arange(QK_ROPE)
    axis, half, f = j // (2 * ROPE_FREQS), (j % (2 * ROPE_FREQS)) // ROPE_FREQS, j % ROPE_FREQS
    return axis * 2 * ROPE_FREQS + (1 - half) * ROPE_FREQS + f


def _rope_tables(seq, ctx_len):
    rows = seq // GRID_W
    row = jnp.repeat(jnp.arange(rows, dtype=F32), GRID_W)
    col = (jnp.arange(rows * GRID_W) % GRID_W).astype(F32)
    inv = ROPE_THETA ** (-jnp.arange(ROPE_FREQS, dtype=F32) / ROPE_FREQS)
    ang = jnp.stack([row[:, None] * inv, col[:, None] * inv], axis=1)
    cos, sin = jnp.cos(ang), jnp.sin(ang)
    cos64 = jnp.stack([cos, cos], axis=2).reshape(seq, QK_ROPE)
    sin64 = jnp.stack([-sin, sin], axis=2).reshape(seq, QK_ROPE)
    pad = jnp.zeros((seq, LANE - QK_ROPE), F32)
    cos_t = jnp.concatenate([cos64, pad], axis=1)
    sin_t = jnp.concatenate([sin64, pad], axis=1)
    ident = jnp.concatenate([jnp.ones((ctx_len, QK_ROPE), F32), jnp.zeros((ctx_len, LANE - QK_ROPE), F32)], axis=1)
    return jnp.concatenate([cos_t, ident], axis=0), jnp.concatenate([sin_t, jnp.zeros((ctx_len, LANE), F32)], axis=0)


def _prep_w_in(w):
    d = w.shape[0]
    pw = lw = d
    o = np.cumsum([0, pw, lw, lw, Q_LORA, KV_LORA, QK_ROPE, 3 * d])
    pool, lx, lg, cq, ckv, kr, gt = (w[:, o[i]:o[i + 1]] for i in range(7))
    cols = [pool, lx, lg, gt, cq, ckv, kr, kr[:, _swap_perm()]]
    n = sum(c.shape[1] for c in cols)
    cols.append(jnp.zeros((d, -n % INPROJ_TN), w.dtype))
    return jnp.concatenate(cols, axis=1).astype(BF)


def _prep_w_uq(w):
    qk = QK_NOPE + QK_ROPE
    w = w.reshape(w.shape[0], HEADS, qk)
    rope = w[:, :, QK_NOPE:]
    return jnp.concatenate([w, rope[:, :, _swap_perm()]], axis=2).reshape(w.shape[0], HEADS * HEAD_W).T.astype(BF)


def kernel(x, c, ctx, c_ctx, mod_w, mod_b, pre_mix_g, post_mix_g, pre_ffn_g, post_ffn_g, w_in, pool_w, pool_scale,
           pool_proj, conv_w, conv_b, gate_a_w, gate_a_b, gate_x_w, gate_x_b, lru_lambda, lru_proj, q_norm_g, w_uq,
           kv_norm_g, w_ukv, mla_proj, w_out, ffn_w1, ffn_w3, ffn_w2, router_w, moe_w1, moe_w3, moe_w2):
    bsz, seq, d = x.shape
    ctx_len = ctx.shape[1]
    depth = mod_w.shape[0]
    assert ctx_len == SEQ_TILE and seq % SEQ_TILE == 0 and seq % GRID_W == 0
    n_lat = bsz * seq
    n_ctx = bsz * ctx_len
    n_all = n_lat + n_ctx
    tm = min(1024, seq, n_ctx)
    assert seq % tm == 0 and n_ctx % tm == 0
    tm_merge = tm // 2

    def mod_idx_for(rows):
        return lambda i: jnp.where(i < n_lat // rows, i // (seq // rows), bsz)

    mod_idx = mod_idx_for(tm)

    cc = jnp.concatenate([c, c_ctx[None, :], jnp.zeros((8 - bsz - 1, d), F32)], axis=0)
    mods = _mod_call(cc, mod_w, mod_b)[:, :bsz + 1].reshape(depth, bsz + 1, 6, d)
    cos_t, sin_t = _rope_tables(seq, ctx_len)
    x_all = jnp.concatenate([x.reshape(n_lat, d), ctx.reshape(n_ctx, d)], axis=0)
    row1 = lambda v: v.reshape(1, -1)

    for l in range(depth):
        last = l == depth - 1
        n_out = n_lat if last else n_all
        mod = mods[l]
        z = _inproj_call(x_all, mod, row1(pre_mix_g[l]), _prep_w_in(w_in[l]), tm, mod_idx)
        wkv = w_ukv[l].reshape(KV_LORA, HEADS, QK_NOPE + V_DIM)
        wk = wkv[:, :, :QK_NOPE].reshape(KV_LORA, HEADS * QK_NOPE).astype(BF)
        wvt = wkv[:, :, QK_NOPE:].reshape(KV_LORA, HEADS * V_DIM).T.astype(BF)
        q, k, v = _qkv_call(z, cos_t, sin_t, row1(q_norm_g[l]), row1(kv_norm_g[l]), _prep_w_uq(w_uq[l]),
                            wk, wvt, bsz, seq, ctx_len, 6 * d)
        tq = min(512, seq)
        o = _attn_call(q, k, v, bsz=bsz, row_blk0=0, nq=seq // tq, tq=tq, kblk=0, klen=seq + ctx_len)
        if not last:
            o_ctx = _attn_call(q, k, v, bsz=bsz, row_blk0=n_lat // ctx_len, nq=1, tq=ctx_len,
                               kblk=seq // ctx_len, klen=ctx_len)
            o = jnp.concatenate([o, o_ctx], axis=0)
        mp = _pool_call(z, pool_w[l].astype(BF), row1(pool_scale[l]), bsz, seq, ctx_len, n_out)
        hs = []
        for dr in range(2):
            hs.append(_lru_call(z, conv_w[l], row1(conv_b[l]), gate_a_w[l, dr].astype(BF), row1(gate_a_b[l, dr]),
                                gate_x_w[l, dr].astype(BF), row1(gate_x_b[l, dr]), row1(lru_lambda[l, dr]),
                                bsz, seq, dr == 1))
        moe_layer = l % 2 == 1
        rw = None
        if moe_layer:
            rw = router_w[l // 2].T
        outs = _merge_call(x_all, mod, mp, hs[0], hs[1], z, o, pool_proj[l].astype(BF), lru_proj[l].astype(BF),
                           mla_proj[l].astype(BF), w_out[l].astype(BF), row1(post_mix_g[l]), row1(pre_ffn_g[l]),
                           rw, tm_merge, n_out, mod_idx_for(tm_merge))
        if not moe_layer:
            x1, h2 = outs
            x_all = _ffn_call(h2, x1, mod, row1(post_ffn_g[l]), ffn_w1[l // 2], ffn_w3[l // 2], ffn_w2[l // 2],
                              tm, mod_idx)
        else:
            x1, h2, rt = outs
            tme = min(1024, n_out * TOP_K // N_EXPERTS)
            dest, src_tok, blk_e, nused = _route_plan(rt, tme)
            xs = _sc_gather_rows(h2, src_tok)
            y = _moe_call(blk_e, nused, xs, moe_w1[l // 2], moe_w3[l // 2], moe_w2[l // 2], tme)
            ya = _sc_gather_rows(y, dest[:n_out])
            yb = _sc_gather_rows(y, dest[n_out:])
            x_all = _combine_call(ya, yb, rt, x1, mod, row1(post_ffn_g[l]), tm, mod_idx)
    return x_all[:n_lat].reshape(bsz, seq, d)
```

```python
import functools
import math

import numpy as np
import jax
import jax.numpy as jnp
from jax import lax
from jax.experimental import pallas as pl
from jax.experimental.pallas import tpu as pltpu
from jax.experimental.pallas import tpu_sc as plsc

BF = jnp.bfloat16
F32 = jnp.float32

RMS_EPS = 1e-6
GRID_W = 64
POOL_WINDOWS = (2, 4, 8, 16)
LRU_BLOCKS = 8
CONV_W = 4
LRU_C = 8.0
HEADS = 8
Q_LORA = 384
KV_LORA = 256
QK_NOPE = 128
QK_ROPE = 64
V_DIM = 128
MLA_SCALE = (QK_NOPE + QK_ROPE) ** -0.5
Q_SCALE = MLA_SCALE * math.log2(math.e)
ROPE_FREQS = QK_ROPE // 4
ROPE_THETA = 10000.0
N_EXPERTS = 8
TOP_K = 2

LANE = 128
HALO = 16
SEQ_TILE = 256
HEAD_W = 256
VT_ROWS = V_DIM + HALO
INPROJ_TN = 1024
FFN_TF = 768
VMEM_LIMIT = 48 * 1024 * 1024


def _cparams(*sem, flags=None):
    return pltpu.CompilerParams(dimension_semantics=sem, vmem_limit_bytes=VMEM_LIMIT, flags=flags)


def _rms(x, g):
    ms = jnp.mean(x * x, axis=-1, keepdims=True)
    return x * lax.rsqrt(ms + RMS_EPS) * g


def _sigmoid(x):
    return 0.5 * jnp.tanh(0.5 * x) + 0.5


def _silu(x):
    return x * _sigmoid(x)


def _gelu_tanh(x):
    return 0.5 * x * (1.0 + jnp.tanh(math.sqrt(2.0 / math.pi) * (x + 0.044715 * (x * x * x))))


def _dot(a, b):
    return jnp.dot(a, b, preferred_element_type=F32)


def _pack_rows(x):
    w = x.shape[1] // 2
    lo = lax.bitcast_convert_type(x[:, :w].astype(BF).astype(F32), jnp.uint32)
    hi = lax.bitcast_convert_type(x[:, w:].astype(BF).astype(F32), jnp.uint32)
    return (lo >> 16) | hi


def _unpack_rows(u):
    lo = lax.bitcast_convert_type(u << 16, F32)
    hi = lax.bitcast_convert_type(u & jnp.uint32(0xFFFF0000), F32)
    return lo, hi


def _mod_kernel(c_ref, w_ref, b_ref, o_ref):
    s = _silu(c_ref[...])
    o_ref[...] = _dot(s.astype(BF), w_ref[...].astype(BF)) + b_ref[...]


def _mod_call(cc, mod_w, mod_b):
    depth, d, n6 = mod_w.shape
    tn = 1536
    return pl.pallas_call(
        _mod_kernel,
        grid=(depth, n6 // tn),
        in_specs=[pl.BlockSpec((8, d), lambda l, j: (0, 0)),
                  pl.BlockSpec((None, d, tn), lambda l, j: (l, 0, j)),
                  pl.BlockSpec((None, 1, tn), lambda l, j: (l, 0, j))],
        out_specs=pl.BlockSpec((None, 8, tn), lambda l, j: (l, 0, j)),
        out_shape=jax.ShapeDtypeStruct((depth, 8, n6), F32),
        compiler_params=_cparams("parallel", "arbitrary"),
        name="mod",
    )(cc, mod_w, mod_b.reshape(depth, 1, n6))


def _row_pair_specs(tm, width, n_lat_rows):
    nb = n_lat_rows // tm
    lat = pl.BlockSpec((tm, width), lambda i, *_: (jnp.minimum(i, nb - 1), 0))
    ctx = pl.BlockSpec((tm, width), lambda i, *_: (jnp.maximum(i - nb, 0), 0))
    return lat, ctx


def _pick(is_ctx, lat_ref, ctx_ref, rs=slice(None)):
    if ctx_ref is None:
        return lat_ref[rs, :]
    return jnp.where(is_ctx, ctx_ref[rs, :], lat_ref[rs, :])


def _inproj_kernel(x_ref, *rest, n_lat_blocks):
    xc_ref = rest[0] if len(rest) == 6 else None
    mod_ref, g_ref, w_ref, z_ref, h_scr = rest[-5:]

    @pl.when(pl.program_id(1) == 0)
    def _():
        h = _rms(_pick(pl.program_id(0) >= n_lat_blocks, x_ref, xc_ref), g_ref[...])
        h = h * (1.0 + mod_ref[1:2, :]) + mod_ref[0:1, :]
        h_scr[...] = h.astype(BF)

    z_ref[...] = _dot(h_scr[...], w_ref[...]).astype(BF)


def _inproj_call(x, x_ctx, mod, g, w, tm, mod_idx):
    d = x.shape[1]
    m = x.shape[0] + (0 if x_ctx is None else x_ctx.shape[0])
    n = w.shape[1]
    tn = INPROJ_TN
    if x_ctx is None:
        x_specs, xs = [pl.BlockSpec((tm, d), lambda i, j: (i, 0))], [x]
    else:
        x_specs, xs = list(_row_pair_specs(tm, d, x.shape[0])), [x, x_ctx]
    return pl.pallas_call(
        functools.partial(_inproj_kernel, n_lat_blocks=x.shape[0] // tm),
        grid=(m // tm, n // tn),
        in_specs=x_specs + [pl.BlockSpec((None, 6, d), lambda i, j: (mod_idx(i), 0, 0)),
                            pl.BlockSpec((1, d), lambda i, j: (0, 0)),
                            pl.BlockSpec((d, tn), lambda i, j: (0, j))],
        out_specs=pl.BlockSpec((tm, tn), lambda i, j: (i, j)),
        out_shape=jax.ShapeDtypeStruct((m, n), BF),
        scratch_shapes=[pltpu.VMEM((tm, d), BF)],
        compiler_params=_cparams("parallel", "arbitrary"),
        name="inproj",
    )(*xs, mod, g, w)


def _rope(x, cos, sin):
    return x * cos + pltpu.roll(x, LANE // 2, 1) * sin


_NT = (((1,), (1,)), ((), ()))


def _qkv_kernel(z_ref, cos_ref, sin_ref, cost_ref, sint_ref, qg_ref, kvg_ref, wuqt_ref, wk_ref, wvt_ref,
                qt_ref, k_ref, vt_ref):
    z = z_ref[...]
    cq = z[:, :Q_LORA].astype(F32)
    ckv = z[:, Q_LORA:Q_LORA + KV_LORA].astype(F32)
    kr = z[:, Q_LORA + KV_LORA:].astype(F32)
    cqn = _rms(cq, qg_ref[...]).astype(BF)
    ckvn = _rms(ckv, kvg_ref[...]).astype(BF)
    qt = lax.dot_general(wuqt_ref[...], cqn, _NT, preferred_element_type=F32)
    kn = _dot(ckvn, wk_ref[...])
    vt = lax.dot_general(wvt_ref[...], ckvn, _NT, preferred_element_type=F32)
    krot = _rope(kr, cos_ref[...], sin_ref[...]).astype(BF)
    cost = cost_ref[...]
    sint = sint_ref[...]
    for h in range(HEADS):
        c0 = h * HEAD_W
        c1 = c0 + QK_NOPE
        c2 = c1 + QK_ROPE
        qt_ref[c0:c1, :] = (qt[c0:c1, :] * Q_SCALE).astype(BF)
        qt_ref[c1:c2, :] = ((qt[c1:c2, :] * cost + qt[c2:c0 + HEAD_W, :] * sint) * Q_SCALE).astype(BF)
        qt_ref[c2:c0 + HEAD_W, :] = jnp.zeros((QK_ROPE, qt.shape[1]), BF)
        k_ref[h, :, 0:QK_NOPE] = kn[:, h * QK_NOPE:(h + 1) * QK_NOPE].astype(BF)
        k_ref[h, :, QK_NOPE:HEAD_W] = krot
        vt_ref[h, 0:V_DIM, :] = vt[h * V_DIM:(h + 1) * V_DIM, :].astype(BF)
        vt_ref[h, V_DIM:VT_ROWS, :] = jnp.ones((VT_ROWS - V_DIM, vt.shape[1]), BF)


def _qkv_call(z, cos_t, sin_t, qg, kvg, wuqt, wk, wvt, bsz, seq, ctx_len, z_off):
    m = z.shape[0]
    cos_tt = cos_t[:, :QK_ROPE].T
    sin_tt = sin_t[:, :QK_ROPE].T
    ts = SEQ_TILE
    nt = seq // ts
    nlat = bsz * nt
    lk = seq + ctx_len
    zw = Q_LORA + KV_LORA + LANE
    assert z_off % zw == 0
    zcol = z_off // zw

    def tab_idx(i):
        return (jnp.where(i < nlat, i % nt, nt), 0)

    def kv_idx(i):
        return (jnp.where(i < nlat, i // nt, i - nlat), 0, jnp.where(i < nlat, i % nt, nt), 0)

    def vt_idx(i):
        return (jnp.where(i < nlat, i // nt, i - nlat), 0, 0, jnp.where(i < nlat, i % nt, nt))

    return pl.pallas_call(
        _qkv_kernel,
        grid=(m // ts,),
        in_specs=[pl.BlockSpec((ts, zw), lambda i: (i, zcol)),
                  pl.BlockSpec((ts, LANE), tab_idx),
                  pl.BlockSpec((ts, LANE), tab_idx),
                  pl.BlockSpec((QK_ROPE, ts), lambda i: tab_idx(i)[::-1]),
                  pl.BlockSpec((QK_ROPE, ts), lambda i: tab_idx(i)[::-1]),
                  pl.BlockSpec((1, Q_LORA), lambda i: (0, 0)),
                  pl.BlockSpec((1, KV_LORA), lambda i: (0, 0)),
                  pl.BlockSpec(wuqt.shape, lambda i: (0, 0)),
                  pl.BlockSpec(wk.shape, lambda i: (0, 0)),
                  pl.BlockSpec(wvt.shape, lambda i: (0, 0))],
        out_specs=[pl.BlockSpec((HEADS * HEAD_W, ts), lambda i: (0, i)),
                   pl.BlockSpec((None, HEADS, ts, HEAD_W), kv_idx),
                   pl.BlockSpec((None, HEADS, VT_ROWS, ts), vt_idx)],
        out_shape=[jax.ShapeDtypeStruct((HEADS * HEAD_W, m), BF),
                   jax.ShapeDtypeStruct((bsz, HEADS, lk, HEAD_W), BF),
                   jax.ShapeDtypeStruct((bsz, HEADS, VT_ROWS, lk), BF)],
        compiler_params=_cparams("parallel"),
        name="qkv_up",
    )(z, cos_t, sin_t, cos_tt, sin_tt, qg, kvg, wuqt, wk, wvt)


def _col_reduce(x, pair, red):
    n = x.shape[0] // 4
    a = pair(pair(x[0:n], x[n:2 * n]), pair(x[2 * n:3 * n], x[3 * n:4 * n]))
    return red(a, axis=0, keepdims=True)


ATTN_LAG_LIMIT = 20.0


def _attn_kernel(qt_ref, k_ref, vt_ref, o_ref, p_scr, *, chunks, kp):
    qt = qt_ref[...]
    tq = qt.shape[1]

    nk = len(chunks)

    def scores(c):
        k0, kn = chunks[c]
        return _dot(k_ref[k0:k0 + kn, :], qt)

    def values(c):
        k0, kn = chunks[c]
        return _dot(vt_ref[:, k0:k0 + kn], p_scr[c % 2, 0:kn, :])

    def finish(a):
        o_ref[...] = (a[0:V_DIM] / a[V_DIM:V_DIM + 1]).T.astype(o_ref.dtype)

    s = scores(0)
    ref = _col_reduce(s, jnp.maximum, jnp.max)
    p_scr[0, 0:chunks[0][1], :] = jnp.exp2((s - ref).astype(BF))
    acc = alpha = None
    lag = jnp.zeros((1, tq), F32)
    for c in range(1, nk):
        s = scores(c)
        p_scr[c % 2, 0:chunks[c][1], :] = jnp.exp2((s - ref).astype(BF))
        mc = _col_reduce(s, jnp.maximum, jnp.max)
        pv = values(c - 1)
        acc = pv if acc is None else acc + pv
        if alpha is not None:
            acc = acc * alpha
        lag = jnp.maximum(lag, mc - ref)
        new_ref = jnp.maximum(ref, mc)
        alpha = jnp.exp2(ref - new_ref)
        ref = new_ref
    pv = values(nk - 1)
    acc = pv if acc is None else acc + pv
    over = jnp.max(lag) > ATTN_LAG_LIMIT

    @pl.when(jnp.logical_not(over))
    def _():
        finish(acc)

    @pl.when(over)
    def _():
        def body(j, carry):
            m, a = carry
            r0 = pl.multiple_of(j * kp, kp)
            sj = _dot(k_ref[pl.ds(r0, kp), :], qt)
            m_new = jnp.maximum(m, jnp.max(sj, axis=0, keepdims=True))
            pj = jnp.exp2((sj - m_new).astype(BF))
            a = jnp.exp2(m - m_new) * a + _dot(vt_ref[:, pl.ds(r0, kp)], pj)
            return m_new, a

        init = (jnp.full((1, tq), -1e30, F32), jnp.zeros((VT_ROWS, tq), F32))
        finish(lax.fori_loop(0, k_ref.shape[0] // kp, body, init)[1])


ATTN_FIRST = 256
ATTN_CHUNK = 512


def _attn_call(q, k, v, *, bsz, row_blk0, nq, tq, kblk, klen):
    rest = klen - ATTN_FIRST
    step = ATTN_CHUNK if rest % ATTN_CHUNK == 0 else ATTN_FIRST
    assert rest % step == 0
    chunks = ((0, ATTN_FIRST),) + tuple((ATTN_FIRST + i * step, step) for i in range(rest // step))
    tk = max(n for _, n in chunks)
    kern = functools.partial(_attn_kernel, chunks=chunks, kp=ATTN_FIRST)
    return pl.pallas_call(
        kern,
        grid=(bsz, HEADS, nq),
        in_specs=[pl.BlockSpec((HEAD_W, tq), lambda b, h, i: (h, row_blk0 + b * nq + i)),
                  pl.BlockSpec((None, None, klen, HEAD_W), lambda b, h, i: (b, h, kblk, 0)),
                  pl.BlockSpec((None, None, VT_ROWS, klen), lambda b, h, i: (b, h, 0, kblk))],
        out_specs=pl.BlockSpec((tq, V_DIM), lambda b, h, i: (b * nq + i, h)),
        out_shape=jax.ShapeDtypeStruct((bsz * nq * tq, HEADS * V_DIM), BF),
        scratch_shapes=[pltpu.VMEM((2, tk, tq), BF)],
        compiler_params=_cparams("parallel", "parallel", "arbitrary"),
        name="attn",
    )(q, k, v)


def _seq_flags(i, nlat, nt):
    is_ctx = i >= nlat
    t = jnp.where(is_ctx, 0, i % nt)
    first = jnp.logical_or(is_ctx, t == 0)
    last = jnp.logical_or(is_ctx, t == nt - 1)
    return is_ctx, t, first, last


def _pool_kernel(x_ref, xp_ref, xn_ref, pw_ref, ps_ref, o_ref, *, nlat, nt, seq, ctx_len):
    ts = x_ref.shape[0]
    is_ctx, t, first, last = _seq_flags(pl.program_id(0), nlat, nt)
    seq_len = jnp.where(is_ctx, ctx_len, seq)
    x = x_ref[...]
    xp = jnp.where(first, jnp.zeros_like(xp_ref[...]), xp_ref[...])
    xn = jnp.where(last, jnp.zeros_like(xn_ref[...]), xn_ref[...])
    xe = jnp.concatenate([xp, x, xn], axis=0)
    tpos = t * ts + lax.broadcasted_iota(jnp.int32, (ts, 1), 0)
    rel = (lax.broadcasted_iota(jnp.int32, (ts, ts + 2 * HALO), 1) - HALO
           - lax.broadcasted_iota(jnp.int32, (ts, ts + 2 * HALO), 0))
    gw = x.shape[1] // len(POOL_WINDOWS)
    cols = [slice(g * gw, (g + 1) * gw) for g in range(len(POOL_WINDOWS))]
    sums = []
    for cs, w in zip(cols, POOL_WINDOWS):
        band = jnp.where(rel >= -(w // 2), jnp.where(rel < w - w // 2, 1.0, 0.0), 0.0).astype(BF)
        sums.append(_dot(band, xe[:, cs]))
    for g, (cs, w) in enumerate(zip(cols, POOL_WINDOWS)):
        cnt = (jnp.minimum(tpos + (w - w // 2), seq_len) - jnp.maximum(tpos - w // 2, 0)).astype(F32)
        mean_minus = sums[g] / cnt - x[:, cs].astype(F32)
        o_ref[:, cs] = (_dot(mean_minus.astype(BF), pw_ref[g]) * ps_ref[:, cs]).astype(BF)


def _halo_specs(ts, width, col, row_blk, m):
    r = ts // HALO
    nh = m // HALO
    prev = pl.BlockSpec((HALO, width), lambda *a: (jnp.maximum(row_blk(*a) * r - 1, 0), col))
    nxt = pl.BlockSpec((HALO, width), lambda *a: (jnp.minimum((row_blk(*a) + 1) * r, nh - 1), col))
    return prev, nxt


def _pool_call(z, pool_w, pool_scale, bsz, seq, ctx_len, n_rows):
    m = z.shape[0]
    ts = SEQ_TILE
    nt = seq // ts
    nlat = bsz * nt
    width = pool_scale.shape[1]
    prev, nxt = _halo_specs(ts, width, 0, lambda i: i, m)
    kern = functools.partial(_pool_kernel, nlat=nlat, nt=nt, seq=seq, ctx_len=ctx_len)
    return pl.pallas_call(
        kern,
        grid=(n_rows // ts,),
        in_specs=[pl.BlockSpec((ts, width), lambda i: (i, 0)), prev, nxt,
                  pl.BlockSpec(pool_w.shape, lambda i: (0, 0, 0)),
                  pl.BlockSpec((1, width), lambda i: (0, 0))],
        out_specs=pl.BlockSpec((ts, width), lambda i: (i, 0)),
        out_shape=jax.ShapeDtypeStruct((n_rows, width), BF),
        compiler_params=_cparams("parallel"),
        name="pool",
    )(z, z, z, pool_w, pool_scale)


def _lru_kernel(x_ref, xp_ref, xn_ref, cw_ref, cb_ref, wa_ref, ba_ref, wx_ref, bx_ref, lam_ref, o_ref,
                xf_scr, hl_scr, ca_scr, ga_scr, gb_scr, hp_scr, h_scr, *, reverse, nt):
    ts, width = x_ref.shape
    ng = ts // 8
    bw = width // LRU_BLOCKS
    s = pl.program_id(1)
    t = (nt - s) if reverse else (s - 1)
    first = jnp.logical_or(s == 0, t == 0)
    last = jnp.logical_or(s == 0, t == nt - 1)

    @pl.when(s == 0)
    def _():
        h_scr[...] = jnp.zeros_like(h_scr)

    left = CONV_W // 2
    xp = xp_ref[...].astype(F32)[HALO - 8:HALO]
    xn = xn_ref[...].astype(F32)[0:8]
    xp = jnp.where(first, jnp.zeros_like(xp), xp)
    xn = jnp.where(last, jnp.zeros_like(xn), xn)
    lam = lam_ref[...]
    neg = -lam
    softplus = jnp.maximum(neg, 0.0) + jnp.log1p(jnp.exp(-jnp.abs(neg)))
    coef = (-LRU_C * math.log2(math.e)) * softplus
    row = lax.broadcasted_iota(jnp.int32, (ng, bw), 0)
    slab = lambda n, j: (n, pl.ds(j, ng, stride=8), slice(None))
    order = range(7, -1, -1) if reverse else range(8)

    for n in range(LRU_BLOCKS):
        cs = slice(n * bw, (n + 1) * bw)
        xf_scr[n] = x_ref[:, cs].astype(F32)
        xs = {j: xf_scr[slab(n, j)] for j in range(8)}
        for j in range(-left, 0):
            xs[j] = jnp.where(row == 0, xp[8 + j:9 + j, cs], pltpu.roll(xs[8 + j], 1, 0))
        for j in range(8, 8 + CONV_W - 1 - left):
            xs[j] = jnp.where(row == ng - 1, xn[j - 8:j - 7, cs], pltpu.roll(xs[j - 8], ng - 1, 0))
        us = []
        for j in range(8):
            u = cb_ref[:, cs] + cw_ref[0:1, cs] * xs[j - left]
            for k in range(1, CONV_W):
                u = u + cw_ref[k:k + 1, cs] * xs[j - left + k]
            us.append(u)
        un = jnp.concatenate(us, axis=0)
        ub = un.astype(BF)
        r = _sigmoid(_dot(ub, wa_ref[n]) + ba_ref[:, cs])
        gi = _sigmoid(_dot(ub, wx_ref[n]) + bx_ref[:, cs])
        a = jnp.exp2(r * coef[:, cs])
        om = 1.0 - a * a
        b = (om * lax.rsqrt(jnp.maximum(om, 1e-30))) * (gi * un)
        hl = ca = None
        for j in order:
            aj = a[j * ng:(j + 1) * ng]
            bj = b[j * ng:(j + 1) * ng]
            hl, ca = (bj, aj) if hl is None else (aj * hl + bj, aj * ca)
            hl_scr[n, j * ng:(j + 1) * ng, :] = hl
            ca_scr[n, j * ng:(j + 1) * ng, :] = ca
        ga_scr[pl.ds(n, ng, stride=LRU_BLOCKS), :] = ca
        gb_scr[pl.ds(n, ng, stride=LRU_BLOCKS), :] = hl

    h = h_scr[...]
    for g in (range(ng - 1, -1, -1) if reverse else range(ng)):
        gs = slice(g * LRU_BLOCKS, (g + 1) * LRU_BLOCKS)
        hp_scr[gs, :] = h
        h = ga_scr[gs, :] * h + gb_scr[gs, :]
    h_scr[...] = h

    for n in range(LRU_BLOCKS):
        hp = hp_scr[pl.ds(n, ng, stride=LRU_BLOCKS), :]
        for j in range(8):
            xf_scr[slab(n, j)] = hl_scr[n, j * ng:(j + 1) * ng, :] + ca_scr[n, j * ng:(j + 1) * ng, :] * hp
        o_ref[:, n * bw:(n + 1) * bw] = xf_scr[n].astype(o_ref.dtype)


def _lru_call(z, conv_w, conv_b, wa, ba, wx, bx, lam, bsz, seq, reverse):
    m = z.shape[0]
    ts = SEQ_TILE
    nt = seq // ts
    nlat = bsz * nt
    width = conv_b.shape[1]

    def row_blk(b, s):
        t = (nt - s) if reverse else (s - 1)
        return jnp.where(s == 0, nlat + b, b * nt + t)

    prev, nxt = _halo_specs(ts, width, 1, row_blk, m)
    vec = lambda shape: pl.BlockSpec(shape, lambda b, s: (0,) * len(shape))
    kern = functools.partial(_lru_kernel, reverse=reverse, nt=nt)
    return pl.pallas_call(
        kern,
        grid=(bsz, nt + 1),
        in_specs=[pl.BlockSpec((ts, width), lambda b, s: (row_blk(b, s), 1)), prev, nxt,
                  vec(conv_w.shape), vec(conv_b.shape), vec(wa.shape), vec(ba.shape),
                  vec(wx.shape), vec(bx.shape), vec(lam.shape)],
        out_specs=pl.BlockSpec((ts, width), lambda b, s: (row_blk(b, s), 0)),
        out_shape=jax.ShapeDtypeStruct((m, width), BF),
        scratch_shapes=[pltpu.VMEM((LRU_BLOCKS, ts, width // LRU_BLOCKS), F32)] * 3
        + [pltpu.VMEM((ts // 8 * LRU_BLOCKS, width // LRU_BLOCKS), F32)] * 3
        + [pltpu.VMEM((LRU_BLOCKS, width // LRU_BLOCKS), F32)],
        compiler_params=_cparams("parallel", "arbitrary"),
        name="lru_bwd" if reverse else "lru_fwd",
    )(z, z, z, conv_w, conv_b, wa, ba, wx, bx, lam)


def _merge_kernel(*refs, route, n_lat_blocks):
    refs = list(refs)
    x_ref = refs.pop(0)
    xc_ref = refs.pop(0) if n_lat_blocks else None
    mod_ref, mp_ref, hf_ref, hb_ref, lg_ref, o_ref = (refs.pop(0) for _ in range(6))
    oc_ref = refs.pop(0) if n_lat_blocks else None
    gt_ref, wp_ref, wl_ref, wm_ref, wo_ref, g1_ref, g2_ref = (refs.pop(0) for _ in range(7))
    if route:
        rw_ref, x1_ref, h2_ref, rt_ref, rtt_ref = refs
    else:
        x1_ref, h2_ref = refs
    is_ctx = pl.program_id(0) >= n_lat_blocks if n_lat_blocks else None
    tm, d = x_ref.shape
    halves = [slice(0, tm // 2), slice(tm // 2, tm)]

    def branches(rs):
        y_pool = _dot(mp_ref[rs, :], wp_ref[...])
        lru_in = (hf_ref[rs, :].astype(F32) + hb_ref[rs, :].astype(F32)) * _gelu_tanh(lg_ref[rs, :].astype(F32))
        y_lru = _dot(lru_in.astype(BF), wl_ref[...])
        y_mla = _dot(_pick(is_ctx, o_ref, oc_ref, rs), wm_ref[...])
        return y_pool, y_lru, y_mla

    def mixed(rs, ys):
        mix = (_sigmoid(gt_ref[rs, 0:d].astype(F32)) * ys[0]
               + _sigmoid(gt_ref[rs, d:2 * d].astype(F32)) * ys[1]
               + _sigmoid(gt_ref[rs, 2 * d:3 * d].astype(F32)) * ys[2])
        return _dot(mix.astype(BF), wo_ref[...])

    ys = [branches(rs) for rs in halves]
    outs = [mixed(rs, y) for rs, y in zip(halves, ys)]
    for rs, y in zip(halves, outs):
        x1 = _pick(is_ctx, x_ref, xc_ref, rs) + mod_ref[2:3, :] * _rms(y, g1_ref[...])
        x1_ref[rs, :] = x1
        h2 = _rms(x1, g2_ref[...]) * (1.0 + mod_ref[4:5, :]) + mod_ref[3:4, :]
        if route:
            h2_ref[rs, :] = _pack_rows(h2)
            _route_rows(h2, rw_ref, rt_ref, rtt_ref, rs)
        else:
            h2_ref[rs, :] = h2.astype(h2_ref.dtype)


def _route_rows(h2, rw_ref, rt_ref, rtt_ref, rs):
    logit = [jnp.sum(h2 * rw_ref[e:e + 1, :], axis=1, keepdims=True) for e in range(N_EXPERTS)]
    v1, i1 = logit[0], jnp.zeros_like(logit[0])
    for e in range(1, N_EXPERTS):
        upd = logit[e] > v1
        v1 = jnp.where(upd, logit[e], v1)
        i1 = jnp.where(upd, float(e), i1)
    v2, i2 = jnp.full_like(v1, -jnp.inf), jnp.zeros_like(v1)
    for e in range(N_EXPERTS):
        cand = jnp.where(i1 == float(e), -jnp.inf, logit[e])
        upd = cand > v2
        v2 = jnp.where(upd, cand, v2)
        i2 = jnp.where(upd, float(e), i2)
    ex = jnp.exp(v2 - v1)
    gate1 = 1.0 / (1.0 + ex)
    gate2 = ex / (1.0 + ex)
    col = lax.broadcasted_iota(jnp.int32, (h2.shape[0], rt_ref.shape[1]), 1)
    table = jnp.where(col == 0, i1, jnp.where(col == 1, i2,
                      jnp.where(col == 2, gate1, jnp.where(col == 3, gate2, 0.0))))
    rt_ref[rs, :] = table
    rtt_ref[:, rs] = table.T[0:rtt_ref.shape[0], :]


def _merge_call(x, x_ctx, mod, mp, hf, hb, z, o, o_ctx, wp, wl, wm, wo, g1, g2, rw, tm, n_rows, mod_idx):
    d = x.shape[1]
    route = rw is not None
    pair = x_ctx is not None
    row = lambda c: pl.BlockSpec((tm, d), lambda i: (i, c))
    full = lambda a: pl.BlockSpec(a.shape, lambda i: (0,) * a.ndim)
    x_specs = list(_row_pair_specs(tm, d, x.shape[0])) if pair else [row(0)]
    o_specs = list(_row_pair_specs(tm, d, o.shape[0])) if pair else [row(0)]
    in_specs = (x_specs + [pl.BlockSpec((None, 6, d), lambda i: (mod_idx(i), 0, 0)), row(0), row(0), row(0), row(2)]
                + o_specs + [pl.BlockSpec((tm, 3 * d), lambda i: (i, 1)),
                             full(wp), full(wl), full(wm), full(wo), full(g1), full(g2)])
    args = ([x] + ([x_ctx] if pair else []) + [mod, mp, hf, hb, z, o] + ([o_ctx] if pair else [])
            + [z, wp, wl, wm, wo, g1, g2])
    out_specs = [row(0), row(0)]
    out_shape = [jax.ShapeDtypeStruct((n_rows, d), F32), jax.ShapeDtypeStruct((n_rows, d), BF)]
    if route:
        out_specs[1] = pl.BlockSpec((tm, d // 2), lambda i: (i, 0))
        out_shape[1] = jax.ShapeDtypeStruct((n_rows, d // 2), jnp.uint32)
        in_specs.append(full(rw))
        args.append(rw)
        out_specs.append(pl.BlockSpec((tm, LANE), lambda i: (i, 0)))
        out_shape.append(jax.ShapeDtypeStruct((n_rows, LANE), F32))
        out_specs.append(pl.BlockSpec((8, tm), lambda i: (0, i)))
        out_shape.append(jax.ShapeDtypeStruct((8, n_rows), F32))
    return pl.pallas_call(
        functools.partial(_merge_kernel, route=route, n_lat_blocks=x.shape[0] // tm if pair else 0),
        grid=(n_rows // tm,),
        in_specs=in_specs, out_specs=out_specs, out_shape=out_shape,
        compiler_params=_cparams("parallel"),
        name="merge",
    )(*args)


def _swiglu_step(x, w1, w3, w2):
    a = _dot(x, w1)
    b = _dot(x, w3)
    return _dot((_silu(a) * b).astype(BF), w2)


def _ffn_kernel(h_ref, x_ref, mod_ref, g_ref, w13_ref, w2_ref, o_ref):
    f = pl.program_id(1)
    tf = w2_ref.shape[0]

    @pl.when(f == 0)
    def _():
        o_ref[...] = jnp.zeros_like(o_ref)

    ab = _dot(h_ref[...], w13_ref[...])
    o_ref[...] += _dot((_silu(ab[:, :tf]) * ab[:, tf:]).astype(BF), w2_ref[...])

    @pl.when(f == pl.num_programs(1) - 1)
    def _():
        o_ref[...] = x_ref[...] + mod_ref[5:6, :] * _rms(o_ref[...], g_ref[...])


def _ffn_call(h2, x1, mod, g, w1, w3, w2, tm, mod_idx):
    m, d = x1.shape
    tf = FFN_TF
    pad = -w1.shape[1] % tf
    w1, w3 = (jnp.pad(w, ((0, 0), (0, pad))) for w in (w1, w3))
    ff = w1.shape[1]
    w13 = jnp.concatenate([w[:, f * tf:(f + 1) * tf].astype(BF) for f in range(ff // tf) for w in (w1, w3)], axis=1)
    w2 = jnp.pad(w2, ((0, pad), (0, 0))).astype(BF)
    return pl.pallas_call(
        _ffn_kernel,
        grid=(m // tm, ff // tf),
        in_specs=[pl.BlockSpec((tm, d), lambda i, f: (i, 0)),
                  pl.BlockSpec((tm, d), lambda i, f: (i, 0)),
                  pl.BlockSpec((None, 6, d), lambda i, f: (mod_idx(i), 0, 0)),
                  pl.BlockSpec((1, d), lambda i, f: (0, 0)),
                  pl.BlockSpec((d, 2 * tf), lambda i, f: (0, f)),
                  pl.BlockSpec((tf, d), lambda i, f: (f, 0))],
        out_specs=pl.BlockSpec((tm, d), lambda i, f: (i, 0)),
        out_shape=jax.ShapeDtypeStruct((m, d), F32),
        compiler_params=_cparams("parallel", "arbitrary"),
        name="ffn",
    )(h2, x1, mod, g, w13, w2)


def _moe_kernel(blk_e_ref, nused_ref, x_ref, w1_ref, w3_ref, w2_ref, o_ref, xb, acc):
    i = pl.program_id(0)
    f = pl.program_id(1)
    used = i < nused_ref[0]
    w = x_ref.shape[1]

    @pl.when(f == 0)
    def _():
        acc[...] = jnp.zeros_like(acc)
        lo, hi = _unpack_rows(x_ref[...])
        xb[:, :w] = lo.astype(BF)
        xb[:, w:] = hi.astype(BF)

    @pl.when(used)
    def _():
        acc[...] += _swiglu_step(xb[...], w1_ref[...].astype(BF), w3_ref[...].astype(BF), w2_ref[...].astype(BF))

    @pl.when(f == pl.num_programs(1) - 1)
    def _():
        o_ref[...] = _pack_rows(acc[...])


def _moe_call(blk_e, nused, xs, w1, w3, w2, tme):
    n_rows, dp = xs.shape
    d = w1.shape[1]
    ff = w1.shape[2]
    tf = 512
    nf = ff // tf

    def ftile(i, f, nu):
        return jnp.where(i < nu[0], f, nf - 1)

    return pl.pallas_call(
        _moe_kernel,
        grid_spec=pltpu.PrefetchScalarGridSpec(
            num_scalar_prefetch=2,
            grid=(n_rows // tme, nf),
            in_specs=[pl.BlockSpec((tme, dp), lambda i, f, be, nu: (i, 0)),
                      pl.BlockSpec((None, d, tf), lambda i, f, be, nu: (be[i], 0, ftile(i, f, nu))),
                      pl.BlockSpec((None, d, tf), lambda i, f, be, nu: (be[i], 0, ftile(i, f, nu))),
                      pl.BlockSpec((None, tf, d), lambda i, f, be, nu: (be[i], ftile(i, f, nu), 0))],
            out_specs=pl.BlockSpec((tme, dp), lambda i, f, be, nu: (i, 0)),
            scratch_shapes=[pltpu.VMEM((tme, d), BF), pltpu.VMEM((tme, d), F32)]),
        out_shape=jax.ShapeDtypeStruct((n_rows, dp), jnp.uint32),
        compiler_params=_cparams("arbitrary", "arbitrary"),
        name="moe",
    )(blk_e, nused, xs, w1, w3, w2)


SC_CORES = 2
SC_SUBCORES = 16
SC_GATHER_ROWS = 128


def _sc_gather_rows(table, idx):
    n_idx = idx.shape[0]
    d = table.shape[1]
    workers = SC_CORES * SC_SUBCORES
    per_w = n_idx // workers
    rows = min(SC_GATHER_ROWS, per_w)
    assert n_idx % (8 * workers) == 0 and per_w % rows == 0
    mesh = plsc.VectorSubcoreMesh(core_axis_name="c", subcore_axis_name="s",
                                  num_cores=SC_CORES, num_subcores=SC_SUBCORES)

    def body(table_hbm, idx_hbm, out_hbm, idx_v, rows_v, sem):
        wid = lax.axis_index("s") * SC_CORES + lax.axis_index("c")
        base = wid * per_w

        @pl.loop(0, per_w // rows)
        def _(it):
            off = pl.multiple_of(base + it * rows, 8)
            pltpu.sync_copy(idx_hbm.at[pl.ds(off, rows)], idx_v)
            pltpu.async_copy(table_hbm.at[idx_v], rows_v, sem).wait()
            pltpu.sync_copy(rows_v, out_hbm.at[pl.ds(off, rows)])

    return pl.kernel(
        body,
        out_type=jax.ShapeDtypeStruct((n_idx, d), table.dtype),
        mesh=mesh,
        scratch_types=[pltpu.VMEM((rows,), jnp.int32), pltpu.VMEM((rows, d), table.dtype),
                       pltpu.SemaphoreType.DMA],
        name="sc_gather",
    )(table, idx)


def _combine_kernel(ya_ref, yb_ref, rt_ref, x_ref, mod_ref, g_ref, o_ref):
    g1 = rt_ref[:, 2:3]
    g2 = rt_ref[:, 3:4]
    a_lo, a_hi = _unpack_rows(ya_ref[...])
    b_lo, b_hi = _unpack_rows(yb_ref[...])
    f = jnp.concatenate([g1 * a_lo + g2 * b_lo, g1 * a_hi + g2 * b_hi], axis=1)
    o_ref[...] = x_ref[...] + mod_ref[5:6, :] * _rms(f, g_ref[...])


def _combine_call(ya, yb, rt, x1, mod, g, tm, mod_idx):
    n, d = x1.shape
    return pl.pallas_call(
        _combine_kernel,
        grid=(n // tm,),
        in_specs=[pl.BlockSpec((tm, d // 2), lambda i: (i, 0)),
                  pl.BlockSpec((tm, d // 2), lambda i: (i, 0)),
                  pl.BlockSpec((tm, LANE), lambda i: (i, 0)),
                  pl.BlockSpec((tm, d), lambda i: (i, 0)),
                  pl.BlockSpec((None, 6, d), lambda i: (mod_idx(i), 0, 0)),
                  pl.BlockSpec((1, d), lambda i: (0, 0))],
        out_specs=pl.BlockSpec((tm, d), lambda i: (i, 0)),
        out_shape=jax.ShapeDtypeStruct((n, d), F32),
        compiler_params=_cparams("parallel"),
        name="combine",
    )(ya, yb, rt, x1, mod, g)


def _route_plan(rt, tme):
    n = rt.shape[1]
    e_flat = rt[:TOP_K].reshape(-1).astype(jnp.int32)
    onehot = (e_flat[:, None] == jnp.arange(N_EXPERTS, dtype=jnp.int32)[None, :]).astype(jnp.int32)
    csum = jnp.cumsum(onehot, axis=0)
    counts = csum[-1]
    padded = (counts + tme - 1) // tme * tme
    pad_end = jnp.cumsum(padded)
    pad_start = pad_end - padded
    dest = jnp.sum(onehot * (csum + pad_start[None, :]), axis=1) - 1
    n_rows = n * TOP_K + N_EXPERTS * tme
    n_blk = n_rows // tme
    src_tok = (jnp.arange(n_rows, dtype=jnp.int32) % n).at[dest].set(
        jnp.arange(n * TOP_K, dtype=jnp.int32) % n, unique_indices=True)
    blk_start = jnp.arange(n_blk, dtype=jnp.int32) * tme
    blk_e = jnp.minimum(jnp.sum((pad_end[None, :] <= blk_start[:, None]).astype(jnp.int32), axis=1), N_EXPERTS - 1)
    nused = (pad_end[-1] // tme).astype(jnp.int32).reshape(1)
    return dest, src_tok, blk_e, nused


def _swap_perm():
    j = np.arange(QK_ROPE)
    axis, half, f = j // (2 * ROPE_FREQS), (j % (2 * ROPE_FREQS)) // ROPE_FREQS, j % ROPE_FREQS
    return axis * 2 * ROPE_FREQS + (1 - half) * ROPE_FREQS + f


def _rope_tables(seq, ctx_len):
    rows = seq // GRID_W
    row = jnp.repeat(jnp.arange(rows, dtype=F32), GRID_W)
    col = (jnp.arange(rows * GRID_W) % GRID_W).astype(F32)
    inv = ROPE_THETA ** (-jnp.arange(ROPE_FREQS, dtype=F32) / ROPE_FREQS)
    ang = jnp.stack([row[:, None] * inv, col[:, None] * inv], axis=1)
    cos, sin = jnp.cos(ang), jnp.sin(ang)
    cos64 = jnp.stack([cos, cos], axis=2).reshape(seq, QK_ROPE)
    sin64 = jnp.stack([-sin, sin], axis=2).reshape(seq, QK_ROPE)
    pad = jnp.zeros((seq, LANE - QK_ROPE), F32)
    cos_t = jnp.concatenate([cos64, pad], axis=1)
    sin_t = jnp.concatenate([sin64, pad], axis=1)
    ident = jnp.concatenate([jnp.ones((ctx_len, QK_ROPE), F32), jnp.zeros((ctx_len, LANE - QK_ROPE), F32)], axis=1)
    return jnp.concatenate([cos_t, ident], axis=0), jnp.concatenate([sin_t, jnp.zeros((ctx_len, LANE), F32)], axis=0)


def _prep_w_in(w):
    d = w.shape[0]
    pw = lw = d
    o = np.cumsum([0, pw, lw, lw, Q_LORA, KV_LORA, QK_ROPE, 3 * d])
    pool, lx, lg, cq, ckv, kr, gt = (w[:, o[i]:o[i + 1]] for i in range(7))
    cols = [pool, lx, lg, gt, cq, ckv, kr, kr[:, _swap_perm()]]
    n = sum(c.shape[1] for c in cols)
    cols.append(jnp.zeros((d, -n % INPROJ_TN), w.dtype))
    return jnp.concatenate(cols, axis=1).astype(BF)


def _prep_w_uq(w):
    qk = QK_NOPE + QK_ROPE
    w = w.reshape(w.shape[0], HEADS, qk)
    rope = w[:, :, QK_NOPE:]
    return jnp.concatenate([w, rope[:, :, _swap_perm()]], axis=2).reshape(w.shape[0], HEADS * HEAD_W).T.astype(BF)


def kernel(x, c, ctx, c_ctx, mod_w, mod_b, pre_mix_g, post_mix_g, pre_ffn_g, post_ffn_g, w_in, pool_w, pool_scale,
           pool_proj, conv_w, conv_b, gate_a_w, gate_a_b, gate_x_w, gate_x_b, lru_lambda, lru_proj, q_norm_g, w_uq,
           kv_norm_g, w_ukv, mla_proj, w_out, ffn_w1, ffn_w3, ffn_w2, router_w, moe_w1, moe_w3, moe_w2):
    bsz, seq, d = x.shape
    ctx_len = ctx.shape[1]
    depth = mod_w.shape[0]
    assert ctx_len == SEQ_TILE and seq % SEQ_TILE == 0 and seq % GRID_W == 0
    n_lat = bsz * seq
    n_ctx = bsz * ctx_len
    n_all = n_lat + n_ctx
    tm = min(1024, seq, n_ctx)
    assert seq % tm == 0 and n_ctx % tm == 0
    tm_merge = tm // 2

    def mod_idx_for(rows):
        return lambda i: jnp.where(i < n_lat // rows, i // (seq // rows), bsz)

    mod_idx = mod_idx_for(tm)

    cc = jnp.concatenate([c, c_ctx[None, :], jnp.zeros((8 - bsz - 1, d), F32)], axis=0)
    mods = _mod_call(cc, mod_w, mod_b)[:, :bsz + 1].reshape(depth, bsz + 1, 6, d)
    cos_t, sin_t = _rope_tables(seq, ctx_len)
    x_all, x_ctx = x.reshape(n_lat, d), ctx.reshape(n_ctx, d)
    row1 = lambda v: v.reshape(1, -1)

    for l in range(depth):
        last = l == depth - 1
        n_out = n_lat if last else n_all
        mod = mods[l]
        z = _inproj_call(x_all, x_ctx, mod, row1(pre_mix_g[l]), _prep_w_in(w_in[l]), tm, mod_idx)
        wkv = w_ukv[l].reshape(KV_LORA, HEADS, QK_NOPE + V_DIM)
        wk = wkv[:, :, :QK_NOPE].reshape(KV_LORA, HEADS * QK_NOPE).astype(BF)
        wvt = wkv[:, :, QK_NOPE:].reshape(KV_LORA, HEADS * V_DIM).T.astype(BF)
        q, k, v = _qkv_call(z, cos_t, sin_t, row1(q_norm_g[l]), row1(kv_norm_g[l]), _prep_w_uq(w_uq[l]),
                            wk, wvt, bsz, seq, ctx_len, 6 * d)
        tq = min(512, seq)
        o = _attn_call(q, k, v, bsz=bsz, row_blk0=0, nq=seq // tq, tq=tq, kblk=0, klen=seq + ctx_len)
        o_ctx = None
        if not last:
            o_ctx = _attn_call(q, k, v, bsz=bsz, row_blk0=n_lat // ctx_len, nq=1, tq=ctx_len,
                               kblk=seq // ctx_len, klen=ctx_len)
            if x_ctx is None:
                o, o_ctx = jnp.concatenate([o, o_ctx], axis=0), None
        mp = _pool_call(z, pool_w[l].astype(BF), row1(pool_scale[l]), bsz, seq, ctx_len, n_out)
        hs = []
        for dr in range(2):
            hs.append(_lru_call(z, conv_w[l], row1(conv_b[l]), gate_a_w[l, dr].astype(BF), row1(gate_a_b[l, dr]),
                                gate_x_w[l, dr].astype(BF), row1(gate_x_b[l, dr]), row1(lru_lambda[l, dr]),
                                bsz, seq, dr == 1))
        moe_layer = l % 2 == 1
        rw = None
        if moe_layer:
            rw = router_w[l // 2].T
        outs = _merge_call(x_all, x_ctx if o_ctx is not None else None, mod, mp, hs[0], hs[1], z, o, o_ctx,
                           pool_proj[l].astype(BF), lru_proj[l].astype(BF), mla_proj[l].astype(BF),
                           w_out[l].astype(BF), row1(post_mix_g[l]), row1(pre_ffn_g[l]),
                           rw, tm_merge, n_out, mod_idx_for(tm_merge))
        x_ctx = None
        if not moe_layer:
            x1, h2 = outs
            x_all = _ffn_call(h2, x1, mod, row1(post_ffn_g[l]), ffn_w1[l // 2], ffn_w3[l // 2], ffn_w2[l // 2],
                              tm, mod_idx)
        else:
            x1, h2, rt, rtt = outs
            tme = min(1024, n_out * TOP_K // N_EXPERTS)
            dest, src_tok, blk_e, nused = _route_plan(rtt, tme)
            xs = _sc_gather_rows(h2, src_tok)
            y = _moe_call(blk_e, nused, xs, moe_w1[l // 2], moe_w3[l // 2], moe_w2[l // 2], tme)
            ya = _sc_gather_rows(y, dest[:n_out])
            yb = _sc_gather_rows(y, dest[n_out:])
            x_all = _combine_call(ya, yb, rt, x1, mod, row1(post_ffn_g[l]), tm, mod_idx)
    return x_all[:n_lat].reshape(bsz, seq, d)
```

```python
import functools
import math

import numpy as np
import jax
import jax.numpy as jnp
from jax import lax
from jax.experimental import pallas as pl
from jax.experimental.pallas import tpu as pltpu
from jax.experimental.pallas import tpu_sc as plsc

BF = jnp.bfloat16
F32 = jnp.float32

RMS_EPS = 1e-6
GRID_W = 64
POOL_WINDOWS = (2, 4, 8, 16)
LRU_BLOCKS = 8
CONV_W = 4
LRU_C = 8.0
HEADS = 8
Q_LORA = 384
KV_LORA = 256
QK_NOPE = 128
QK_ROPE = 64
V_DIM = 128
MLA_SCALE = (QK_NOPE + QK_ROPE) ** -0.5
Q_SCALE = MLA_SCALE * math.log2(math.e)
ROPE_FREQS = QK_ROPE // 4
ROPE_THETA = 10000.0
N_EXPERTS = 8
TOP_K = 2

LANE = 128
SUBLANE = 8
HALO = 2 * SUBLANE
SEQ_TILE = 256
HEAD_W = 256
VT_ROWS = V_DIM + HALO
INPROJ_TN = 1024
FFN_TF = 768
V7X_VMEM_BYTES = 64 * 1024 * 1024
VMEM_LIMIT = V7X_VMEM_BYTES * 3 // 4


def _cparams(*sem):
    return pltpu.CompilerParams(dimension_semantics=sem, vmem_limit_bytes=VMEM_LIMIT)


def _rms(x, g):
    ms = jnp.mean(x * x, axis=-1, keepdims=True)
    return x * lax.rsqrt(ms + RMS_EPS) * g


def _sigmoid(x):
    return 0.5 * jnp.tanh(0.5 * x) + 0.5


def _silu(x):
    return x * _sigmoid(x)


def _gelu_tanh(x):
    return 0.5 * x * (1.0 + jnp.tanh(math.sqrt(2.0 / math.pi) * (x + 0.044715 * (x * x * x))))


def _dot(a, b):
    return jnp.dot(a, b, preferred_element_type=F32)


def _pack_rows(x):
    w = x.shape[1] // 2
    lo = lax.bitcast_convert_type(x[:, :w].astype(BF).astype(F32), jnp.uint32)
    hi = lax.bitcast_convert_type(x[:, w:].astype(BF).astype(F32), jnp.uint32)
    return (lo >> 16) | hi


def _unpack_rows(u):
    lo = lax.bitcast_convert_type(u << 16, F32)
    hi = lax.bitcast_convert_type(u & jnp.uint32(0xFFFF0000), F32)
    return lo, hi


def _mod_kernel(c_ref, w_ref, b_ref, o_ref):
    s = _silu(c_ref[...])
    o_ref[...] = _dot(s.astype(BF), w_ref[...].astype(BF)) + b_ref[...]


def _mod_call(cc, mod_w, mod_b):
    depth, d, n6 = mod_w.shape
    tn = 1536
    return pl.pallas_call(
        _mod_kernel,
        grid=(depth, n6 // tn),
        in_specs=[pl.BlockSpec((SUBLANE, d), lambda l, j: (0, 0)),
                  pl.BlockSpec((None, d, tn), lambda l, j: (l, 0, j)),
                  pl.BlockSpec((None, 1, tn), lambda l, j: (l, 0, j))],
        out_specs=pl.BlockSpec((None, SUBLANE, tn), lambda l, j: (l, 0, j)),
        out_shape=jax.ShapeDtypeStruct((depth, SUBLANE, n6), F32),
        compiler_params=_cparams("parallel", "arbitrary"),
        name="mod",
    )(cc, mod_w, mod_b.reshape(depth, 1, n6))


def _row_pair_specs(tm, width, n_lat_rows):
    nb = n_lat_rows // tm
    lat = pl.BlockSpec((tm, width), lambda i, *_: (jnp.minimum(i, nb - 1), 0))
    ctx = pl.BlockSpec((tm, width), lambda i, *_: (jnp.maximum(i - nb, 0), 0))
    return lat, ctx


def _pick(is_ctx, lat_ref, ctx_ref, rs=slice(None)):
    if ctx_ref is None:
        return lat_ref[rs, :]
    return jnp.where(is_ctx, ctx_ref[rs, :], lat_ref[rs, :])


def _inproj_kernel(x_ref, *rest, n_lat_blocks):
    xc_ref = rest[0] if len(rest) == 6 else None
    mod_ref, g_ref, w_ref, z_ref, h_scr = rest[-5:]

    @pl.when(pl.program_id(1) == 0)
    def _():
        h = _rms(_pick(pl.program_id(0) >= n_lat_blocks, x_ref, xc_ref), g_ref[...])
        h = h * (1.0 + mod_ref[1:2, :]) + mod_ref[0:1, :]
        h_scr[...] = h.astype(BF)

    z_ref[...] = _dot(h_scr[...], w_ref[...]).astype(BF)


def _inproj_call(x, x_ctx, mod, g, w, tm, mod_idx):
    d = x.shape[1]
    m = x.shape[0] + (0 if x_ctx is None else x_ctx.shape[0])
    n = w.shape[1]
    tn = INPROJ_TN
    if x_ctx is None:
        x_specs, xs = [pl.BlockSpec((tm, d), lambda i, j: (i, 0))], [x]
    else:
        x_specs, xs = list(_row_pair_specs(tm, d, x.shape[0])), [x, x_ctx]
    return pl.pallas_call(
        functools.partial(_inproj_kernel, n_lat_blocks=x.shape[0] // tm),
        grid=(m // tm, n // tn),
        in_specs=x_specs + [pl.BlockSpec((None, 6, d), lambda i, j: (mod_idx(i), 0, 0)),
                            pl.BlockSpec((1, d), lambda i, j: (0, 0)),
                            pl.BlockSpec((d, tn), lambda i, j: (0, j))],
        out_specs=pl.BlockSpec((tm, tn), lambda i, j: (i, j)),
        out_shape=jax.ShapeDtypeStruct((m, n), BF),
        scratch_shapes=[pltpu.VMEM((tm, d), BF)],
        compiler_params=_cparams("parallel", "arbitrary"),
        name="inproj",
    )(*xs, mod, g, w)


def _rope(x, cos, sin):
    return x * cos + pltpu.roll(x, LANE // 2, 1) * sin


_NT = (((1,), (1,)), ((), ()))


def _qkv_kernel(z_ref, cos_ref, sin_ref, cost_ref, sint_ref, qg_ref, kvg_ref, wuqt_ref, wk_ref, wvt_ref,
                qt_ref, k_ref, vt_ref):
    z = z_ref[...]
    cq = z[:, :Q_LORA].astype(F32)
    ckv = z[:, Q_LORA:Q_LORA + KV_LORA].astype(F32)
    kr = z[:, Q_LORA + KV_LORA:].astype(F32)
    cqn = _rms(cq, qg_ref[...]).astype(BF)
    ckvn = _rms(ckv, kvg_ref[...]).astype(BF)
    qt = lax.dot_general(wuqt_ref[...], cqn, _NT, preferred_element_type=F32)
    kn = _dot(ckvn, wk_ref[...])
    vt = lax.dot_general(wvt_ref[...], ckvn, _NT, preferred_element_type=F32)
    krot = _rope(kr, cos_ref[...], sin_ref[...]).astype(BF)
    cost = cost_ref[...]
    sint = sint_ref[...]
    for h in range(HEADS):
        c0 = h * HEAD_W
        c1 = c0 + QK_NOPE
        c2 = c1 + QK_ROPE
        qt_ref[c0:c1, :] = (qt[c0:c1, :] * Q_SCALE).astype(BF)
        qt_ref[c1:c2, :] = ((qt[c1:c2, :] * cost + qt[c2:c0 + HEAD_W, :] * sint) * Q_SCALE).astype(BF)
        qt_ref[c2:c0 + HEAD_W, :] = jnp.zeros((QK_ROPE, qt.shape[1]), BF)
        k_ref[h, :, 0:QK_NOPE] = kn[:, h * QK_NOPE:(h + 1) * QK_NOPE].astype(BF)
        k_ref[h, :, QK_NOPE:HEAD_W] = krot
        vt_ref[h, 0:V_DIM, :] = vt[h * V_DIM:(h + 1) * V_DIM, :].astype(BF)
        vt_ref[h, V_DIM:VT_ROWS, :] = jnp.ones((VT_ROWS - V_DIM, vt.shape[1]), BF)


def _qkv_call(z, cos_t, sin_t, qg, kvg, wuqt, wk, wvt, bsz, seq, ctx_len, z_off):
    m = z.shape[0]
    cos_tt = cos_t[:, :QK_ROPE].T
    sin_tt = sin_t[:, :QK_ROPE].T
    ts = SEQ_TILE
    nt = seq // ts
    nlat = bsz * nt
    lk = seq + ctx_len
    zw = Q_LORA + KV_LORA + LANE
    assert z_off % zw == 0
    zcol = z_off // zw

    def tab_idx(i):
        return (jnp.where(i < nlat, i % nt, nt), 0)

    def kv_idx(i):
        return (jnp.where(i < nlat, i // nt, i - nlat), 0, jnp.where(i < nlat, i % nt, nt), 0)

    def vt_idx(i):
        return (jnp.where(i < nlat, i // nt, i - nlat), 0, 0, jnp.where(i < nlat, i % nt, nt))

    return pl.pallas_call(
        _qkv_kernel,
        grid=(m // ts,),
        in_specs=[pl.BlockSpec((ts, zw), lambda i: (i, zcol)),
                  pl.BlockSpec((ts, LANE), tab_idx),
                  pl.BlockSpec((ts, LANE), tab_idx),
                  pl.BlockSpec((QK_ROPE, ts), lambda i: tab_idx(i)[::-1]),
                  pl.BlockSpec((QK_ROPE, ts), lambda i: tab_idx(i)[::-1]),
                  pl.BlockSpec((1, Q_LORA), lambda i: (0, 0)),
                  pl.BlockSpec((1, KV_LORA), lambda i: (0, 0)),
                  pl.BlockSpec(wuqt.shape, lambda i: (0, 0)),
                  pl.BlockSpec(wk.shape, lambda i: (0, 0)),
                  pl.BlockSpec(wvt.shape, lambda i: (0, 0))],
        out_specs=[pl.BlockSpec((HEADS * HEAD_W, ts), lambda i: (0, i)),
                   pl.BlockSpec((None, HEADS, ts, HEAD_W), kv_idx),
                   pl.BlockSpec((None, HEADS, VT_ROWS, ts), vt_idx)],
        out_shape=[jax.ShapeDtypeStruct((HEADS * HEAD_W, m), BF),
                   jax.ShapeDtypeStruct((bsz, HEADS, lk, HEAD_W), BF),
                   jax.ShapeDtypeStruct((bsz, HEADS, VT_ROWS, lk), BF)],
        compiler_params=_cparams("parallel"),
        name="qkv_up",
    )(z, cos_t, sin_t, cos_tt, sin_tt, qg, kvg, wuqt, wk, wvt)


def _col_reduce(x, pair, red):
    n = x.shape[0] // 4
    a = pair(pair(x[0:n], x[n:2 * n]), pair(x[2 * n:3 * n], x[3 * n:4 * n]))
    return red(a, axis=0, keepdims=True)


ATTN_LAG_LIMIT = 20.0


def _attn_kernel(qt_ref, k_ref, vt_ref, o_ref, p_scr, *, chunks, kp):
    qt = qt_ref[...]
    tq = qt.shape[1]

    nk = len(chunks)

    def scores(c):
        k0, kn = chunks[c]
        return _dot(k_ref[k0:k0 + kn, :], qt)

    def values(c):
        k0, kn = chunks[c]
        return _dot(vt_ref[:, k0:k0 + kn], p_scr[c % 2, 0:kn, :])

    def finish(a):
        o_ref[...] = (a[0:V_DIM] / a[V_DIM:V_DIM + 1]).T.astype(o_ref.dtype)

    s = scores(0)
    ref = _col_reduce(s, jnp.maximum, jnp.max)
    p_scr[0, 0:chunks[0][1], :] = jnp.exp2((s - ref).astype(BF))
    acc = alpha = None
    lag = jnp.zeros((1, tq), F32)
    for c in range(1, nk):
        s = scores(c)
        p_scr[c % 2, 0:chunks[c][1], :] = jnp.exp2((s - ref).astype(BF))
        mc = _col_reduce(s, jnp.maximum, jnp.max)
        pv = values(c - 1)
        acc = pv if acc is None else acc + pv
        if alpha is not None:
            acc = acc * alpha
        lag = jnp.maximum(lag, mc - ref)
        new_ref = jnp.maximum(ref, mc)
        alpha = jnp.exp2(ref - new_ref)
        ref = new_ref
    pv = values(nk - 1)
    acc = pv if acc is None else acc + pv
    over = jnp.max(lag) > ATTN_LAG_LIMIT

    @pl.when(jnp.logical_not(over))
    def _():
        finish(acc)

    @pl.when(over)
    def _():
        def body(j, carry):
            m, a = carry
            r0 = pl.multiple_of(j * kp, kp)
            sj = _dot(k_ref[pl.ds(r0, kp), :], qt)
            m_new = jnp.maximum(m, jnp.max(sj, axis=0, keepdims=True))
            pj = jnp.exp2((sj - m_new).astype(BF))
            a = jnp.exp2(m - m_new) * a + _dot(vt_ref[:, pl.ds(r0, kp)], pj)
            return m_new, a

        init = (jnp.full((1, tq), -1e30, F32), jnp.zeros((VT_ROWS, tq), F32))
        finish(lax.fori_loop(0, k_ref.shape[0] // kp, body, init)[1])


ATTN_FIRST = 256
ATTN_CHUNK = 512


def _attn_call(q, k, v, *, bsz, row_blk0, nq, tq, kblk, klen):
    rest = klen - ATTN_FIRST
    step = ATTN_CHUNK if rest % ATTN_CHUNK == 0 else ATTN_FIRST
    assert rest % step == 0
    chunks = ((0, ATTN_FIRST),) + tuple((ATTN_FIRST + i * step, step) for i in range(rest // step))
    tk = max(n for _, n in chunks)
    kern = functools.partial(_attn_kernel, chunks=chunks, kp=ATTN_FIRST)
    return pl.pallas_call(
        kern,
        grid=(bsz, HEADS, nq),
        in_specs=[pl.BlockSpec((HEAD_W, tq), lambda b, h, i: (h, row_blk0 + b * nq + i)),
                  pl.BlockSpec((None, None, klen, HEAD_W), lambda b, h, i: (b, h, kblk, 0)),
                  pl.BlockSpec((None, None, VT_ROWS, klen), lambda b, h, i: (b, h, 0, kblk))],
        out_specs=pl.BlockSpec((tq, V_DIM), lambda b, h, i: (b * nq + i, h)),
        out_shape=jax.ShapeDtypeStruct((bsz * nq * tq, HEADS * V_DIM), BF),
        scratch_shapes=[pltpu.VMEM((2, tk, tq), BF)],
        compiler_params=_cparams("parallel", "parallel", "arbitrary"),
        name="attn",
    )(q, k, v)


def _seq_flags(i, nlat, nt):
    is_ctx = i >= nlat
    t = jnp.where(is_ctx, 0, i % nt)
    first = jnp.logical_or(is_ctx, t == 0)
    last = jnp.logical_or(is_ctx, t == nt - 1)
    return is_ctx, t, first, last


def _pool_kernel(x_ref, xp_ref, xn_ref, pw_ref, ps_ref, o_ref, *, nlat, nt, seq, ctx_len):
    ts = x_ref.shape[0]
    is_ctx, t, first, last = _seq_flags(pl.program_id(0), nlat, nt)
    seq_len = jnp.where(is_ctx, ctx_len, seq)
    x = x_ref[...]
    xp = jnp.where(first, jnp.zeros_like(xp_ref[...]), xp_ref[...])
    xn = jnp.where(last, jnp.zeros_like(xn_ref[...]), xn_ref[...])
    xe = jnp.concatenate([xp, x, xn], axis=0)
    tpos = t * ts + lax.broadcasted_iota(jnp.int32, (ts, 1), 0)
    rel = (lax.broadcasted_iota(jnp.int32, (ts, ts + 2 * HALO), 1) - HALO
           - lax.broadcasted_iota(jnp.int32, (ts, ts + 2 * HALO), 0))
    gw = x.shape[1] // len(POOL_WINDOWS)
    cols = [slice(g * gw, (g + 1) * gw) for g in range(len(POOL_WINDOWS))]
    sums = []
    for cs, w in zip(cols, POOL_WINDOWS):
        band = jnp.where(rel >= -(w // 2), jnp.where(rel < w - w // 2, 1.0, 0.0), 0.0).astype(BF)
        sums.append(_dot(band, xe[:, cs]))
    for g, (cs, w) in enumerate(zip(cols, POOL_WINDOWS)):
        cnt = (jnp.minimum(tpos + (w - w // 2), seq_len) - jnp.maximum(tpos - w // 2, 0)).astype(F32)
        mean_minus = sums[g] / cnt - x[:, cs].astype(F32)
        o_ref[:, cs] = (_dot(mean_minus.astype(BF), pw_ref[g]) * ps_ref[:, cs]).astype(BF)


def _halo_specs(ts, width, col, row_blk, m):
    r = ts // HALO
    nh = m // HALO
    prev = pl.BlockSpec((HALO, width), lambda *a: (jnp.maximum(row_blk(*a) * r - 1, 0), col))
    nxt = pl.BlockSpec((HALO, width), lambda *a: (jnp.minimum((row_blk(*a) + 1) * r, nh - 1), col))
    return prev, nxt


def _pool_call(z, pool_w, pool_scale, bsz, seq, ctx_len, n_rows):
    m = z.shape[0]
    ts = SEQ_TILE
    nt = seq // ts
    nlat = bsz * nt
    width = pool_scale.shape[1]
    prev, nxt = _halo_specs(ts, width, 0, lambda i: i, m)
    kern = functools.partial(_pool_kernel, nlat=nlat, nt=nt, seq=seq, ctx_len=ctx_len)
    return pl.pallas_call(
        kern,
        grid=(n_rows // ts,),
        in_specs=[pl.BlockSpec((ts, width), lambda i: (i, 0)), prev, nxt,
                  pl.BlockSpec(pool_w.shape, lambda i: (0, 0, 0)),
                  pl.BlockSpec((1, width), lambda i: (0, 0))],
        out_specs=pl.BlockSpec((ts, width), lambda i: (i, 0)),
        out_shape=jax.ShapeDtypeStruct((n_rows, width), BF),
        compiler_params=_cparams("parallel"),
        name="pool",
    )(z, z, z, pool_w, pool_scale)


def _lru_kernel(x_ref, xp_ref, xn_ref, cw_ref, cb_ref, wa_ref, ba_ref, wx_ref, bx_ref, lam_ref, o_ref,
                xf_scr, hl_scr, ca_scr, ga_scr, gb_scr, hp_scr, h_scr, *, reverse, nt):
    ts, width = x_ref.shape
    gs = SUBLANE
    ng = ts // gs
    bw = width // LRU_BLOCKS
    assert LRU_BLOCKS == gs
    s = pl.program_id(1)
    t = (nt - s) if reverse else (s - 1)
    first = jnp.logical_or(s == 0, t == 0)
    last = jnp.logical_or(s == 0, t == nt - 1)

    @pl.when(s == 0)
    def _():
        h_scr[...] = jnp.zeros_like(h_scr)

    left = CONV_W // 2
    xp = xp_ref[...].astype(F32)[HALO - gs:HALO]
    xn = xn_ref[...].astype(F32)[0:gs]
    xp = jnp.where(first, jnp.zeros_like(xp), xp)
    xn = jnp.where(last, jnp.zeros_like(xn), xn)
    lam = lam_ref[...]
    neg = -lam
    softplus = jnp.maximum(neg, 0.0) + jnp.log1p(jnp.exp(-jnp.abs(neg)))
    coef = (-LRU_C * math.log2(math.e)) * softplus
    row = lax.broadcasted_iota(jnp.int32, (ng, bw), 0)
    slab = lambda n, j: (n, pl.ds(j, ng, stride=gs), slice(None))
    order = range(gs - 1, -1, -1) if reverse else range(gs)

    for n in range(LRU_BLOCKS):
        cs = slice(n * bw, (n + 1) * bw)
        xf_scr[n] = x_ref[:, cs].astype(F32)
        xs = {j: xf_scr[slab(n, j)] for j in range(gs)}
        for j in range(-left, 0):
            xs[j] = jnp.where(row == 0, xp[gs + j:gs + j + 1, cs], pltpu.roll(xs[gs + j], 1, 0))
        for j in range(gs, gs + CONV_W - 1 - left):
            xs[j] = jnp.where(row == ng - 1, xn[j - gs:j - gs + 1, cs], pltpu.roll(xs[j - gs], ng - 1, 0))
        us = []
        for j in range(gs):
            u = cb_ref[:, cs] + cw_ref[0:1, cs] * xs[j - left]
            for k in range(1, CONV_W):
                u = u + cw_ref[k:k + 1, cs] * xs[j - left + k]
            us.append(u)
        un = jnp.concatenate(us, axis=0)
        ub = un.astype(BF)
        r = _sigmoid(_dot(ub, wa_ref[n]) + ba_ref[:, cs])
        gi = _sigmoid(_dot(ub, wx_ref[n]) + bx_ref[:, cs])
        a = jnp.exp2(r * coef[:, cs])
        om = 1.0 - a * a
        b = (om * lax.rsqrt(jnp.maximum(om, 1e-30))) * (gi * un)
        hl = ca = None
        for j in order:
            aj = a[j * ng:(j + 1) * ng]
            bj = b[j * ng:(j + 1) * ng]
            hl, ca = (bj, aj) if hl is None else (aj * hl + bj, aj * ca)
            hl_scr[n, j * ng:(j + 1) * ng, :] = hl
            ca_scr[n, j * ng:(j + 1) * ng, :] = ca
        ga_scr[pl.ds(n, ng, stride=LRU_BLOCKS), :] = ca
        gb_scr[pl.ds(n, ng, stride=LRU_BLOCKS), :] = hl

    h = h_scr[...]
    for g in (range(ng - 1, -1, -1) if reverse else range(ng)):
        rows = slice(g * LRU_BLOCKS, (g + 1) * LRU_BLOCKS)
        hp_scr[rows, :] = h
        h = ga_scr[rows, :] * h + gb_scr[rows, :]
    h_scr[...] = h

    for n in range(LRU_BLOCKS):
        hp = hp_scr[pl.ds(n, ng, stride=LRU_BLOCKS), :]
        for j in range(gs):
            xf_scr[slab(n, j)] = hl_scr[n, j * ng:(j + 1) * ng, :] + ca_scr[n, j * ng:(j + 1) * ng, :] * hp
        o_ref[:, n * bw:(n + 1) * bw] = xf_scr[n].astype(o_ref.dtype)


def _lru_call(z, conv_w, conv_b, wa, ba, wx, bx, lam, bsz, seq, reverse):
    m = z.shape[0]
    ts = SEQ_TILE
    nt = seq // ts
    nlat = bsz * nt
    width = conv_b.shape[1]

    def row_blk(b, s):
        t = (nt - s) if reverse else (s - 1)
        return jnp.where(s == 0, nlat + b, b * nt + t)

    prev, nxt = _halo_specs(ts, width, 1, row_blk, m)
    vec = lambda shape: pl.BlockSpec(shape, lambda b, s: (0,) * len(shape))
    kern = functools.partial(_lru_kernel, reverse=reverse, nt=nt)
    return pl.pallas_call(
        kern,
        grid=(bsz, nt + 1),
        in_specs=[pl.BlockSpec((ts, width), lambda b, s: (row_blk(b, s), 1)), prev, nxt,
                  vec(conv_w.shape), vec(conv_b.shape), vec(wa.shape), vec(ba.shape),
                  vec(wx.shape), vec(bx.shape), vec(lam.shape)],
        out_specs=pl.BlockSpec((ts, width), lambda b, s: (row_blk(b, s), 0)),
        out_shape=jax.ShapeDtypeStruct((m, width), BF),
        scratch_shapes=[pltpu.VMEM((LRU_BLOCKS, ts, width // LRU_BLOCKS), F32)] * 3
        + [pltpu.VMEM((ts // SUBLANE * LRU_BLOCKS, width // LRU_BLOCKS), F32)] * 3
        + [pltpu.VMEM((LRU_BLOCKS, width // LRU_BLOCKS), F32)],
        compiler_params=_cparams("parallel", "arbitrary"),
        name="lru_bwd" if reverse else "lru_fwd",
    )(z, z, z, conv_w, conv_b, wa, ba, wx, bx, lam)


def _merge_kernel(*refs, route, n_lat_blocks):
    refs = list(refs)
    x_ref = refs.pop(0)
    xc_ref = refs.pop(0) if n_lat_blocks else None
    mod_ref, mp_ref, hf_ref, hb_ref, lg_ref, o_ref = (refs.pop(0) for _ in range(6))
    oc_ref = refs.pop(0) if n_lat_blocks else None
    gt_ref, wp_ref, wl_ref, wm_ref, wo_ref, g1_ref, g2_ref = (refs.pop(0) for _ in range(7))
    if route:
        rw_ref, x1_ref, h2_ref, rt_ref, rtt_ref = refs
    else:
        x1_ref, h2_ref = refs
    is_ctx = pl.program_id(0) >= n_lat_blocks if n_lat_blocks else None
    tm, d = x_ref.shape
    halves = [slice(0, tm // 2), slice(tm // 2, tm)]

    def branches(rs):
        y_pool = _dot(mp_ref[rs, :], wp_ref[...])
        lru_in = (hf_ref[rs, :].astype(F32) + hb_ref[rs, :].astype(F32)) * _gelu_tanh(lg_ref[rs, :].astype(F32))
        y_lru = _dot(lru_in.astype(BF), wl_ref[...])
        y_mla = _dot(_pick(is_ctx, o_ref, oc_ref, rs), wm_ref[...])
        return y_pool, y_lru, y_mla

    def mixed(rs, ys):
        mix = (_sigmoid(gt_ref[rs, 0:d].astype(F32)) * ys[0]
               + _sigmoid(gt_ref[rs, d:2 * d].astype(F32)) * ys[1]
               + _sigmoid(gt_ref[rs, 2 * d:3 * d].astype(F32)) * ys[2])
        return _dot(mix.astype(BF), wo_ref[...])

    ys = [branches(rs) for rs in halves]
    outs = [mixed(rs, y) for rs, y in zip(halves, ys)]
    for rs, y in zip(halves, outs):
        x1 = _pick(is_ctx, x_ref, xc_ref, rs) + mod_ref[2:3, :] * _rms(y, g1_ref[...])
        x1_ref[rs, :] = x1
        h2 = _rms(x1, g2_ref[...]) * (1.0 + mod_ref[4:5, :]) + mod_ref[3:4, :]
        if route:
            h2_ref[rs, :] = _pack_rows(h2)
            _route_rows(h2, rw_ref, rt_ref, rtt_ref, rs)
        else:
            h2_ref[rs, :] = h2.astype(h2_ref.dtype)


def _route_rows(h2, rw_ref, rt_ref, rtt_ref, rs):
    logit = [jnp.sum(h2 * rw_ref[e:e + 1, :], axis=1, keepdims=True) for e in range(N_EXPERTS)]
    v1, i1 = logit[0], jnp.zeros_like(logit[0])
    for e in range(1, N_EXPERTS):
        upd = logit[e] > v1
        v1 = jnp.where(upd, logit[e], v1)
        i1 = jnp.where(upd, float(e), i1)
    v2, i2 = jnp.full_like(v1, -jnp.inf), jnp.zeros_like(v1)
    for e in range(N_EXPERTS):
        cand = jnp.where(i1 == float(e), -jnp.inf, logit[e])
        upd = cand > v2
        v2 = jnp.where(upd, cand, v2)
        i2 = jnp.where(upd, float(e), i2)
    ex = jnp.exp(v2 - v1)
    gate1 = 1.0 / (1.0 + ex)
    gate2 = ex / (1.0 + ex)
    col = lax.broadcasted_iota(jnp.int32, (h2.shape[0], rt_ref.shape[1]), 1)
    table = jnp.where(col == 0, i1, jnp.where(col == 1, i2,
                      jnp.where(col == 2, gate1, jnp.where(col == 3, gate2, 0.0))))
    rt_ref[rs, :] = table
    rtt_ref[:, rs] = table.T[0:rtt_ref.shape[0], :]


def _merge_call(x, x_ctx, mod, mp, hf, hb, z, o, o_ctx, wp, wl, wm, wo, g1, g2, rw, tm, n_rows, mod_idx):
    d = x.shape[1]
    route = rw is not None
    pair = x_ctx is not None
    row = lambda c: pl.BlockSpec((tm, d), lambda i: (i, c))
    full = lambda a: pl.BlockSpec(a.shape, lambda i: (0,) * a.ndim)
    x_specs = list(_row_pair_specs(tm, d, x.shape[0])) if pair else [row(0)]
    o_specs = list(_row_pair_specs(tm, d, o.shape[0])) if pair else [row(0)]
    in_specs = (x_specs + [pl.BlockSpec((None, 6, d), lambda i: (mod_idx(i), 0, 0)), row(0), row(0), row(0), row(2)]
                + o_specs + [pl.BlockSpec((tm, 3 * d), lambda i: (i, 1)),
                             full(wp), full(wl), full(wm), full(wo), full(g1), full(g2)])
    args = ([x] + ([x_ctx] if pair else []) + [mod, mp, hf, hb, z, o] + ([o_ctx] if pair else [])
            + [z, wp, wl, wm, wo, g1, g2])
    out_specs = [row(0), row(0)]
    out_shape = [jax.ShapeDtypeStruct((n_rows, d), F32), jax.ShapeDtypeStruct((n_rows, d), BF)]
    if route:
        out_specs[1] = pl.BlockSpec((tm, d // 2), lambda i: (i, 0))
        out_shape[1] = jax.ShapeDtypeStruct((n_rows, d // 2), jnp.uint32)
        in_specs.append(full(rw))
        args.append(rw)
        out_specs.append(pl.BlockSpec((tm, LANE), lambda i: (i, 0)))
        out_shape.append(jax.ShapeDtypeStruct((n_rows, LANE), F32))
        out_specs.append(pl.BlockSpec((SUBLANE, tm), lambda i: (0, i)))
        out_shape.append(jax.ShapeDtypeStruct((SUBLANE, n_rows), F32))
    return pl.pallas_call(
        functools.partial(_merge_kernel, route=route, n_lat_blocks=x.shape[0] // tm if pair else 0),
        grid=(n_rows // tm,),
        in_specs=in_specs, out_specs=out_specs, out_shape=out_shape,
        compiler_params=_cparams("parallel"),
        name="merge",
    )(*args)


def _swiglu_step(x, w1, w3, w2):
    a = _dot(x, w1)
    b = _dot(x, w3)
    return _dot((_silu(a) * b).astype(BF), w2)


def _ffn_kernel(h_ref, x_ref, mod_ref, g_ref, w13_ref, w2_ref, o_ref):
    f = pl.program_id(1)
    tf = w2_ref.shape[0]

    @pl.when(f == 0)
    def _():
        o_ref[...] = jnp.zeros_like(o_ref)

    ab = _dot(h_ref[...], w13_ref[...])
    o_ref[...] += _dot((_silu(ab[:, :tf]) * ab[:, tf:]).astype(BF), w2_ref[...])

    @pl.when(f == pl.num_programs(1) - 1)
    def _():
        o_ref[...] = x_ref[...] + mod_ref[5:6, :] * _rms(o_ref[...], g_ref[...])


def _ffn_call(h2, x1, mod, g, w1, w3, w2, tm, mod_idx):
    m, d = x1.shape
    tf = FFN_TF
    pad = -w1.shape[1] % tf
    w1, w3 = (jnp.pad(w, ((0, 0), (0, pad))) for w in (w1, w3))
    ff = w1.shape[1]
    w13 = jnp.concatenate([w[:, f * tf:(f + 1) * tf].astype(BF) for f in range(ff // tf) for w in (w1, w3)], axis=1)
    w2 = jnp.pad(w2, ((0, pad), (0, 0))).astype(BF)
    return pl.pallas_call(
        _ffn_kernel,
        grid=(m // tm, ff // tf),
        in_specs=[pl.BlockSpec((tm, d), lambda i, f: (i, 0)),
                  pl.BlockSpec((tm, d), lambda i, f: (i, 0)),
                  pl.BlockSpec((None, 6, d), lambda i, f: (mod_idx(i), 0, 0)),
                  pl.BlockSpec((1, d), lambda i, f: (0, 0)),
                  pl.BlockSpec((d, 2 * tf), lambda i, f: (0, f)),
                  pl.BlockSpec((tf, d), lambda i, f: (f, 0))],
        out_specs=pl.BlockSpec((tm, d), lambda i, f: (i, 0)),
        out_shape=jax.ShapeDtypeStruct((m, d), F32),
        compiler_params=_cparams("parallel", "arbitrary"),
        name="ffn",
    )(h2, x1, mod, g, w13, w2)


def _moe_kernel(blk_e_ref, nused_ref, x_ref, w1_ref, w3_ref, w2_ref, o_ref, xb, acc):
    i = pl.program_id(0)
    f = pl.program_id(1)
    used = i < nused_ref[0]
    w = x_ref.shape[1]

    @pl.when(f == 0)
    def _():
        acc[...] = jnp.zeros_like(acc)
        lo, hi = _unpack_rows(x_ref[...])
        xb[:, :w] = lo.astype(BF)
        xb[:, w:] = hi.astype(BF)

    @pl.when(used)
    def _():
        acc[...] += _swiglu_step(xb[...], w1_ref[...].astype(BF), w3_ref[...].astype(BF), w2_ref[...].astype(BF))

    @pl.when(f == pl.num_programs(1) - 1)
    def _():
        o_ref[...] = _pack_rows(acc[...])


def _moe_call(blk_e, nused, xs, w1, w3, w2, tme):
    n_rows, dp = xs.shape
    d = w1.shape[1]
    ff = w1.shape[2]
    tf = 512
    nf = ff // tf

    def ftile(i, f, nu):
        return jnp.where(i < nu[0], f, nf - 1)

    return pl.pallas_call(
        _moe_kernel,
        grid_spec=pltpu.PrefetchScalarGridSpec(
            num_scalar_prefetch=2,
            grid=(n_rows // tme, nf),
            in_specs=[pl.BlockSpec((tme, dp), lambda i, f, be, nu: (i, 0)),
                      pl.BlockSpec((None, d, tf), lambda i, f, be, nu: (be[i], 0, ftile(i, f, nu))),
                      pl.BlockSpec((None, d, tf), lambda i, f, be, nu: (be[i], 0, ftile(i, f, nu))),
                      pl.BlockSpec((None, tf, d), lambda i, f, be, nu: (be[i], ftile(i, f, nu), 0))],
            out_specs=pl.BlockSpec((tme, dp), lambda i, f, be, nu: (i, 0)),
            scratch_shapes=[pltpu.VMEM((tme, d), BF), pltpu.VMEM((tme, d), F32)]),
        out_shape=jax.ShapeDtypeStruct((n_rows, dp), jnp.uint32),
        compiler_params=_cparams("arbitrary", "arbitrary"),
        name="moe",
    )(blk_e, nused, xs, w1, w3, w2)


SC_CORES = 2
SC_SUBCORES = 16
SC_GATHER_ROWS = 128


def _sc_gather_rows(table, idx):
    n_idx = idx.shape[0]
    d = table.shape[1]
    workers = SC_CORES * SC_SUBCORES
    per_w = n_idx // workers
    rows = min(SC_GATHER_ROWS, per_w)
    assert n_idx % (8 * workers) == 0 and per_w % rows == 0
    mesh = plsc.VectorSubcoreMesh(core_axis_name="c", subcore_axis_name="s",
                                  num_cores=SC_CORES, num_subcores=SC_SUBCORES)

    def body(table_hbm, idx_hbm, out_hbm, idx_v, rows_v, sem):
        wid = lax.axis_index("s") * SC_CORES + lax.axis_index("c")
        base = wid * per_w

        @pl.loop(0, per_w // rows)
        def _(it):
            off = pl.multiple_of(base + it * rows, 8)
            pltpu.sync_copy(idx_hbm.at[pl.ds(off, rows)], idx_v)
            pltpu.async_copy(table_hbm.at[idx_v], rows_v, sem).wait()
            pltpu.sync_copy(rows_v, out_hbm.at[pl.ds(off, rows)])

    return pl.kernel(
        body,
        out_type=jax.ShapeDtypeStruct((n_idx, d), table.dtype),
        mesh=mesh,
        scratch_types=[pltpu.VMEM((rows,), jnp.int32), pltpu.VMEM((rows, d), table.dtype),
                       pltpu.SemaphoreType.DMA],
        name="sc_gather",
    )(table, idx)


def _combine_kernel(ya_ref, yb_ref, rt_ref, x_ref, mod_ref, g_ref, o_ref):
    g1 = rt_ref[:, 2:3]
    g2 = rt_ref[:, 3:4]
    a_lo, a_hi = _unpack_rows(ya_ref[...])
    b_lo, b_hi = _unpack_rows(yb_ref[...])
    f = jnp.concatenate([g1 * a_lo + g2 * b_lo, g1 * a_hi + g2 * b_hi], axis=1)
    o_ref[...] = x_ref[...] + mod_ref[5:6, :] * _rms(f, g_ref[...])


def _combine_call(ya, yb, rt, x1, mod, g, tm, mod_idx):
    n, d = x1.shape
    return pl.pallas_call(
        _combine_kernel,
        grid=(n // tm,),
        in_specs=[pl.BlockSpec((tm, d // 2), lambda i: (i, 0)),
                  pl.BlockSpec((tm, d // 2), lambda i: (i, 0)),
                  pl.BlockSpec((tm, LANE), lambda i: (i, 0)),
                  pl.BlockSpec((tm, d), lambda i: (i, 0)),
                  pl.BlockSpec((None, 6, d), lambda i: (mod_idx(i), 0, 0)),
                  pl.BlockSpec((1, d), lambda i: (0, 0))],
        out_specs=pl.BlockSpec((tm, d), lambda i: (i, 0)),
        out_shape=jax.ShapeDtypeStruct((n, d), F32),
        compiler_params=_cparams("parallel"),
        name="combine",
    )(ya, yb, rt, x1, mod, g)


def _route_plan(rt, tme):
    n = rt.shape[1]
    e_flat = rt[:TOP_K].reshape(-1).astype(jnp.int32)
    onehot = (e_flat[:, None] == jnp.arange(N_EXPERTS, dtype=jnp.int32)[None, :]).astype(jnp.int32)
    csum = jnp.cumsum(onehot, axis=0)
    counts = csum[-1]
    padded = (counts + tme - 1) // tme * tme
    pad_end = jnp.cumsum(padded)
    pad_start = pad_end - padded
    dest = jnp.sum(onehot * (csum + pad_start[None, :]), axis=1) - 1
    n_rows = n * TOP_K + N_EXPERTS * tme
    n_blk = n_rows // tme
    src_tok = (jnp.arange(n_rows, dtype=jnp.int32) % n).at[dest].set(
        jnp.arange(n * TOP_K, dtype=jnp.int32) % n, unique_indices=True)
    blk_start = jnp.arange(n_blk, dtype=jnp.int32) * tme
    blk_e = jnp.minimum(jnp.sum((pad_end[None, :] <= blk_start[:, None]).astype(jnp.int32), axis=1), N_EXPERTS - 1)
    nused = (pad_end[-1] // tme).astype(jnp.int32).reshape(1)
    return dest, src_tok, blk_e, nused


def _swap_perm():
    j = np.arange(QK_ROPE)
    axis, half, f = j // (2 * ROPE_FREQS), (j % (2 * ROPE_FREQS)) // ROPE_FREQS, j % ROPE_FREQS
    return axis * 2 * ROPE_FREQS + (1 - half) * ROPE_FREQS + f


def _rope_tables(seq, ctx_len):
    rows = seq // GRID_W
    row = jnp.repeat(jnp.arange(rows, dtype=F32), GRID_W)
    col = (jnp.arange(rows * GRID_W) % GRID_W).astype(F32)
    inv = ROPE_THETA ** (-jnp.arange(ROPE_FREQS, dtype=F32) / ROPE_FREQS)
    ang = jnp.stack([row[:, None] * inv, col[:, None] * inv], axis=1)
    cos, sin = jnp.cos(ang), jnp.sin(ang)
    cos64 = jnp.stack([cos, cos], axis=2).reshape(seq, QK_ROPE)
    sin64 = jnp.stack([-sin, sin], axis=2).reshape(seq, QK_ROPE)
    pad = jnp.zeros((seq, LANE - QK_ROPE), F32)
    cos_t = jnp.concatenate([cos64, pad], axis=1)
    sin_t = jnp.concatenate([sin64, pad], axis=1)
    ident = jnp.concatenate([jnp.ones((ctx_len, QK_ROPE), F32), jnp.zeros((ctx_len, LANE - QK_ROPE), F32)], axis=1)
    return jnp.concatenate([cos_t, ident], axis=0), jnp.concatenate([sin_t, jnp.zeros((ctx_len, LANE), F32)], axis=0)


def _prep_w_in(w):
    d = w.shape[0]
    pw = lw = d
    o = np.cumsum([0, pw, lw, lw, Q_LORA, KV_LORA, QK_ROPE, 3 * d])
    pool, lx, lg, cq, ckv, kr, gt = (w[:, o[i]:o[i + 1]] for i in range(7))
    cols = [pool, lx, lg, gt, cq, ckv, kr, kr[:, _swap_perm()]]
    n = sum(c.shape[1] for c in cols)
    cols.append(jnp.zeros((d, -n % INPROJ_TN), w.dtype))
    return jnp.concatenate(cols, axis=1).astype(BF)


def _prep_w_uq(w):
    qk = QK_NOPE + QK_ROPE
    w = w.reshape(w.shape[0], HEADS, qk)
    rope = w[:, :, QK_NOPE:]
    return jnp.concatenate([w, rope[:, :, _swap_perm()]], axis=2).reshape(w.shape[0], HEADS * HEAD_W).T.astype(BF)


def kernel(x, c, ctx, c_ctx, mod_w, mod_b, pre_mix_g, post_mix_g, pre_ffn_g, post_ffn_g, w_in, pool_w, pool_scale,
           pool_proj, conv_w, conv_b, gate_a_w, gate_a_b, gate_x_w, gate_x_b, lru_lambda, lru_proj, q_norm_g, w_uq,
           kv_norm_g, w_ukv, mla_proj, w_out, ffn_w1, ffn_w3, ffn_w2, router_w, moe_w1, moe_w3, moe_w2):
    bsz, seq, d = x.shape
    ctx_len = ctx.shape[1]
    depth = mod_w.shape[0]
    assert ctx_len == SEQ_TILE and seq % SEQ_TILE == 0 and seq % GRID_W == 0
    n_lat = bsz * seq
    n_ctx = bsz * ctx_len
    n_all = n_lat + n_ctx
    tm = min(1024, seq, n_ctx)
    assert seq % tm == 0 and n_ctx % tm == 0
    tm_merge = tm // 2

    def mod_idx_for(rows):
        return lambda i: jnp.where(i < n_lat // rows, i // (seq // rows), bsz)

    mod_idx = mod_idx_for(tm)

    assert bsz + 1 <= SUBLANE
    cc = jnp.concatenate([c, c_ctx[None, :], jnp.zeros((SUBLANE - bsz - 1, d), F32)], axis=0)
    mods = _mod_call(cc, mod_w, mod_b)[:, :bsz + 1].reshape(depth, bsz + 1, 6, d)
    cos_t, sin_t = _rope_tables(seq, ctx_len)
    x_all, x_ctx = x.reshape(n_lat, d), ctx.reshape(n_ctx, d)
    row1 = lambda v: v.reshape(1, -1)

    for l in range(depth):
        last = l == depth - 1
        n_out = n_lat if last else n_all
        mod = mods[l]
        z = _inproj_call(x_all, x_ctx, mod, row1(pre_mix_g[l]), _prep_w_in(w_in[l]), tm, mod_idx)
        wkv = w_ukv[l].reshape(KV_LORA, HEADS, QK_NOPE + V_DIM)
        wk = wkv[:, :, :QK_NOPE].reshape(KV_LORA, HEADS * QK_NOPE).astype(BF)
        wvt = wkv[:, :, QK_NOPE:].reshape(KV_LORA, HEADS * V_DIM).T.astype(BF)
        q, k, v = _qkv_call(z, cos_t, sin_t, row1(q_norm_g[l]), row1(kv_norm_g[l]), _prep_w_uq(w_uq[l]),
                            wk, wvt, bsz, seq, ctx_len, 6 * d)
        tq = min(512, seq)
        o = _attn_call(q, k, v, bsz=bsz, row_blk0=0, nq=seq // tq, tq=tq, kblk=0, klen=seq + ctx_len)
        o_ctx = None
        if not last:
            o_ctx = _attn_call(q, k, v, bsz=bsz, row_blk0=n_lat // ctx_len, nq=1, tq=ctx_len,
                               kblk=seq // ctx_len, klen=ctx_len)
            if x_ctx is None:
                o, o_ctx = jnp.concatenate([o, o_ctx], axis=0), None
        mp = _pool_call(z, pool_w[l].astype(BF), row1(pool_scale[l]), bsz, seq, ctx_len, n_out)
        hs = []
        for dr in range(2):
            hs.append(_lru_call(z, conv_w[l], row1(conv_b[l]), gate_a_w[l, dr].astype(BF), row1(gate_a_b[l, dr]),
                                gate_x_w[l, dr].astype(BF), row1(gate_x_b[l, dr]), row1(lru_lambda[l, dr]),
                                bsz, seq, dr == 1))
        moe_layer = l % 2 == 1
        rw = None
        if moe_layer:
            rw = router_w[l // 2].T
        outs = _merge_call(x_all, x_ctx if o_ctx is not None else None, mod, mp, hs[0], hs[1], z, o, o_ctx,
                           pool_proj[l].astype(BF), lru_proj[l].astype(BF), mla_proj[l].astype(BF),
                           w_out[l].astype(BF), row1(post_mix_g[l]), row1(pre_ffn_g[l]),
                           rw, tm_merge, n_out, mod_idx_for(tm_merge))
        x_ctx = None
        if not moe_layer:
            x1, h2 = outs
            x_all = _ffn_call(h2, x1, mod, row1(post_ffn_g[l]), ffn_w1[l // 2], ffn_w3[l // 2], ffn_w2[l // 2],
                              tm, mod_idx)
        else:
            x1, h2, rt, rtt = outs
            tme = min(1024, n_out * TOP_K // N_EXPERTS)
            dest, src_tok, blk_e, nused = _route_plan(rtt, tme)
            xs = _sc_gather_rows(h2, src_tok)
            y = _moe_call(blk_e, nused, xs, moe_w1[l // 2], moe_w3[l // 2], moe_w2[l // 2], tme)
            ya = _sc_gather_rows(y, dest[:n_out])
            yb = _sc_gather_rows(y, dest[n_out:])
            x_all = _combine_call(ya, yb, rt, x1, mod, row1(post_ffn_g[l]), tm, mod_idx)
    return x_all[:n_lat].reshape(bsz, seq, d)
```

```python
import functools
import math

import numpy as np
import jax
import jax.numpy as jnp
from jax import lax
from jax.experimental import pallas as pl
from jax.experimental.pallas import tpu as pltpu
from jax.experimental.pallas import tpu_sc as plsc

BF = jnp.bfloat16
F32 = jnp.float32

RMS_EPS = 1e-6
GRID_W = 64
POOL_WINDOWS = (2, 4, 8, 16)
LRU_BLOCKS = 8
CONV_W = 4
LRU_C = 8.0
HEADS = 8
Q_LORA = 384
KV_LORA = 256
QK_NOPE = 128
QK_ROPE = 64
V_DIM = 128
MLA_SCALE = (QK_NOPE + QK_ROPE) ** -0.5
Q_SCALE = MLA_SCALE * math.log2(math.e)
ROPE_FREQS = QK_ROPE // 4
ROPE_THETA = 10000.0
N_EXPERTS = 8
TOP_K = 2

LANE = 128
SUBLANE = 8
HALO = 2 * SUBLANE
SEQ_TILE = 256
HEAD_W = 256
VT_ROWS = V_DIM + HALO
INPROJ_TN = 1024
FFN_TF = 768
V7X_VMEM_BYTES = 64 * 1024 * 1024
VMEM_LIMIT = V7X_VMEM_BYTES * 3 // 4


def _cparams(*sem):
    return pltpu.CompilerParams(dimension_semantics=sem, vmem_limit_bytes=VMEM_LIMIT)


def _rms(x, g):
    ms = jnp.mean(x * x, axis=-1, keepdims=True)
    return x * lax.rsqrt(ms + RMS_EPS) * g


def _sigmoid(x):
    return 0.5 * jnp.tanh(0.5 * x) + 0.5


def _silu(x):
    return x * _sigmoid(x)


def _gelu_tanh(x):
    return 0.5 * x * (1.0 + jnp.tanh(math.sqrt(2.0 / math.pi) * (x + 0.044715 * (x * x * x))))


def _dot(a, b):
    return jnp.dot(a, b, preferred_element_type=F32)


def _pack_rows(x):
    w = x.shape[1] // 2
    lo = lax.bitcast_convert_type(x[:, :w].astype(BF).astype(F32), jnp.uint32)
    hi = lax.bitcast_convert_type(x[:, w:].astype(BF).astype(F32), jnp.uint32)
    return (lo >> 16) | hi


def _unpack_rows(u):
    lo = lax.bitcast_convert_type(u << 16, F32)
    hi = lax.bitcast_convert_type(u & jnp.uint32(0xFFFF0000), F32)
    return lo, hi


def _mod_kernel(c_ref, w_ref, b_ref, o_ref):
    s = _silu(c_ref[...])
    o_ref[...] = _dot(s.astype(BF), w_ref[...].astype(BF)) + b_ref[...]


def _mod_call(cc, mod_w, mod_b):
    depth, d, n6 = mod_w.shape
    tn = 1536
    return pl.pallas_call(
        _mod_kernel,
        grid=(depth, n6 // tn),
        in_specs=[pl.BlockSpec((SUBLANE, d), lambda l, j: (0, 0)),
                  pl.BlockSpec((None, d, tn), lambda l, j: (l, 0, j)),
                  pl.BlockSpec((None, 1, tn), lambda l, j: (l, 0, j))],
        out_specs=pl.BlockSpec((None, SUBLANE, tn), lambda l, j: (l, 0, j)),
        out_shape=jax.ShapeDtypeStruct((depth, SUBLANE, n6), F32),
        compiler_params=_cparams("parallel", "arbitrary"),
        name="mod",
    )(cc, mod_w, mod_b.reshape(depth, 1, n6))


def _row_pair_specs(tm, width, n_lat_rows):
    nb = n_lat_rows // tm
    lat = pl.BlockSpec((tm, width), lambda i, *_: (jnp.minimum(i, nb - 1), 0))
    ctx = pl.BlockSpec((tm, width), lambda i, *_: (jnp.maximum(i - nb, 0), 0))
    return lat, ctx


def _pick(is_ctx, lat_ref, ctx_ref, rs=slice(None)):
    if ctx_ref is None:
        return lat_ref[rs, :]
    return jnp.where(is_ctx, ctx_ref[rs, :], lat_ref[rs, :])


def _inproj_kernel(x_ref, *rest, n_lat_blocks):
    xc_ref = rest[0] if len(rest) == 6 else None
    mod_ref, g_ref, w_ref, z_ref, h_scr = rest[-5:]

    @pl.when(pl.program_id(1) == 0)
    def _():
        h = _rms(_pick(pl.program_id(0) >= n_lat_blocks, x_ref, xc_ref), g_ref[...])
        h = h * (1.0 + mod_ref[1:2, :]) + mod_ref[0:1, :]
        h_scr[...] = h.astype(BF)

    z_ref[...] = _dot(h_scr[...], w_ref[...]).astype(BF)


def _inproj_call(x, x_ctx, mod, g, w, tm, mod_idx):
    d = x.shape[1]
    m = x.shape[0] + (0 if x_ctx is None else x_ctx.shape[0])
    n = w.shape[1]
    tn = INPROJ_TN
    if x_ctx is None:
        x_specs, xs = [pl.BlockSpec((tm, d), lambda i, j: (i, 0))], [x]
    else:
        x_specs, xs = list(_row_pair_specs(tm, d, x.shape[0])), [x, x_ctx]
    return pl.pallas_call(
        functools.partial(_inproj_kernel, n_lat_blocks=x.shape[0] // tm),
        grid=(m // tm, n // tn),
        in_specs=x_specs + [pl.BlockSpec((None, 6, d), lambda i, j: (mod_idx(i), 0, 0)),
                            pl.BlockSpec((1, d), lambda i, j: (0, 0)),
                            pl.BlockSpec((d, tn), lambda i, j: (0, j))],
        out_specs=pl.BlockSpec((tm, tn), lambda i, j: (i, j)),
        out_shape=jax.ShapeDtypeStruct((m, n), BF),
        scratch_shapes=[pltpu.VMEM((tm, d), BF)],
        compiler_params=_cparams("parallel", "arbitrary"),
        name="inproj",
    )(*xs, mod, g, w)


def _rope(x, cos, sin):
    return x * cos + pltpu.roll(x, LANE // 2, 1) * sin


_NT = (((1,), (1,)), ((), ()))


def _qkv_kernel(z_ref, cos_ref, sin_ref, cost_ref, sint_ref, qg_ref, kvg_ref, wuqt_ref, wk_ref, wvt_ref,
                qt_ref, k_ref, vt_ref):
    z = z_ref[...]
    cq = z[:, :Q_LORA].astype(F32)
    ckv = z[:, Q_LORA:Q_LORA + KV_LORA].astype(F32)
    kr = z[:, Q_LORA + KV_LORA:].astype(F32)
    cqn = _rms(cq, qg_ref[...]).astype(BF)
    ckvn = _rms(ckv, kvg_ref[...]).astype(BF)
    qt = lax.dot_general(wuqt_ref[...], cqn, _NT, preferred_element_type=F32)
    kn = _dot(ckvn, wk_ref[...])
    vt = lax.dot_general(wvt_ref[...], ckvn, _NT, preferred_element_type=F32)
    krot = _rope(kr, cos_ref[...], sin_ref[...]).astype(BF)
    cost = cost_ref[...]
    sint = sint_ref[...]
    for h in range(HEADS):
        c0 = h * HEAD_W
        c1 = c0 + QK_NOPE
        c2 = c1 + QK_ROPE
        qt_ref[c0:c1, :] = (qt[c0:c1, :] * Q_SCALE).astype(BF)
        qt_ref[c1:c2, :] = ((qt[c1:c2, :] * cost + qt[c2:c0 + HEAD_W, :] * sint) * Q_SCALE).astype(BF)
        qt_ref[c2:c0 + HEAD_W, :] = jnp.zeros((QK_ROPE, qt.shape[1]), BF)
        k_ref[h, :, 0:QK_NOPE] = kn[:, h * QK_NOPE:(h + 1) * QK_NOPE].astype(BF)
        k_ref[h, :, QK_NOPE:HEAD_W] = krot
        vt_ref[h, 0:V_DIM, :] = vt[h * V_DIM:(h + 1) * V_DIM, :].astype(BF)
        vt_ref[h, V_DIM:VT_ROWS, :] = jnp.ones((VT_ROWS - V_DIM, vt.shape[1]), BF)


def _qkv_call(z, cos_t, sin_t, qg, kvg, wuqt, wk, wvt, bsz, seq, ctx_len, z_off):
    m = z.shape[0]
    cos_tt = cos_t[:, :QK_ROPE].T
    sin_tt = sin_t[:, :QK_ROPE].T
    ts = SEQ_TILE
    nt = seq // ts
    nlat = bsz * nt
    lk = seq + ctx_len
    zw = Q_LORA + KV_LORA + LANE
    assert z_off % zw == 0
    zcol = z_off // zw

    def tab_idx(i):
        return (jnp.where(i < nlat, i % nt, nt), 0)

    def kv_idx(i):
        return (jnp.where(i < nlat, i // nt, i - nlat), 0, jnp.where(i < nlat, i % nt, nt), 0)

    def vt_idx(i):
        return (jnp.where(i < nlat, i // nt, i - nlat), 0, 0, jnp.where(i < nlat, i % nt, nt))

    return pl.pallas_call(
        _qkv_kernel,
        grid=(m // ts,),
        in_specs=[pl.BlockSpec((ts, zw), lambda i: (i, zcol)),
                  pl.BlockSpec((ts, LANE), tab_idx),
                  pl.BlockSpec((ts, LANE), tab_idx),
                  pl.BlockSpec((QK_ROPE, ts), lambda i: tab_idx(i)[::-1]),
                  pl.BlockSpec((QK_ROPE, ts), lambda i: tab_idx(i)[::-1]),
                  pl.BlockSpec((1, Q_LORA), lambda i: (0, 0)),
                  pl.BlockSpec((1, KV_LORA), lambda i: (0, 0)),
                  pl.BlockSpec(wuqt.shape, lambda i: (0, 0)),
                  pl.BlockSpec(wk.shape, lambda i: (0, 0)),
                  pl.BlockSpec(wvt.shape, lambda i: (0, 0))],
        out_specs=[pl.BlockSpec((HEADS * HEAD_W, ts), lambda i: (0, i)),
                   pl.BlockSpec((None, HEADS, ts, HEAD_W), kv_idx),
                   pl.BlockSpec((None, HEADS, VT_ROWS, ts), vt_idx)],
        out_shape=[jax.ShapeDtypeStruct((HEADS * HEAD_W, m), BF),
                   jax.ShapeDtypeStruct((bsz, HEADS, lk, HEAD_W), BF),
                   jax.ShapeDtypeStruct((bsz, HEADS, VT_ROWS, lk), BF)],
        compiler_params=_cparams("parallel"),
        name="qkv_up",
    )(z, cos_t, sin_t, cos_tt, sin_tt, qg, kvg, wuqt, wk, wvt)


def _col_reduce(x, pair, red):
    n = x.shape[0] // 4
    a = pair(pair(x[0:n], x[n:2 * n]), pair(x[2 * n:3 * n], x[3 * n:4 * n]))
    return red(a, axis=0, keepdims=True)


ATTN_LAG_LIMIT = 20.0


def _attn_kernel(qt_ref, k_ref, vt_ref, o_ref, p_scr, *, chunks, kp):
    qt = qt_ref[...]
    tq = qt.shape[1]

    nk = len(chunks)

    def scores(c):
        k0, kn = chunks[c]
        return _dot(k_ref[k0:k0 + kn, :], qt)

    def values(c):
        k0, kn = chunks[c]
        return _dot(vt_ref[:, k0:k0 + kn], p_scr[c % 2, 0:kn, :])

    def finish(a):
        o_ref[...] = (a[0:V_DIM] / a[V_DIM:V_DIM + 1]).T.astype(o_ref.dtype)

    s = scores(0)
    ref = _col_reduce(s, jnp.maximum, jnp.max)
    p_scr[0, 0:chunks[0][1], :] = jnp.exp2((s - ref).astype(BF))
    acc = alpha = None
    lag = jnp.zeros((1, tq), F32)
    for c in range(1, nk):
        s = scores(c)
        p_scr[c % 2, 0:chunks[c][1], :] = jnp.exp2((s - ref).astype(BF))
        mc = _col_reduce(s, jnp.maximum, jnp.max)
        pv = values(c - 1)
        acc = pv if acc is None else acc + pv
        if alpha is not None:
            acc = acc * alpha
        lag = jnp.maximum(lag, mc - ref)
        new_ref = jnp.maximum(ref, mc)
        alpha = jnp.exp2(ref - new_ref)
        ref = new_ref
    pv = values(nk - 1)
    acc = pv if acc is None else acc + pv
    over = jnp.max(lag) > ATTN_LAG_LIMIT

    @pl.when(jnp.logical_not(over))
    def _():
        finish(acc)

    @pl.when(over)
    def _():
        def body(j, carry):
            m, a = carry
            r0 = pl.multiple_of(j * kp, kp)
            sj = _dot(k_ref[pl.ds(r0, kp), :], qt)
            m_new = jnp.maximum(m, jnp.max(sj, axis=0, keepdims=True))
            pj = jnp.exp2((sj - m_new).astype(BF))
            a = jnp.exp2(m - m_new) * a + _dot(vt_ref[:, pl.ds(r0, kp)], pj)
            return m_new, a

        init = (jnp.full((1, tq), -1e30, F32), jnp.zeros((VT_ROWS, tq), F32))
        finish(lax.fori_loop(0, k_ref.shape[0] // kp, body, init)[1])


ATTN_FIRST = 256
ATTN_CHUNK = 512


def _attn_call(q, k, v, *, bsz, row_blk0, nq, tq, kblk, klen):
    rest = klen - ATTN_FIRST
    step = ATTN_CHUNK if rest % ATTN_CHUNK == 0 else ATTN_FIRST
    assert rest % step == 0
    chunks = ((0, ATTN_FIRST),) + tuple((ATTN_FIRST + i * step, step) for i in range(rest // step))
    tk = max(n for _, n in chunks)
    kern = functools.partial(_attn_kernel, chunks=chunks, kp=ATTN_FIRST)
    return pl.pallas_call(
        kern,
        grid=(bsz, HEADS, nq),
        in_specs=[pl.BlockSpec((HEAD_W, tq), lambda b, h, i: (h, row_blk0 + b * nq + i)),
                  pl.BlockSpec((None, None, klen, HEAD_W), lambda b, h, i: (b, h, kblk, 0)),
                  pl.BlockSpec((None, None, VT_ROWS, klen), lambda b, h, i: (b, h, 0, kblk))],
        out_specs=pl.BlockSpec((tq, V_DIM), lambda b, h, i: (b * nq + i, h)),
        out_shape=jax.ShapeDtypeStruct((bsz * nq * tq, HEADS * V_DIM), BF),
        scratch_shapes=[pltpu.VMEM((2, tk, tq), BF)],
        compiler_params=_cparams("parallel", "parallel", "arbitrary"),
        name="attn",
    )(q, k, v)


def _seq_flags(i, nlat, nt):
    is_ctx = i >= nlat
    t = jnp.where(is_ctx, 0, i % nt)
    first = jnp.logical_or(is_ctx, t == 0)
    last = jnp.logical_or(is_ctx, t == nt - 1)
    return is_ctx, t, first, last


def _pool_kernel(x_ref, xp_ref, xn_ref, pw_ref, ps_ref, o_ref, *, nlat, nt, seq, ctx_len):
    ts = x_ref.shape[0]
    is_ctx, t, first, last = _seq_flags(pl.program_id(0), nlat, nt)
    seq_len = jnp.where(is_ctx, ctx_len, seq)
    x = x_ref[...]
    xp = jnp.where(first, jnp.zeros_like(xp_ref[...]), xp_ref[...])
    xn = jnp.where(last, jnp.zeros_like(xn_ref[...]), xn_ref[...])
    xe = jnp.concatenate([xp, x, xn], axis=0)
    tpos = t * ts + lax.broadcasted_iota(jnp.int32, (ts, 1), 0)
    rel = (lax.broadcasted_iota(jnp.int32, (ts, ts + 2 * HALO), 1) - HALO
           - lax.broadcasted_iota(jnp.int32, (ts, ts + 2 * HALO), 0))
    gw = x.shape[1] // len(POOL_WINDOWS)
    cols = [slice(g * gw, (g + 1) * gw) for g in range(len(POOL_WINDOWS))]
    sums = []
    for cs, w in zip(cols, POOL_WINDOWS):
        band = jnp.where(rel >= -(w // 2), jnp.where(rel < w - w // 2, 1.0, 0.0), 0.0).astype(BF)
        sums.append(_dot(band, xe[:, cs]))
    for g, (cs, w) in enumerate(zip(cols, POOL_WINDOWS)):
        cnt = (jnp.minimum(tpos + (w - w // 2), seq_len) - jnp.maximum(tpos - w // 2, 0)).astype(F32)
        mean_minus = sums[g] / cnt - x[:, cs].astype(F32)
        o_ref[:, cs] = (_dot(mean_minus.astype(BF), pw_ref[g]) * ps_ref[:, cs]).astype(BF)


def _halo_specs(ts, width, col, row_blk, m):
    r = ts // HALO
    nh = m // HALO
    prev = pl.BlockSpec((HALO, width), lambda *a: (jnp.maximum(row_blk(*a) * r - 1, 0), col))
    nxt = pl.BlockSpec((HALO, width), lambda *a: (jnp.minimum((row_blk(*a) + 1) * r, nh - 1), col))
    return prev, nxt


def _pool_call(z, pool_w, pool_scale, bsz, seq, ctx_len, n_rows):
    m = z.shape[0]
    ts = SEQ_TILE
    nt = seq // ts
    nlat = bsz * nt
    width = pool_scale.shape[1]
    prev, nxt = _halo_specs(ts, width, 0, lambda i: i, m)
    kern = functools.partial(_pool_kernel, nlat=nlat, nt=nt, seq=seq, ctx_len=ctx_len)
    return pl.pallas_call(
        kern,
        grid=(n_rows // ts,),
        in_specs=[pl.BlockSpec((ts, width), lambda i: (i, 0)), prev, nxt,
                  pl.BlockSpec(pool_w.shape, lambda i: (0, 0, 0)),
                  pl.BlockSpec((1, width), lambda i: (0, 0))],
        out_specs=pl.BlockSpec((ts, width), lambda i: (i, 0)),
        out_shape=jax.ShapeDtypeStruct((n_rows, width), BF),
        compiler_params=_cparams("parallel"),
        name="pool",
    )(z, z, z, pool_w, pool_scale)


def _lru_kernel(x_ref, xp_ref, xn_ref, cw_ref, cb_ref, wa_ref, ba_ref, wx_ref, bx_ref, lam_ref, o_ref,
                xf_scr, hl_scr, ca_scr, ga_scr, gb_scr, hp_scr, h_scr, *, reverse, nt):
    ts, width = x_ref.shape
    gs = SUBLANE
    ng = ts // gs
    bw = width // LRU_BLOCKS
    assert LRU_BLOCKS == gs
    s = pl.program_id(1)
    t = (nt - s) if reverse else (s - 1)
    first = jnp.logical_or(s == 0, t == 0)
    last = jnp.logical_or(s == 0, t == nt - 1)

    @pl.when(s == 0)
    def _():
        h_scr[...] = jnp.zeros_like(h_scr)

    left = CONV_W // 2
    xp = xp_ref[...].astype(F32)[HALO - gs:HALO]
    xn = xn_ref[...].astype(F32)[0:gs]
    xp = jnp.where(first, jnp.zeros_like(xp), xp)
    xn = jnp.where(last, jnp.zeros_like(xn), xn)
    lam = lam_ref[...]
    neg = -lam
    softplus = jnp.maximum(neg, 0.0) + jnp.log1p(jnp.exp(-jnp.abs(neg)))
    coef = (-LRU_C * math.log2(math.e)) * softplus
    row = lax.broadcasted_iota(jnp.int32, (ng, bw), 0)
    slab = lambda n, j: (n, pl.ds(j, ng, stride=gs), slice(None))
    order = range(gs - 1, -1, -1) if reverse else range(gs)

    for n in range(LRU_BLOCKS):
        cs = slice(n * bw, (n + 1) * bw)
        xf_scr[n] = x_ref[:, cs].astype(F32)
        xs = {j: xf_scr[slab(n, j)] for j in range(gs)}
        for j in range(-left, 0):
            xs[j] = jnp.where(row == 0, xp[gs + j:gs + j + 1, cs], pltpu.roll(xs[gs + j], 1, 0))
        for j in range(gs, gs + CONV_W - 1 - left):
            xs[j] = jnp.where(row == ng - 1, xn[j - gs:j - gs + 1, cs], pltpu.roll(xs[j - gs], ng - 1, 0))
        us = []
        for j in range(gs):
            u = cb_ref[:, cs] + cw_ref[0:1, cs] * xs[j - left]
            for k in range(1, CONV_W):
                u = u + cw_ref[k:k + 1, cs] * xs[j - left + k]
            us.append(u)
        un = jnp.concatenate(us, axis=0)
        ub = un.astype(BF)
        r = _sigmoid(_dot(ub, wa_ref[n]) + ba_ref[:, cs])
        gi = _sigmoid(_dot(ub, wx_ref[n]) + bx_ref[:, cs])
        a = jnp.exp2(r * coef[:, cs])
        om = 1.0 - a * a
        b = (om * lax.rsqrt(jnp.maximum(om, 1e-30))) * (gi * un)
        hl = ca = None
        for j in order:
            aj = a[j * ng:(j + 1) * ng]
            bj = b[j * ng:(j + 1) * ng]
            hl, ca = (bj, aj) if hl is None else (aj * hl + bj, aj * ca)
            hl_scr[n, j * ng:(j + 1) * ng, :] = hl
            ca_scr[n, j * ng:(j + 1) * ng, :] = ca
        ga_scr[pl.ds(n, ng, stride=LRU_BLOCKS), :] = ca
        gb_scr[pl.ds(n, ng, stride=LRU_BLOCKS), :] = hl

    h = h_scr[...]
    for g in (range(ng - 1, -1, -1) if reverse else range(ng)):
        rows = slice(g * LRU_BLOCKS, (g + 1) * LRU_BLOCKS)
        hp_scr[rows, :] = h
        h = ga_scr[rows, :] * h + gb_scr[rows, :]
    h_scr[...] = h

    for n in range(LRU_BLOCKS):
        hp = hp_scr[pl.ds(n, ng, stride=LRU_BLOCKS), :]
        for j in range(gs):
            xf_scr[slab(n, j)] = hl_scr[n, j * ng:(j + 1) * ng, :] + ca_scr[n, j * ng:(j + 1) * ng, :] * hp
        o_ref[:, n * bw:(n + 1) * bw] = xf_scr[n].astype(o_ref.dtype)


def _lru_call(z, conv_w, conv_b, wa, ba, wx, bx, lam, bsz, seq, reverse):
    m = z.shape[0]
    ts = SEQ_TILE
    nt = seq // ts
    nlat = bsz * nt
    width = conv_b.shape[1]

    def row_blk(b, s):
        t = (nt - s) if reverse else (s - 1)
        return jnp.where(s == 0, nlat + b, b * nt + t)

    prev, nxt = _halo_specs(ts, width, 1, row_blk, m)
    vec = lambda shape: pl.BlockSpec(shape, lambda b, s: (0,) * len(shape))
    kern = functools.partial(_lru_kernel, reverse=reverse, nt=nt)
    return pl.pallas_call(
        kern,
        grid=(bsz, nt + 1),
        in_specs=[pl.BlockSpec((ts, width), lambda b, s: (row_blk(b, s), 1)), prev, nxt,
                  vec(conv_w.shape), vec(conv_b.shape), vec(wa.shape), vec(ba.shape),
                  vec(wx.shape), vec(bx.shape), vec(lam.shape)],
        out_specs=pl.BlockSpec((ts, width), lambda b, s: (row_blk(b, s), 0)),
        out_shape=jax.ShapeDtypeStruct((m, width), BF),
        scratch_shapes=[pltpu.VMEM((LRU_BLOCKS, ts, width // LRU_BLOCKS), F32)] * 3
        + [pltpu.VMEM((ts // SUBLANE * LRU_BLOCKS, width // LRU_BLOCKS), F32)] * 3
        + [pltpu.VMEM((LRU_BLOCKS, width // LRU_BLOCKS), F32)],
        compiler_params=_cparams("parallel", "arbitrary"),
        name="lru_bwd" if reverse else "lru_fwd",
    )(z, z, z, conv_w, conv_b, wa, ba, wx, bx, lam)


def _merge_kernel(*refs, route, n_lat_blocks):
    refs = list(refs)
    x_ref = refs.pop(0)
    xc_ref = refs.pop(0) if n_lat_blocks else None
    mod_ref, mp_ref, hf_ref, hb_ref, lg_ref, o_ref = (refs.pop(0) for _ in range(6))
    oc_ref = refs.pop(0) if n_lat_blocks else None
    gt_ref, wp_ref, wl_ref, wm_ref, wo_ref, g1_ref, g2_ref = (refs.pop(0) for _ in range(7))
    if route:
        rw_ref, x1_ref, h2_ref, rt_ref, rtt_ref = refs
    else:
        x1_ref, h2_ref = refs
    is_ctx = pl.program_id(0) >= n_lat_blocks if n_lat_blocks else None
    tm, d = x_ref.shape
    halves = [slice(0, tm // 2), slice(tm // 2, tm)]

    def branches(rs):
        y_pool = _dot(mp_ref[rs, :], wp_ref[...])
        lru_in = (hf_ref[rs, :].astype(F32) + hb_ref[rs, :].astype(F32)) * _gelu_tanh(lg_ref[rs, :].astype(F32))
        y_lru = _dot(lru_in.astype(BF), wl_ref[...])
        y_mla = _dot(_pick(is_ctx, o_ref, oc_ref, rs), wm_ref[...])
        return y_pool, y_lru, y_mla

    def mixed(rs, ys):
        mix = (_sigmoid(gt_ref[rs, 0:d].astype(F32)) * ys[0]
               + _sigmoid(gt_ref[rs, d:2 * d].astype(F32)) * ys[1]
               + _sigmoid(gt_ref[rs, 2 * d:3 * d].astype(F32)) * ys[2])
        return _dot(mix.astype(BF), wo_ref[...])

    ys = [branches(rs) for rs in halves]
    outs = [mixed(rs, y) for rs, y in zip(halves, ys)]
    for rs, y in zip(halves, outs):
        x1 = _pick(is_ctx, x_ref, xc_ref, rs) + mod_ref[2:3, :] * _rms(y, g1_ref[...])
        x1_ref[rs, :] = x1
        h2 = _rms(x1, g2_ref[...]) * (1.0 + mod_ref[4:5, :]) + mod_ref[3:4, :]
        if route:
            h2_ref[rs, :] = _pack_rows(h2)
            _route_rows(h2, rw_ref, rt_ref, rtt_ref, rs)
        else:
            h2_ref[rs, :] = h2.astype(h2_ref.dtype)


def _route_rows(h2, rw_ref, rt_ref, rtt_ref, rs):
    logit = [jnp.sum(h2 * rw_ref[e:e + 1, :], axis=1, keepdims=True) for e in range(N_EXPERTS)]
    v1, i1 = logit[0], jnp.zeros_like(logit[0])
    for e in range(1, N_EXPERTS):
        upd = logit[e] > v1
        v1 = jnp.where(upd, logit[e], v1)
        i1 = jnp.where(upd, float(e), i1)
    v2, i2 = jnp.full_like(v1, -jnp.inf), jnp.zeros_like(v1)
    for e in range(N_EXPERTS):
        cand = jnp.where(i1 == float(e), -jnp.inf, logit[e])
        upd = cand > v2
        v2 = jnp.where(upd, cand, v2)
        i2 = jnp.where(upd, float(e), i2)
    ex = jnp.exp(v2 - v1)
    gate1 = 1.0 / (1.0 + ex)
    gate2 = ex / (1.0 + ex)
    col = lax.broadcasted_iota(jnp.int32, (h2.shape[0], rt_ref.shape[1]), 1)
    table = jnp.where(col == 0, i1, jnp.where(col == 1, i2,
                      jnp.where(col == 2, gate1, jnp.where(col == 3, gate2, 0.0))))
    rt_ref[rs, :] = table
    rtt_ref[:, rs] = table.T[0:rtt_ref.shape[0], :]


def _merge_call(x, x_ctx, mod, mp, hf, hb, z, o, o_ctx, wp, wl, wm, wo, g1, g2, rw, tm, n_rows, mod_idx):
    d = x.shape[1]
    route = rw is not None
    pair = x_ctx is not None
    row = lambda c: pl.BlockSpec((tm, d), lambda i: (i, c))
    full = lambda a: pl.BlockSpec(a.shape, lambda i: (0,) * a.ndim)
    x_specs = list(_row_pair_specs(tm, d, x.shape[0])) if pair else [row(0)]
    o_specs = list(_row_pair_specs(tm, d, o.shape[0])) if pair else [row(0)]
    in_specs = (x_specs + [pl.BlockSpec((None, 6, d), lambda i: (mod_idx(i), 0, 0)), row(0), row(0), row(0), row(2)]
                + o_specs + [pl.BlockSpec((tm, 3 * d), lambda i: (i, 1)),
                             full(wp), full(wl), full(wm), full(wo), full(g1), full(g2)])
    args = ([x] + ([x_ctx] if pair else []) + [mod, mp, hf, hb, z, o] + ([o_ctx] if pair else [])
            + [z, wp, wl, wm, wo, g1, g2])
    out_specs = [row(0), row(0)]
    out_shape = [jax.ShapeDtypeStruct((n_rows, d), F32), jax.ShapeDtypeStruct((n_rows, d), BF)]
    if route:
        out_specs[1] = pl.BlockSpec((tm, d // 2), lambda i: (i, 0))
        out_shape[1] = jax.ShapeDtypeStruct((n_rows, d // 2), jnp.uint32)
        in_specs.append(full(rw))
        args.append(rw)
        out_specs.append(pl.BlockSpec((tm, LANE), lambda i: (i, 0)))
        out_shape.append(jax.ShapeDtypeStruct((n_rows, LANE), F32))
        out_specs.append(pl.BlockSpec((SUBLANE, tm), lambda i: (0, i)))
        out_shape.append(jax.ShapeDtypeStruct((SUBLANE, n_rows), F32))
    return pl.pallas_call(
        functools.partial(_merge_kernel, route=route, n_lat_blocks=x.shape[0] // tm if pair else 0),
        grid=(n_rows // tm,),
        in_specs=in_specs, out_specs=out_specs, out_shape=out_shape,
        compiler_params=_cparams("parallel"),
        name="merge",
    )(*args)


def _swiglu_step(x, w1, w3, w2):
    a = _dot(x, w1)
    b = _dot(x, w3)
    return _dot((_silu(a) * b).astype(BF), w2)


def _ffn_kernel(h_ref, x_ref, mod_ref, g_ref, w13_ref, w2_ref, o_ref):
    f = pl.program_id(1)
    tf = w2_ref.shape[0]

    @pl.when(f == 0)
    def _():
        o_ref[...] = jnp.zeros_like(o_ref)

    ab = _dot(h_ref[...], w13_ref[...])
    o_ref[...] += _dot((_silu(ab[:, :tf]) * ab[:, tf:]).astype(BF), w2_ref[...])

    @pl.when(f == pl.num_programs(1) - 1)
    def _():
        o_ref[...] = x_ref[...] + mod_ref[5:6, :] * _rms(o_ref[...], g_ref[...])


def _ffn_call(h2, x1, mod, g, w1, w3, w2, tm, mod_idx):
    m, d = x1.shape
    tf = FFN_TF
    pad = -w1.shape[1] % tf
    w1, w3 = (jnp.pad(w, ((0, 0), (0, pad))) for w in (w1, w3))
    ff = w1.shape[1]
    w13 = jnp.concatenate([w[:, f * tf:(f + 1) * tf].astype(BF) for f in range(ff // tf) for w in (w1, w3)], axis=1)
    w2 = jnp.pad(w2, ((0, pad), (0, 0))).astype(BF)
    return pl.pallas_call(
        _ffn_kernel,
        grid=(m // tm, ff // tf),
        in_specs=[pl.BlockSpec((tm, d), lambda i, f: (i, 0)),
                  pl.BlockSpec((tm, d), lambda i, f: (i, 0)),
                  pl.BlockSpec((None, 6, d), lambda i, f: (mod_idx(i), 0, 0)),
                  pl.BlockSpec((1, d), lambda i, f: (0, 0)),
                  pl.BlockSpec((d, 2 * tf), lambda i, f: (0, f)),
                  pl.BlockSpec((tf, d), lambda i, f: (f, 0))],
        out_specs=pl.BlockSpec((tm, d), lambda i, f: (i, 0)),
        out_shape=jax.ShapeDtypeStruct((m, d), F32),
        compiler_params=_cparams("parallel", "arbitrary"),
        name="ffn",
    )(h2, x1, mod, g, w13, w2)


def _moe_kernel(blk_e_ref, nused_ref, x_ref, w1_ref, w3_ref, w2_ref, o_ref, xb, acc):
    i = pl.program_id(0)
    f = pl.program_id(1)
    used = i < nused_ref[0]
    w = x_ref.shape[1]

    @pl.when(f == 0)
    def _():
        acc[...] = jnp.zeros_like(acc)
        lo, hi = _unpack_rows(x_ref[...])
        xb[:, :w] = lo.astype(BF)
        xb[:, w:] = hi.astype(BF)

    @pl.when(used)
    def _():
        acc[...] += _swiglu_step(xb[...], w1_ref[...].astype(BF), w3_ref[...].astype(BF), w2_ref[...].astype(BF))

    @pl.when(f == pl.num_programs(1) - 1)
    def _():
        o_ref[...] = _pack_rows(acc[...])


def _moe_call(blk_e, nused, xs, w1, w3, w2, tme):
    n_rows, dp = xs.shape
    d = w1.shape[1]
    ff = w1.shape[2]
    tf = 512
    nf = ff // tf

    def ftile(i, f, nu):
        return jnp.where(i < nu[0], f, nf - 1)

    return pl.pallas_call(
        _moe_kernel,
        grid_spec=pltpu.PrefetchScalarGridSpec(
            num_scalar_prefetch=2,
            grid=(n_rows // tme, nf),
            in_specs=[pl.BlockSpec((tme, dp), lambda i, f, be, nu: (i, 0)),
                      pl.BlockSpec((None, d, tf), lambda i, f, be, nu: (be[i], 0, ftile(i, f, nu))),
                      pl.BlockSpec((None, d, tf), lambda i, f, be, nu: (be[i], 0, ftile(i, f, nu))),
                      pl.BlockSpec((None, tf, d), lambda i, f, be, nu: (be[i], ftile(i, f, nu), 0))],
            out_specs=pl.BlockSpec((tme, dp), lambda i, f, be, nu: (i, 0)),
            scratch_shapes=[pltpu.VMEM((tme, d), BF), pltpu.VMEM((tme, d), F32)]),
        out_shape=jax.ShapeDtypeStruct((n_rows, dp), jnp.uint32),
        compiler_params=_cparams("arbitrary", "arbitrary"),
        name="moe",
    )(blk_e, nused, xs, w1, w3, w2)


SC_CORES = 2
SC_SUBCORES = 16
SC_GATHER_ROWS = 128


def _sc_gather_rows(table, idx):
    n_idx = idx.shape[0]
    d = table.shape[1]
    workers = SC_CORES * SC_SUBCORES
    per_w = n_idx // workers
    rows = min(SC_GATHER_ROWS, per_w)
    assert n_idx % (8 * workers) == 0 and per_w % rows == 0
    mesh = plsc.VectorSubcoreMesh(core_axis_name="c", subcore_axis_name="s",
                                  num_cores=SC_CORES, num_subcores=SC_SUBCORES)

    def body(table_hbm, idx_hbm, out_hbm, idx_v, rows_v, sem):
        wid = lax.axis_index("s") * SC_CORES + lax.axis_index("c")
        base = wid * per_w

        @pl.loop(0, per_w // rows)
        def _(it):
            off = pl.multiple_of(base + it * rows, 8)
            pltpu.sync_copy(idx_hbm.at[pl.ds(off, rows)], idx_v)
            pltpu.async_copy(table_hbm.at[idx_v], rows_v, sem).wait()
            pltpu.sync_copy(rows_v, out_hbm.at[pl.ds(off, rows)])

    return pl.kernel(
        body,
        out_type=jax.ShapeDtypeStruct((n_idx, d), table.dtype),
        mesh=mesh,
        scratch_types=[pltpu.VMEM((rows,), jnp.int32), pltpu.VMEM((rows, d), table.dtype),
                       pltpu.SemaphoreType.DMA],
        name="sc_gather",
    )(table, idx)


def _combine_kernel(ya_ref, yb_ref, rt_ref, x_ref, mod_ref, g_ref, o_ref):
    g1 = rt_ref[:, 2:3]
    g2 = rt_ref[:, 3:4]
    a_lo, a_hi = _unpack_rows(ya_ref[...])
    b_lo, b_hi = _unpack_rows(yb_ref[...])
    f = jnp.concatenate([g1 * a_lo + g2 * b_lo, g1 * a_hi + g2 * b_hi], axis=1)
    o_ref[...] = x_ref[...] + mod_ref[5:6, :] * _rms(f, g_ref[...])


def _combine_call(ya, yb, rt, x1, mod, g, tm, mod_idx):
    n, d = x1.shape
    return pl.pallas_call(
        _combine_kernel,
        grid=(n // tm,),
        in_specs=[pl.BlockSpec((tm, d // 2), lambda i: (i, 0)),
                  pl.BlockSpec((tm, d // 2), lambda i: (i, 0)),
                  pl.BlockSpec((tm, LANE), lambda i: (i, 0)),
                  pl.BlockSpec((tm, d), lambda i: (i, 0)),
                  pl.BlockSpec((None, 6, d), lambda i: (mod_idx(i), 0, 0)),
                  pl.BlockSpec((1, d), lambda i: (0, 0))],
        out_specs=pl.BlockSpec((tm, d), lambda i: (i, 0)),
        out_shape=jax.ShapeDtypeStruct((n, d), F32),
        compiler_params=_cparams("parallel"),
        name="combine",
    )(ya, yb, rt, x1, mod, g)


def _route_plan(rt, tme):
    n = rt.shape[1]
    e_flat = rt[:TOP_K].reshape(-1).astype(jnp.int32)
    onehot = (e_flat[:, None] == jnp.arange(N_EXPERTS, dtype=jnp.int32)[None, :]).astype(jnp.int32)
    csum = jnp.cumsum(onehot, axis=0)
    counts = csum[-1]
    padded = (counts + tme - 1) // tme * tme
    pad_end = jnp.cumsum(padded)
    pad_start = pad_end - padded
    dest = jnp.sum(onehot * (csum + pad_start[None, :]), axis=1) - 1
    n_rows = n * TOP_K + N_EXPERTS * tme
    n_blk = n_rows // tme
    src_tok = (jnp.arange(n_rows, dtype=jnp.int32) % n).at[dest].set(
        jnp.arange(n * TOP_K, dtype=jnp.int32) % n, unique_indices=True)
    blk_start = jnp.arange(n_blk, dtype=jnp.int32) * tme
    blk_e = jnp.minimum(jnp.sum((pad_end[None, :] <= blk_start[:, None]).astype(jnp.int32), axis=1), N_EXPERTS - 1)
    nused = (pad_end[-1] // tme).astype(jnp.int32).reshape(1)
    return dest, src_tok, blk_e, nused


def _swap_perm():
    j = np.arange(QK_ROPE)
    axis, half, f = j // (2 * ROPE_FREQS), (j % (2 * ROPE_FREQS)) // ROPE_FREQS, j % ROPE_FREQS
    return axis * 2 * ROPE_FREQS + (1 - half) * ROPE_FREQS + f


def _rope_tables(seq, ctx_len):
    rows = seq // GRID_W
    row = jnp.repeat(jnp.arange(rows, dtype=F32), GRID_W)
    col = (jnp.arange(rows * GRID_W) % GRID_W).astype(F32)
    inv = ROPE_THETA ** (-jnp.arange(ROPE_FREQS, dtype=F32) / ROPE_FREQS)
    ang = jnp.stack([row[:, None] * inv, col[:, None] * inv], axis=1)
    cos, sin = jnp.cos(ang), jnp.sin(ang)
    cos64 = jnp.stack([cos, cos], axis=2).reshape(seq, QK_ROPE)
    sin64 = jnp.stack([-sin, sin], axis=2).reshape(seq, QK_ROPE)
    pad = jnp.zeros((seq, LANE - QK_ROPE), F32)
    cos_t = jnp.concatenate([cos64, pad], axis=1)
    sin_t = jnp.concatenate([sin64, pad], axis=1)
    ident = jnp.concatenate([jnp.ones((ctx_len, QK_ROPE), F32), jnp.zeros((ctx_len, LANE - QK_ROPE), F32)], axis=1)
    return jnp.concatenate([cos_t, ident], axis=0), jnp.concatenate([sin_t, jnp.zeros((ctx_len, LANE), F32)], axis=0)


def _prep_w_in(w):
    d = w.shape[0]
    pw = lw = d
    o = np.cumsum([0, pw, lw, lw, Q_LORA, KV_LORA, QK_ROPE, 3 * d])
    pool, lx, lg, cq, ckv, kr, gt = (w[:, o[i]:o[i + 1]] for i in range(7))
    cols = [pool, lx, lg, gt, cq, ckv, kr, kr[:, _swap_perm()]]
    n = sum(c.shape[1] for c in cols)
    cols.append(jnp.zeros((d, -n % INPROJ_TN), w.dtype))
    return jnp.concatenate(cols, axis=1).astype(BF)


def _prep_w_uq(w):
    qk = QK_NOPE + QK_ROPE
    w = w.reshape(w.shape[0], HEADS, qk)
    rope = w[:, :, QK_NOPE:]
    return jnp.concatenate([w, rope[:, :, _swap_perm()]], axis=2).reshape(w.shape[0], HEADS * HEAD_W).T.astype(BF)


def kernel(x, c, ctx, c_ctx, mod_w, mod_b, pre_mix_g, post_mix_g, pre_ffn_g, post_ffn_g, w_in, pool_w, pool_scale,
           pool_proj, conv_w, conv_b, gate_a_w, gate_a_b, gate_x_w, gate_x_b, lru_lambda, lru_proj, q_norm_g, w_uq,
           kv_norm_g, w_ukv, mla_proj, w_out, ffn_w1, ffn_w3, ffn_w2, router_w, moe_w1, moe_w3, moe_w2):
    bsz, seq, d = x.shape
    ctx_len = ctx.shape[1]
    depth = mod_w.shape[0]
    assert ctx_len == SEQ_TILE and seq % SEQ_TILE == 0 and seq % GRID_W == 0
    n_lat = bsz * seq
    n_ctx = bsz * ctx_len
    n_all = n_lat + n_ctx
    tm = min(1024, seq, n_ctx)
    assert seq % tm == 0 and n_ctx % tm == 0
    tm_merge = tm // 2

    def mod_idx_for(rows):
        return lambda i: jnp.where(i < n_lat // rows, i // (seq // rows), bsz)

    mod_idx = mod_idx_for(tm)

    assert bsz + 1 <= SUBLANE
    cc = jnp.concatenate([c, c_ctx[None, :], jnp.zeros((SUBLANE - bsz - 1, d), F32)], axis=0)
    mods = _mod_call(cc, mod_w, mod_b)[:, :bsz + 1].reshape(depth, bsz + 1, 6, d)
    cos_t, sin_t = _rope_tables(seq, ctx_len)
    x_all, x_ctx = x.reshape(n_lat, d), ctx.reshape(n_ctx, d)
    row1 = lambda v: v.reshape(1, -1)

    for l in range(depth):
        last = l == depth - 1
        n_out = n_lat if last else n_all
        mod = mods[l]
        z = _inproj_call(x_all, x_ctx, mod, row1(pre_mix_g[l]), _prep_w_in(w_in[l]), tm, mod_idx)
        wkv = w_ukv[l].reshape(KV_LORA, HEADS, QK_NOPE + V_DIM)
        wk = wkv[:, :, :QK_NOPE].reshape(KV_LORA, HEADS * QK_NOPE).astype(BF)
        wvt = wkv[:, :, QK_NOPE:].reshape(KV_LORA, HEADS * V_DIM).T.astype(BF)
        q, k, v = _qkv_call(z, cos_t, sin_t, row1(q_norm_g[l]), row1(kv_norm_g[l]), _prep_w_uq(w_uq[l]),
                            wk, wvt, bsz, seq, ctx_len, 6 * d)
        tq = min(1024, seq)
        o = _attn_call(q, k, v, bsz=bsz, row_blk0=0, nq=seq // tq, tq=tq, kblk=0, klen=seq + ctx_len)
        o_ctx = None
        if not last:
            o_ctx = _attn_call(q, k, v, bsz=bsz, row_blk0=n_lat // ctx_len, nq=1, tq=ctx_len,
                               kblk=seq // ctx_len, klen=ctx_len)
            if x_ctx is None:
                o, o_ctx = jnp.concatenate([o, o_ctx], axis=0), None
        mp = _pool_call(z, pool_w[l].astype(BF), row1(pool_scale[l]), bsz, seq, ctx_len, n_out)
        hs = []
        for dr in range(2):
            hs.append(_lru_call(z, conv_w[l], row1(conv_b[l]), gate_a_w[l, dr].astype(BF), row1(gate_a_b[l, dr]),
                                gate_x_w[l, dr].astype(BF), row1(gate_x_b[l, dr]), row1(lru_lambda[l, dr]),
                                bsz, seq, dr == 1))
        moe_layer = l % 2 == 1
        rw = None
        if moe_layer:
            rw = router_w[l // 2].T
        outs = _merge_call(x_all, x_ctx if o_ctx is not None else None, mod, mp, hs[0], hs[1], z, o, o_ctx,
                           pool_proj[l].astype(BF), lru_proj[l].astype(BF), mla_proj[l].astype(BF),
                           w_out[l].astype(BF), row1(post_mix_g[l]), row1(pre_ffn_g[l]),
                           rw, tm_merge, n_out, mod_idx_for(tm_merge))
        x_ctx = None
        if not moe_layer:
            x1, h2 = outs
            x_all = _ffn_call(h2, x1, mod, row1(post_ffn_g[l]), ffn_w1[l // 2], ffn_w3[l // 2], ffn_w2[l // 2],
                              tm, mod_idx)
        else:
            x1, h2, rt, rtt = outs
            tme = min(1024, n_out * TOP_K // N_EXPERTS)
            dest, src_tok, blk_e, nused = _route_plan(rtt, tme)
            xs = _sc_gather_rows(h2, src_tok)
            y = _moe_call(blk_e, nused, xs, moe_w1[l // 2], moe_w3[l // 2], moe_w2[l // 2], tme)
            ya = _sc_gather_rows(y, dest[:n_out])
            yb = _sc_gather_rows(y, dest[n_out:])
            x_all = _combine_call(ya, yb, rt, x1, mod, row1(post_ffn_g[l]), tm, mod_idx)
    return x_all[:n_lat].reshape(bsz, seq, d)
```

```python
import functools
import math

import numpy as np
import jax
import jax.numpy as jnp
from jax import lax
from jax.experimental import pallas as pl
from jax.experimental.pallas import tpu as pltpu
from jax.experimental.pallas import tpu_sc as plsc

BF = jnp.bfloat16
F32 = jnp.float32

RMS_EPS = 1e-6
GRID_W = 64
POOL_WINDOWS = (2, 4, 8, 16)
LRU_BLOCKS = 8
CONV_W = 4
LRU_C = 8.0
HEADS = 8
Q_LORA = 384
KV_LORA = 256
QK_NOPE = 128
QK_ROPE = 64
V_DIM = 128
MLA_SCALE = (QK_NOPE + QK_ROPE) ** -0.5
Q_SCALE = MLA_SCALE * math.log2(math.e)
ROPE_FREQS = QK_ROPE // 4
ROPE_THETA = 10000.0
N_EXPERTS = 8
TOP_K = 2

LANE = 128
SUBLANE = 8
HALO = 2 * SUBLANE
SEQ_TILE = 256
HEAD_W = 256
VT_ROWS = V_DIM + HALO
INPROJ_TN = 1024
FFN_TF = 1024
V7X_VMEM_BYTES = 64 * 1024 * 1024
VMEM_LIMIT = V7X_VMEM_BYTES * 3 // 4


def _cparams(*sem):
    return pltpu.CompilerParams(dimension_semantics=sem, vmem_limit_bytes=VMEM_LIMIT)


def _rms(x, g):
    ms = jnp.mean(x * x, axis=-1, keepdims=True)
    return x * lax.rsqrt(ms + RMS_EPS) * g


def _sigmoid(x):
    return 0.5 * jnp.tanh(0.5 * x) + 0.5


def _silu(x):
    return x * _sigmoid(x)


def _gelu_tanh(x):
    return 0.5 * x * (1.0 + jnp.tanh(math.sqrt(2.0 / math.pi) * (x + 0.044715 * (x * x * x))))


def _dot(a, b):
    return jnp.dot(a, b, preferred_element_type=F32)


def _pack_rows(x):
    w = x.shape[1] // 2
    lo = lax.bitcast_convert_type(x[:, :w].astype(BF).astype(F32), jnp.uint32)
    hi = lax.bitcast_convert_type(x[:, w:].astype(BF).astype(F32), jnp.uint32)
    return (lo >> 16) | hi


def _unpack_rows(u):
    lo = lax.bitcast_convert_type(u << 16, F32)
    hi = lax.bitcast_convert_type(u & jnp.uint32(0xFFFF0000), F32)
    return lo, hi


def _mod_kernel(c_ref, w_ref, b_ref, o_ref):
    s = _silu(c_ref[...])
    o_ref[...] = _dot(s.astype(BF), w_ref[...].astype(BF)) + b_ref[...]


def _mod_call(cc, mod_w, mod_b):
    depth, d, n6 = mod_w.shape
    tn = 1536
    return pl.pallas_call(
        _mod_kernel,
        grid=(depth, n6 // tn),
        in_specs=[pl.BlockSpec((SUBLANE, d), lambda l, j: (0, 0)),
                  pl.BlockSpec((None, d, tn), lambda l, j: (l, 0, j)),
                  pl.BlockSpec((None, 1, tn), lambda l, j: (l, 0, j))],
        out_specs=pl.BlockSpec((None, SUBLANE, tn), lambda l, j: (l, 0, j)),
        out_shape=jax.ShapeDtypeStruct((depth, SUBLANE, n6), F32),
        compiler_params=_cparams("parallel", "arbitrary"),
        name="mod",
    )(cc, mod_w, mod_b.reshape(depth, 1, n6))


def _row_pair_specs(tm, width, n_lat_rows):
    nb = n_lat_rows // tm
    lat = pl.BlockSpec((tm, width), lambda i, *_: (jnp.minimum(i, nb - 1), 0))
    ctx = pl.BlockSpec((tm, width), lambda i, *_: (jnp.maximum(i - nb, 0), 0))
    return lat, ctx


def _pick(is_ctx, lat_ref, ctx_ref, rs=slice(None)):
    if ctx_ref is None:
        return lat_ref[rs, :]
    return jnp.where(is_ctx, ctx_ref[rs, :], lat_ref[rs, :])


def _inproj_kernel(x_ref, *rest, n_lat_blocks):
    xc_ref = rest[0] if len(rest) == 6 else None
    mod_ref, g_ref, w_ref, z_ref, h_scr = rest[-5:]

    @pl.when(pl.program_id(1) == 0)
    def _():
        h = _rms(_pick(pl.program_id(0) >= n_lat_blocks, x_ref, xc_ref), g_ref[...])
        h = h * (1.0 + mod_ref[1:2, :]) + mod_ref[0:1, :]
        h_scr[...] = h.astype(BF)

    z_ref[...] = _dot(h_scr[...], w_ref[...]).astype(BF)


def _inproj_call(x, x_ctx, mod, g, w, tm, mod_idx):
    d = x.shape[1]
    m = x.shape[0] + (0 if x_ctx is None else x_ctx.shape[0])
    n = w.shape[1]
    tn = INPROJ_TN
    if x_ctx is None:
        x_specs, xs = [pl.BlockSpec((tm, d), lambda i, j: (i, 0))], [x]
    else:
        x_specs, xs = list(_row_pair_specs(tm, d, x.shape[0])), [x, x_ctx]
    return pl.pallas_call(
        functools.partial(_inproj_kernel, n_lat_blocks=x.shape[0] // tm),
        grid=(m // tm, n // tn),
        in_specs=x_specs + [pl.BlockSpec((None, 6, d), lambda i, j: (mod_idx(i), 0, 0)),
                            pl.BlockSpec((1, d), lambda i, j: (0, 0)),
                            pl.BlockSpec((d, tn), lambda i, j: (0, j))],
        out_specs=pl.BlockSpec((tm, tn), lambda i, j: (i, j)),
        out_shape=jax.ShapeDtypeStruct((m, n), BF),
        scratch_shapes=[pltpu.VMEM((tm, d), BF)],
        compiler_params=_cparams("parallel", "arbitrary"),
        name="inproj",
    )(*xs, mod, g, w)


def _rope(x, cos, sin):
    return x * cos + pltpu.roll(x, LANE // 2, 1) * sin


_NT = (((1,), (1,)), ((), ()))


def _qkv_kernel(z_ref, cos_ref, sin_ref, cost_ref, sint_ref, qg_ref, kvg_ref, wuqt_ref, wk_ref, wvt_ref,
                qt_ref, k_ref, vt_ref):
    z = z_ref[...]
    cq = z[:, :Q_LORA].astype(F32)
    ckv = z[:, Q_LORA:Q_LORA + KV_LORA].astype(F32)
    kr = z[:, Q_LORA + KV_LORA:].astype(F32)
    cqn = _rms(cq, qg_ref[...]).astype(BF)
    ckvn = _rms(ckv, kvg_ref[...]).astype(BF)
    qt = lax.dot_general(wuqt_ref[...], cqn, _NT, preferred_element_type=F32)
    kn = _dot(ckvn, wk_ref[...])
    vt = lax.dot_general(wvt_ref[...], ckvn, _NT, preferred_element_type=F32)
    krot = _rope(kr, cos_ref[...], sin_ref[...]).astype(BF)
    cost = cost_ref[...]
    sint = sint_ref[...]
    for h in range(HEADS):
        c0 = h * HEAD_W
        c1 = c0 + QK_NOPE
        c2 = c1 + QK_ROPE
        qt_ref[c0:c1, :] = (qt[c0:c1, :] * Q_SCALE).astype(BF)
        qt_ref[c1:c2, :] = ((qt[c1:c2, :] * cost + qt[c2:c0 + HEAD_W, :] * sint) * Q_SCALE).astype(BF)
        qt_ref[c2:c0 + HEAD_W, :] = jnp.zeros((QK_ROPE, qt.shape[1]), BF)
        k_ref[h, :, 0:QK_NOPE] = kn[:, h * QK_NOPE:(h + 1) * QK_NOPE].astype(BF)
        k_ref[h, :, QK_NOPE:HEAD_W] = krot
        vt_ref[h, 0:V_DIM, :] = vt[h * V_DIM:(h + 1) * V_DIM, :].astype(BF)
        vt_ref[h, V_DIM:VT_ROWS, :] = jnp.ones((VT_ROWS - V_DIM, vt.shape[1]), BF)


def _qkv_call(z, cos_t, sin_t, qg, kvg, wuqt, wk, wvt, bsz, seq, ctx_len, z_off):
    m = z.shape[0]
    cos_tt = cos_t[:, :QK_ROPE].T
    sin_tt = sin_t[:, :QK_ROPE].T
    ts = SEQ_TILE
    nt = seq // ts
    nlat = bsz * nt
    lk = seq + ctx_len
    zw = Q_LORA + KV_LORA + LANE
    assert z_off % zw == 0
    zcol = z_off // zw

    def tab_idx(i):
        return (jnp.where(i < nlat, i % nt, nt), 0)

    def kv_idx(i):
        return (jnp.where(i < nlat, i // nt, i - nlat), 0, jnp.where(i < nlat, i % nt, nt), 0)

    def vt_idx(i):
        return (jnp.where(i < nlat, i // nt, i - nlat), 0, 0, jnp.where(i < nlat, i % nt, nt))

    return pl.pallas_call(
        _qkv_kernel,
        grid=(m // ts,),
        in_specs=[pl.BlockSpec((ts, zw), lambda i: (i, zcol)),
                  pl.BlockSpec((ts, LANE), tab_idx),
                  pl.BlockSpec((ts, LANE), tab_idx),
                  pl.BlockSpec((QK_ROPE, ts), lambda i: tab_idx(i)[::-1]),
                  pl.BlockSpec((QK_ROPE, ts), lambda i: tab_idx(i)[::-1]),
                  pl.BlockSpec((1, Q_LORA), lambda i: (0, 0)),
                  pl.BlockSpec((1, KV_LORA), lambda i: (0, 0)),
                  pl.BlockSpec(wuqt.shape, lambda i: (0, 0)),
                  pl.BlockSpec(wk.shape, lambda i: (0, 0)),
                  pl.BlockSpec(wvt.shape, lambda i: (0, 0))],
        out_specs=[pl.BlockSpec((HEADS * HEAD_W, ts), lambda i: (0, i)),
                   pl.BlockSpec((None, HEADS, ts, HEAD_W), kv_idx),
                   pl.BlockSpec((None, HEADS, VT_ROWS, ts), vt_idx)],
        out_shape=[jax.ShapeDtypeStruct((HEADS * HEAD_W, m), BF),
                   jax.ShapeDtypeStruct((bsz, HEADS, lk, HEAD_W), BF),
                   jax.ShapeDtypeStruct((bsz, HEADS, VT_ROWS, lk), BF)],
        compiler_params=_cparams("parallel"),
        name="qkv_up",
    )(z, cos_t, sin_t, cos_tt, sin_tt, qg, kvg, wuqt, wk, wvt)


def _col_reduce(x, pair, red):
    n = x.shape[0] // 4
    a = pair(pair(x[0:n], x[n:2 * n]), pair(x[2 * n:3 * n], x[3 * n:4 * n]))
    return red(a, axis=0, keepdims=True)


ATTN_LAG_LIMIT = 20.0


def _attn_kernel(qt_ref, k_ref, vt_ref, o_ref, p_scr, *, chunks, kp):
    qt = qt_ref[...]
    tq = qt.shape[1]

    nk = len(chunks)

    def scores(c):
        k0, kn = chunks[c]
        return _dot(k_ref[k0:k0 + kn, :], qt)

    def values(c):
        k0, kn = chunks[c]
        return _dot(vt_ref[:, k0:k0 + kn], p_scr[c % 2, 0:kn, :])

    def finish(a):
        o_ref[...] = (a[0:V_DIM] / a[V_DIM:V_DIM + 1]).T.astype(o_ref.dtype)

    s = scores(0)
    ref = _col_reduce(s, jnp.maximum, jnp.max)
    p_scr[0, 0:chunks[0][1], :] = jnp.exp2((s - ref).astype(BF))
    acc = alpha = None
    lag = jnp.zeros((1, tq), F32)
    for c in range(1, nk):
        s = scores(c)
        p_scr[c % 2, 0:chunks[c][1], :] = jnp.exp2((s - ref).astype(BF))
        mc = _col_reduce(s, jnp.maximum, jnp.max)
        pv = values(c - 1)
        acc = pv if acc is None else acc + pv
        if alpha is not None:
            acc = acc * alpha
        lag = jnp.maximum(lag, mc - ref)
        new_ref = jnp.maximum(ref, mc)
        alpha = jnp.exp2(ref - new_ref)
        ref = new_ref
    pv = values(nk - 1)
    acc = pv if acc is None else acc + pv
    over = jnp.max(lag) > ATTN_LAG_LIMIT

    @pl.when(jnp.logical_not(over))
    def _():
        finish(acc)

    @pl.when(over)
    def _():
        def body(j, carry):
            m, a = carry
            r0 = pl.multiple_of(j * kp, kp)
            sj = _dot(k_ref[pl.ds(r0, kp), :], qt)
            m_new = jnp.maximum(m, jnp.max(sj, axis=0, keepdims=True))
            pj = jnp.exp2((sj - m_new).astype(BF))
            a = jnp.exp2(m - m_new) * a + _dot(vt_ref[:, pl.ds(r0, kp)], pj)
            return m_new, a

        init = (jnp.full((1, tq), -1e30, F32), jnp.zeros((VT_ROWS, tq), F32))
        finish(lax.fori_loop(0, k_ref.shape[0] // kp, body, init)[1])


ATTN_FIRST = 256
ATTN_CHUNK = 512


def _attn_call(q, k, v, *, bsz, row_blk0, nq, tq, kblk, klen):
    rest = klen - ATTN_FIRST
    step = ATTN_CHUNK if rest % ATTN_CHUNK == 0 else ATTN_FIRST
    assert rest % step == 0
    chunks = ((0, ATTN_FIRST),) + tuple((ATTN_FIRST + i * step, step) for i in range(rest // step))
    tk = max(n for _, n in chunks)
    kern = functools.partial(_attn_kernel, chunks=chunks, kp=ATTN_FIRST)
    return pl.pallas_call(
        kern,
        grid=(bsz, HEADS, nq),
        in_specs=[pl.BlockSpec((HEAD_W, tq), lambda b, h, i: (h, row_blk0 + b * nq + i)),
                  pl.BlockSpec((None, None, klen, HEAD_W), lambda b, h, i: (b, h, kblk, 0)),
                  pl.BlockSpec((None, None, VT_ROWS, klen), lambda b, h, i: (b, h, 0, kblk))],
        out_specs=pl.BlockSpec((tq, V_DIM), lambda b, h, i: (b * nq + i, h)),
        out_shape=jax.ShapeDtypeStruct((bsz * nq * tq, HEADS * V_DIM), BF),
        scratch_shapes=[pltpu.VMEM((2, tk, tq), BF)],
        compiler_params=_cparams("parallel", "parallel", "arbitrary"),
        name="attn",
    )(q, k, v)


def _seq_flags(i, nlat, nt):
    is_ctx = i >= nlat
    t = jnp.where(is_ctx, 0, i % nt)
    first = jnp.logical_or(is_ctx, t == 0)
    last = jnp.logical_or(is_ctx, t == nt - 1)
    return is_ctx, t, first, last


def _pool_kernel(x_ref, xp_ref, xn_ref, pw_ref, ps_ref, o_ref, *, nlat, nt, seq, ctx_len):
    ts = x_ref.shape[0]
    is_ctx, t, first, last = _seq_flags(pl.program_id(0), nlat, nt)
    seq_len = jnp.where(is_ctx, ctx_len, seq)
    x = x_ref[...]
    xp = jnp.where(first, jnp.zeros_like(xp_ref[...]), xp_ref[...])
    xn = jnp.where(last, jnp.zeros_like(xn_ref[...]), xn_ref[...])
    xe = jnp.concatenate([xp, x, xn], axis=0)
    tpos = t * ts + lax.broadcasted_iota(jnp.int32, (ts, 1), 0)
    rel = (lax.broadcasted_iota(jnp.int32, (ts, ts + 2 * HALO), 1) - HALO
           - lax.broadcasted_iota(jnp.int32, (ts, ts + 2 * HALO), 0))
    gw = x.shape[1] // len(POOL_WINDOWS)
    cols = [slice(g * gw, (g + 1) * gw) for g in range(len(POOL_WINDOWS))]
    sums = []
    for cs, w in zip(cols, POOL_WINDOWS):
        band = jnp.where(rel >= -(w // 2), jnp.where(rel < w - w // 2, 1.0, 0.0), 0.0).astype(BF)
        sums.append(_dot(band, xe[:, cs]))
    for g, (cs, w) in enumerate(zip(cols, POOL_WINDOWS)):
        cnt = (jnp.minimum(tpos + (w - w // 2), seq_len) - jnp.maximum(tpos - w // 2, 0)).astype(F32)
        mean_minus = sums[g] / cnt - x[:, cs].astype(F32)
        o_ref[:, cs] = (_dot(mean_minus.astype(BF), pw_ref[g]) * ps_ref[:, cs]).astype(BF)


def _halo_specs(ts, width, col, row_blk, m):
    r = ts // HALO
    nh = m // HALO
    prev = pl.BlockSpec((HALO, width), lambda *a: (jnp.maximum(row_blk(*a) * r - 1, 0), col))
    nxt = pl.BlockSpec((HALO, width), lambda *a: (jnp.minimum((row_blk(*a) + 1) * r, nh - 1), col))
    return prev, nxt


def _pool_call(z, pool_w, pool_scale, bsz, seq, ctx_len, n_rows):
    m = z.shape[0]
    ts = SEQ_TILE
    nt = seq // ts
    nlat = bsz * nt
    width = pool_scale.shape[1]
    prev, nxt = _halo_specs(ts, width, 0, lambda i: i, m)
    kern = functools.partial(_pool_kernel, nlat=nlat, nt=nt, seq=seq, ctx_len=ctx_len)
    return pl.pallas_call(
        kern,
        grid=(n_rows // ts,),
        in_specs=[pl.BlockSpec((ts, width), lambda i: (i, 0)), prev, nxt,
                  pl.BlockSpec(pool_w.shape, lambda i: (0, 0, 0)),
                  pl.BlockSpec((1, width), lambda i: (0, 0))],
        out_specs=pl.BlockSpec((ts, width), lambda i: (i, 0)),
        out_shape=jax.ShapeDtypeStruct((n_rows, width), BF),
        compiler_params=_cparams("parallel"),
        name="pool",
    )(z, z, z, pool_w, pool_scale)


def _lru_kernel(x_ref, xp_ref, xn_ref, cw_ref, cb_ref, wa_ref, ba_ref, wx_ref, bx_ref, lam_ref, o_ref,
                xf_scr, hl_scr, ca_scr, ga_scr, gb_scr, hp_scr, h_scr, *, reverse, nt):
    ts, width = x_ref.shape
    gs = SUBLANE
    ng = ts // gs
    bw = width // LRU_BLOCKS
    assert LRU_BLOCKS == gs
    s = pl.program_id(1)
    t = (nt - s) if reverse else (s - 1)
    first = jnp.logical_or(s == 0, t == 0)
    last = jnp.logical_or(s == 0, t == nt - 1)

    @pl.when(s == 0)
    def _():
        h_scr[...] = jnp.zeros_like(h_scr)

    left = CONV_W // 2
    xp = xp_ref[...].astype(F32)[HALO - gs:HALO]
    xn = xn_ref[...].astype(F32)[0:gs]
    xp = jnp.where(first, jnp.zeros_like(xp), xp)
    xn = jnp.where(last, jnp.zeros_like(xn), xn)
    lam = lam_ref[...]
    neg = -lam
    softplus = jnp.maximum(neg, 0.0) + jnp.log1p(jnp.exp(-jnp.abs(neg)))
    coef = (-LRU_C * math.log2(math.e)) * softplus
    row = lax.broadcasted_iota(jnp.int32, (ng, bw), 0)
    slab = lambda n, j: (n, pl.ds(j, ng, stride=gs), slice(None))
    order = range(gs - 1, -1, -1) if reverse else range(gs)

    for n in range(LRU_BLOCKS):
        cs = slice(n * bw, (n + 1) * bw)
        xf_scr[n] = x_ref[:, cs].astype(F32)
        xs = {j: xf_scr[slab(n, j)] for j in range(gs)}
        for j in range(-left, 0):
            xs[j] = jnp.where(row == 0, xp[gs + j:gs + j + 1, cs], pltpu.roll(xs[gs + j], 1, 0))
        for j in range(gs, gs + CONV_W - 1 - left):
            xs[j] = jnp.where(row == ng - 1, xn[j - gs:j - gs + 1, cs], pltpu.roll(xs[j - gs], ng - 1, 0))
        us = []
        for j in range(gs):
            u = cb_ref[:, cs] + cw_ref[0:1, cs] * xs[j - left]
            for k in range(1, CONV_W):
                u = u + cw_ref[k:k + 1, cs] * xs[j - left + k]
            us.append(u)
        un = jnp.concatenate(us, axis=0)
        ub = un.astype(BF)
        r = _sigmoid(_dot(ub, wa_ref[n]) + ba_ref[:, cs])
        gi = _sigmoid(_dot(ub, wx_ref[n]) + bx_ref[:, cs])
        a = jnp.exp2(r * coef[:, cs])
        om = 1.0 - a * a
        b = (om * lax.rsqrt(jnp.maximum(om, 1e-30))) * (gi * un)
        hl = ca = None
        for j in order:
            aj = a[j * ng:(j + 1) * ng]
            bj = b[j * ng:(j + 1) * ng]
            hl, ca = (bj, aj) if hl is None else (aj * hl + bj, aj * ca)
            hl_scr[n, j * ng:(j + 1) * ng, :] = hl
            ca_scr[n, j * ng:(j + 1) * ng, :] = ca
        ga_scr[pl.ds(n, ng, stride=LRU_BLOCKS), :] = ca
        gb_scr[pl.ds(n, ng, stride=LRU_BLOCKS), :] = hl

    h = h_scr[...]
    for g in (range(ng - 1, -1, -1) if reverse else range(ng)):
        rows = slice(g * LRU_BLOCKS, (g + 1) * LRU_BLOCKS)
        hp_scr[rows, :] = h
        h = ga_scr[rows, :] * h + gb_scr[rows, :]
    h_scr[...] = h

    for n in range(LRU_BLOCKS):
        hp = hp_scr[pl.ds(n, ng, stride=LRU_BLOCKS), :]
        for j in range(gs):
            xf_scr[slab(n, j)] = hl_scr[n, j * ng:(j + 1) * ng, :] + ca_scr[n, j * ng:(j + 1) * ng, :] * hp
        o_ref[:, n * bw:(n + 1) * bw] = xf_scr[n].astype(o_ref.dtype)


def _lru_call(z, conv_w, conv_b, wa, ba, wx, bx, lam, bsz, seq, reverse):
    m = z.shape[0]
    ts = SEQ_TILE
    nt = seq // ts
    nlat = bsz * nt
    width = conv_b.shape[1]

    def row_blk(b, s):
        t = (nt - s) if reverse else (s - 1)
        return jnp.where(s == 0, nlat + b, b * nt + t)

    prev, nxt = _halo_specs(ts, width, 1, row_blk, m)
    vec = lambda shape: pl.BlockSpec(shape, lambda b, s: (0,) * len(shape))
    kern = functools.partial(_lru_kernel, reverse=reverse, nt=nt)
    return pl.pallas_call(
        kern,
        grid=(bsz, nt + 1),
        in_specs=[pl.BlockSpec((ts, width), lambda b, s: (row_blk(b, s), 1)), prev, nxt,
                  vec(conv_w.shape), vec(conv_b.shape), vec(wa.shape), vec(ba.shape),
                  vec(wx.shape), vec(bx.shape), vec(lam.shape)],
        out_specs=pl.BlockSpec((ts, width), lambda b, s: (row_blk(b, s), 0)),
        out_shape=jax.ShapeDtypeStruct((m, width), BF),
        scratch_shapes=[pltpu.VMEM((LRU_BLOCKS, ts, width // LRU_BLOCKS), F32)] * 3
        + [pltpu.VMEM((ts // SUBLANE * LRU_BLOCKS, width // LRU_BLOCKS), F32)] * 3
        + [pltpu.VMEM((LRU_BLOCKS, width // LRU_BLOCKS), F32)],
        compiler_params=_cparams("parallel", "arbitrary"),
        name="lru_bwd" if reverse else "lru_fwd",
    )(z, z, z, conv_w, conv_b, wa, ba, wx, bx, lam)


def _merge_kernel(*refs, route, n_lat_blocks):
    refs = list(refs)
    x_ref = refs.pop(0)
    xc_ref = refs.pop(0) if n_lat_blocks else None
    mod_ref, mp_ref, hf_ref, hb_ref, lg_ref, o_ref = (refs.pop(0) for _ in range(6))
    oc_ref = refs.pop(0) if n_lat_blocks else None
    gt_ref, wp_ref, wl_ref, wm_ref, wo_ref, g1_ref, g2_ref = (refs.pop(0) for _ in range(7))
    if route:
        rw_ref, x1_ref, h2_ref, rt_ref, rtt_ref = refs
    else:
        x1_ref, h2_ref = refs
    is_ctx = pl.program_id(0) >= n_lat_blocks if n_lat_blocks else None
    tm, d = x_ref.shape
    halves = [slice(0, tm // 2), slice(tm // 2, tm)]

    def branches(rs):
        y_pool = _dot(mp_ref[rs, :], wp_ref[...])
        lru_in = (hf_ref[rs, :].astype(F32) + hb_ref[rs, :].astype(F32)) * _gelu_tanh(lg_ref[rs, :].astype(F32))
        y_lru = _dot(lru_in.astype(BF), wl_ref[...])
        y_mla = _dot(_pick(is_ctx, o_ref, oc_ref, rs), wm_ref[...])
        return y_pool, y_lru, y_mla

    def mixed(rs, ys):
        mix = (_sigmoid(gt_ref[rs, 0:d].astype(F32)) * ys[0]
               + _sigmoid(gt_ref[rs, d:2 * d].astype(F32)) * ys[1]
               + _sigmoid(gt_ref[rs, 2 * d:3 * d].astype(F32)) * ys[2])
        return _dot(mix.astype(BF), wo_ref[...])

    ys = [branches(rs) for rs in halves]
    outs = [mixed(rs, y) for rs, y in zip(halves, ys)]
    for rs, y in zip(halves, outs):
        x1 = _pick(is_ctx, x_ref, xc_ref, rs) + mod_ref[2:3, :] * _rms(y, g1_ref[...])
        x1_ref[rs, :] = x1
        h2 = _rms(x1, g2_ref[...]) * (1.0 + mod_ref[4:5, :]) + mod_ref[3:4, :]
        if route:
            h2_ref[rs, :] = _pack_rows(h2)
            _route_rows(h2, rw_ref, rt_ref, rtt_ref, rs)
        else:
            h2_ref[rs, :] = h2.astype(h2_ref.dtype)


def _route_rows(h2, rw_ref, rt_ref, rtt_ref, rs):
    logit = [jnp.sum(h2 * rw_ref[e:e + 1, :], axis=1, keepdims=True) for e in range(N_EXPERTS)]
    v1, i1 = logit[0], jnp.zeros_like(logit[0])
    for e in range(1, N_EXPERTS):
        upd = logit[e] > v1
        v1 = jnp.where(upd, logit[e], v1)
        i1 = jnp.where(upd, float(e), i1)
    v2, i2 = jnp.full_like(v1, -jnp.inf), jnp.zeros_like(v1)
    for e in range(N_EXPERTS):
        cand = jnp.where(i1 == float(e), -jnp.inf, logit[e])
        upd = cand > v2
        v2 = jnp.where(upd, cand, v2)
        i2 = jnp.where(upd, float(e), i2)
    ex = jnp.exp(v2 - v1)
    gate1 = 1.0 / (1.0 + ex)
    gate2 = ex / (1.0 + ex)
    col = lax.broadcasted_iota(jnp.int32, (h2.shape[0], rt_ref.shape[1]), 1)
    table = jnp.where(col == 0, i1, jnp.where(col == 1, i2,
                      jnp.where(col == 2, gate1, jnp.where(col == 3, gate2, 0.0))))
    rt_ref[rs, :] = table
    rtt_ref[:, rs] = table.T[0:rtt_ref.shape[0], :]


def _merge_call(x, x_ctx, mod, mp, hf, hb, z, o, o_ctx, wp, wl, wm, wo, g1, g2, rw, tm, n_rows, mod_idx):
    d = x.shape[1]
    route = rw is not None
    pair = x_ctx is not None
    row = lambda c: pl.BlockSpec((tm, d), lambda i: (i, c))
    full = lambda a: pl.BlockSpec(a.shape, lambda i: (0,) * a.ndim)
    x_specs = list(_row_pair_specs(tm, d, x.shape[0])) if pair else [row(0)]
    o_specs = list(_row_pair_specs(tm, d, o.shape[0])) if pair else [row(0)]
    in_specs = (x_specs + [pl.BlockSpec((None, 6, d), lambda i: (mod_idx(i), 0, 0)), row(0), row(0), row(0), row(2)]
                + o_specs + [pl.BlockSpec((tm, 3 * d), lambda i: (i, 1)),
                             full(wp), full(wl), full(wm), full(wo), full(g1), full(g2)])
    args = ([x] + ([x_ctx] if pair else []) + [mod, mp, hf, hb, z, o] + ([o_ctx] if pair else [])
            + [z, wp, wl, wm, wo, g1, g2])
    out_specs = [row(0), row(0)]
    out_shape = [jax.ShapeDtypeStruct((n_rows, d), F32), jax.ShapeDtypeStruct((n_rows, d), BF)]
    if route:
        out_specs[1] = pl.BlockSpec((tm, d // 2), lambda i: (i, 0))
        out_shape[1] = jax.ShapeDtypeStruct((n_rows, d // 2), jnp.uint32)
        in_specs.append(full(rw))
        args.append(rw)
        out_specs.append(pl.BlockSpec((tm, LANE), lambda i: (i, 0)))
        out_shape.append(jax.ShapeDtypeStruct((n_rows, LANE), F32))
        out_specs.append(pl.BlockSpec((SUBLANE, tm), lambda i: (0, i)))
        out_shape.append(jax.ShapeDtypeStruct((SUBLANE, n_rows), F32))
    return pl.pallas_call(
        functools.partial(_merge_kernel, route=route, n_lat_blocks=x.shape[0] // tm if pair else 0),
        grid=(n_rows // tm,),
        in_specs=in_specs, out_specs=out_specs, out_shape=out_shape,
        compiler_params=_cparams("parallel"),
        name="merge",
    )(*args)


def _swiglu_step(x, w1, w3, w2):
    a = _dot(x, w1)
    b = _dot(x, w3)
    return _dot((_silu(a) * b).astype(BF), w2)


def _ffn_kernel(h_ref, x_ref, mod_ref, g_ref, w13_ref, w2_ref, o_ref):
    f = pl.program_id(1)
    tf = w2_ref.shape[0]

    @pl.when(f == 0)
    def _():
        o_ref[...] = jnp.zeros_like(o_ref)

    ab = _dot(h_ref[...], w13_ref[...])
    o_ref[...] += _dot((_silu(ab[:, :tf]) * ab[:, tf:]).astype(BF), w2_ref[...])

    @pl.when(f == pl.num_programs(1) - 1)
    def _():
        o_ref[...] = x_ref[...] + mod_ref[5:6, :] * _rms(o_ref[...], g_ref[...])


def _ffn_call(h2, x1, mod, g, w1, w3, w2, tm, mod_idx):
    m, d = x1.shape
    tf = FFN_TF
    pad = -w1.shape[1] % tf
    w1, w3 = (jnp.pad(w, ((0, 0), (0, pad))) for w in (w1, w3))
    ff = w1.shape[1]
    w13 = jnp.concatenate([w[:, f * tf:(f + 1) * tf].astype(BF) for f in range(ff // tf) for w in (w1, w3)], axis=1)
    w2 = jnp.pad(w2, ((0, pad), (0, 0))).astype(BF)
    return pl.pallas_call(
        _ffn_kernel,
        grid=(m // tm, ff // tf),
        in_specs=[pl.BlockSpec((tm, d), lambda i, f: (i, 0)),
                  pl.BlockSpec((tm, d), lambda i, f: (i, 0)),
                  pl.BlockSpec((None, 6, d), lambda i, f: (mod_idx(i), 0, 0)),
                  pl.BlockSpec((1, d), lambda i, f: (0, 0)),
                  pl.BlockSpec((d, 2 * tf), lambda i, f: (0, f)),
                  pl.BlockSpec((tf, d), lambda i, f: (f, 0))],
        out_specs=pl.BlockSpec((tm, d), lambda i, f: (i, 0)),
        out_shape=jax.ShapeDtypeStruct((m, d), F32),
        compiler_params=_cparams("parallel", "arbitrary"),
        name="ffn",
    )(h2, x1, mod, g, w13, w2)


def _moe_kernel(blk_e_ref, nused_ref, x_ref, w1_ref, w3_ref, w2_ref, o_ref, xb, acc):
    i = pl.program_id(0)
    f = pl.program_id(1)
    used = i < nused_ref[0]
    w = x_ref.shape[1]

    @pl.when(f == 0)
    def _():
        acc[...] = jnp.zeros_like(acc)
        lo, hi = _unpack_rows(x_ref[...])
        xb[:, :w] = lo.astype(BF)
        xb[:, w:] = hi.astype(BF)

    @pl.when(used)
    def _():
        acc[...] += _swiglu_step(xb[...], w1_ref[...].astype(BF), w3_ref[...].astype(BF), w2_ref[...].astype(BF))

    @pl.when(f == pl.num_programs(1) - 1)
    def _():
        o_ref[...] = _pack_rows(acc[...])


def _moe_call(blk_e, nused, xs, w1, w3, w2, tme):
    n_rows, dp = xs.shape
    d = w1.shape[1]
    ff = w1.shape[2]
    tf = 512
    nf = ff // tf

    def ftile(i, f, nu):
        return jnp.where(i < nu[0], f, nf - 1)

    return pl.pallas_call(
        _moe_kernel,
        grid_spec=pltpu.PrefetchScalarGridSpec(
            num_scalar_prefetch=2,
            grid=(n_rows // tme, nf),
            in_specs=[pl.BlockSpec((tme, dp), lambda i, f, be, nu: (i, 0)),
                      pl.BlockSpec((None, d, tf), lambda i, f, be, nu: (be[i], 0, ftile(i, f, nu))),
                      pl.BlockSpec((None, d, tf), lambda i, f, be, nu: (be[i], 0, ftile(i, f, nu))),
                      pl.BlockSpec((None, tf, d), lambda i, f, be, nu: (be[i], ftile(i, f, nu), 0))],
            out_specs=pl.BlockSpec((tme, dp), lambda i, f, be, nu: (i, 0)),
            scratch_shapes=[pltpu.VMEM((tme, d), BF), pltpu.VMEM((tme, d), F32)]),
        out_shape=jax.ShapeDtypeStruct((n_rows, dp), jnp.uint32),
        compiler_params=_cparams("arbitrary", "arbitrary"),
        name="moe",
    )(blk_e, nused, xs, w1, w3, w2)


SC_CORES = 2
SC_SUBCORES = 16
SC_GATHER_ROWS = 128


def _sc_gather_rows(table, idx):
    n_idx = idx.shape[0]
    d = table.shape[1]
    workers = SC_CORES * SC_SUBCORES
    per_w = n_idx // workers
    rows = min(SC_GATHER_ROWS, per_w)
    assert n_idx % (8 * workers) == 0 and per_w % rows == 0
    mesh = plsc.VectorSubcoreMesh(core_axis_name="c", subcore_axis_name="s",
                                  num_cores=SC_CORES, num_subcores=SC_SUBCORES)

    def body(table_hbm, idx_hbm, out_hbm, idx_v, rows_v, sem):
        wid = lax.axis_index("s") * SC_CORES + lax.axis_index("c")
        base = wid * per_w

        @pl.loop(0, per_w // rows)
        def _(it):
            off = pl.multiple_of(base + it * rows, 8)
            pltpu.sync_copy(idx_hbm.at[pl.ds(off, rows)], idx_v)
            pltpu.async_copy(table_hbm.at[idx_v], rows_v, sem).wait()
            pltpu.sync_copy(rows_v, out_hbm.at[pl.ds(off, rows)])

    return pl.kernel(
        body,
        out_type=jax.ShapeDtypeStruct((n_idx, d), table.dtype),
        mesh=mesh,
        scratch_types=[pltpu.VMEM((rows,), jnp.int32), pltpu.VMEM((rows, d), table.dtype),
                       pltpu.SemaphoreType.DMA],
        name="sc_gather",
    )(table, idx)


def _combine_kernel(ya_ref, yb_ref, rt_ref, x_ref, mod_ref, g_ref, o_ref):
    g1 = rt_ref[:, 2:3]
    g2 = rt_ref[:, 3:4]
    a_lo, a_hi = _unpack_rows(ya_ref[...])
    b_lo, b_hi = _unpack_rows(yb_ref[...])
    f = jnp.concatenate([g1 * a_lo + g2 * b_lo, g1 * a_hi + g2 * b_hi], axis=1)
    o_ref[...] = x_ref[...] + mod_ref[5:6, :] * _rms(f, g_ref[...])


def _combine_call(ya, yb, rt, x1, mod, g, tm, mod_idx):
    n, d = x1.shape
    return pl.pallas_call(
        _combine_kernel,
        grid=(n // tm,),
        in_specs=[pl.BlockSpec((tm, d // 2), lambda i: (i, 0)),
                  pl.BlockSpec((tm, d // 2), lambda i: (i, 0)),
                  pl.BlockSpec((tm, LANE), lambda i: (i, 0)),
                  pl.BlockSpec((tm, d), lambda i: (i, 0)),
                  pl.BlockSpec((None, 6, d), lambda i: (mod_idx(i), 0, 0)),
                  pl.BlockSpec((1, d), lambda i: (0, 0))],
        out_specs=pl.BlockSpec((tm, d), lambda i: (i, 0)),
        out_shape=jax.ShapeDtypeStruct((n, d), F32),
        compiler_params=_cparams("parallel"),
        name="combine",
    )(ya, yb, rt, x1, mod, g)


def _route_plan(rt, tme):
    n = rt.shape[1]
    e_flat = rt[:TOP_K].reshape(-1).astype(jnp.int32)
    onehot = (e_flat[:, None] == jnp.arange(N_EXPERTS, dtype=jnp.int32)[None, :]).astype(jnp.int32)
    csum = jnp.cumsum(onehot, axis=0)
    counts = csum[-1]
    padded = (counts + tme - 1) // tme * tme
    pad_end = jnp.cumsum(padded)
    pad_start = pad_end - padded
    dest = jnp.sum(onehot * (csum + pad_start[None, :]), axis=1) - 1
    n_rows = n * TOP_K + N_EXPERTS * tme
    n_blk = n_rows // tme
    src_tok = (jnp.arange(n_rows, dtype=jnp.int32) % n).at[dest].set(
        jnp.arange(n * TOP_K, dtype=jnp.int32) % n, unique_indices=True)
    blk_start = jnp.arange(n_blk, dtype=jnp.int32) * tme
    blk_e = jnp.minimum(jnp.sum((pad_end[None, :] <= blk_start[:, None]).astype(jnp.int32), axis=1), N_EXPERTS - 1)
    nused = (pad_end[-1] // tme).astype(jnp.int32).reshape(1)
    return dest, src_tok, blk_e, nused


def _swap_perm():
    j = np.arange(QK_ROPE)
    axis, half, f = j // (2 * ROPE_FREQS), (j % (2 * ROPE_FREQS)) // ROPE_FREQS, j % ROPE_FREQS
    return axis * 2 * ROPE_FREQS + (1 - half) * ROPE_FREQS + f


def _rope_tables(seq, ctx_len):
    rows = seq // GRID_W
    row = jnp.repeat(jnp.arange(rows, dtype=F32), GRID_W)
    col = (jnp.arange(rows * GRID_W) % GRID_W).astype(F32)
    inv = ROPE_THETA ** (-jnp.arange(ROPE_FREQS, dtype=F32) / ROPE_FREQS)
    ang = jnp.stack([row[:, None] * inv, col[:, None] * inv], axis=1)
    cos, sin = jnp.cos(ang), jnp.sin(ang)
    cos64 = jnp.stack([cos, cos], axis=2).reshape(seq, QK_ROPE)
    sin64 = jnp.stack([-sin, sin], axis=2).reshape(seq, QK_ROPE)
    pad = jnp.zeros((seq, LANE - QK_ROPE), F32)
    cos_t = jnp.concatenate([cos64, pad], axis=1)
    sin_t = jnp.concatenate([sin64, pad], axis=1)
    ident = jnp.concatenate([jnp.ones((ctx_len, QK_ROPE), F32), jnp.zeros((ctx_len, LANE - QK_ROPE), F32)], axis=1)
    return jnp.concatenate([cos_t, ident], axis=0), jnp.concatenate([sin_t, jnp.zeros((ctx_len, LANE), F32)], axis=0)


def _prep_w_in(w):
    d = w.shape[0]
    pw = lw = d
    o = np.cumsum([0, pw, lw, lw, Q_LORA, KV_LORA, QK_ROPE, 3 * d])
    pool, lx, lg, cq, ckv, kr, gt = (w[:, o[i]:o[i + 1]] for i in range(7))
    cols = [pool, lx, lg, gt, cq, ckv, kr, kr[:, _swap_perm()]]
    n = sum(c.shape[1] for c in cols)
    cols.append(jnp.zeros((d, -n % INPROJ_TN), w.dtype))
    return jnp.concatenate(cols, axis=1).astype(BF)


def _prep_w_uq(w):
    qk = QK_NOPE + QK_ROPE
    w = w.reshape(w.shape[0], HEADS, qk)
    rope = w[:, :, QK_NOPE:]
    return jnp.concatenate([w, rope[:, :, _swap_perm()]], axis=2).reshape(w.shape[0], HEADS * HEAD_W).T.astype(BF)


def kernel(x, c, ctx, c_ctx, mod_w, mod_b, pre_mix_g, post_mix_g, pre_ffn_g, post_ffn_g, w_in, pool_w, pool_scale,
           pool_proj, conv_w, conv_b, gate_a_w, gate_a_b, gate_x_w, gate_x_b, lru_lambda, lru_proj, q_norm_g, w_uq,
           kv_norm_g, w_ukv, mla_proj, w_out, ffn_w1, ffn_w3, ffn_w2, router_w, moe_w1, moe_w3, moe_w2):
    bsz, seq, d = x.shape
    ctx_len = ctx.shape[1]
    depth = mod_w.shape[0]
    assert ctx_len == SEQ_TILE and seq % SEQ_TILE == 0 and seq % GRID_W == 0
    n_lat = bsz * seq
    n_ctx = bsz * ctx_len
    n_all = n_lat + n_ctx
    tm = min(1024, seq, n_ctx)
    assert seq % tm == 0 and n_ctx % tm == 0
    tm_merge = tm // 2

    def mod_idx_for(rows):
        return lambda i: jnp.where(i < n_lat // rows, i // (seq // rows), bsz)

    mod_idx = mod_idx_for(tm)

    assert bsz + 1 <= SUBLANE
    cc = jnp.concatenate([c, c_ctx[None, :], jnp.zeros((SUBLANE - bsz - 1, d), F32)], axis=0)
    mods = _mod_call(cc, mod_w, mod_b)[:, :bsz + 1].reshape(depth, bsz + 1, 6, d)
    cos_t, sin_t = _rope_tables(seq, ctx_len)
    x_all, x_ctx = x.reshape(n_lat, d), ctx.reshape(n_ctx, d)
    row1 = lambda v: v.reshape(1, -1)

    for l in range(depth):
        last = l == depth - 1
        n_out = n_lat if last else n_all
        mod = mods[l]
        z = _inproj_call(x_all, x_ctx, mod, row1(pre_mix_g[l]), _prep_w_in(w_in[l]), tm, mod_idx)
        wkv = w_ukv[l].reshape(KV_LORA, HEADS, QK_NOPE + V_DIM)
        wk = wkv[:, :, :QK_NOPE].reshape(KV_LORA, HEADS * QK_NOPE).astype(BF)
        wvt = wkv[:, :, QK_NOPE:].reshape(KV_LORA, HEADS * V_DIM).T.astype(BF)
        q, k, v = _qkv_call(z, cos_t, sin_t, row1(q_norm_g[l]), row1(kv_norm_g[l]), _prep_w_uq(w_uq[l]),
                            wk, wvt, bsz, seq, ctx_len, 6 * d)
        tq = min(1024, seq)
        o = _attn_call(q, k, v, bsz=bsz, row_blk0=0, nq=seq // tq, tq=tq, kblk=0, klen=seq + ctx_len)
        o_ctx = None
        if not last:
            o_ctx = _attn_call(q, k, v, bsz=bsz, row_blk0=n_lat // ctx_len, nq=1, tq=ctx_len,
                               kblk=seq // ctx_len, klen=ctx_len)
            if x_ctx is None:
                o, o_ctx = jnp.concatenate([o, o_ctx], axis=0), None
        mp = _pool_call(z, pool_w[l].astype(BF), row1(pool_scale[l]), bsz, seq, ctx_len, n_out)
        hs = []
        for dr in range(2):
            hs.append(_lru_call(z, conv_w[l], row1(conv_b[l]), gate_a_w[l, dr].astype(BF), row1(gate_a_b[l, dr]),
                                gate_x_w[l, dr].astype(BF), row1(gate_x_b[l, dr]), row1(lru_lambda[l, dr]),
                                bsz, seq, dr == 1))
        moe_layer = l % 2 == 1
        rw = None
        if moe_layer:
            rw = router_w[l // 2].T
        outs = _merge_call(x_all, x_ctx if o_ctx is not None else None, mod, mp, hs[0], hs[1], z, o, o_ctx,
                           pool_proj[l].astype(BF), lru_proj[l].astype(BF), mla_proj[l].astype(BF),
                           w_out[l].astype(BF), row1(post_mix_g[l]), row1(pre_ffn_g[l]),
                           rw, tm_merge, n_out, mod_idx_for(tm_merge))
        x_ctx = None
        if not moe_layer:
            x1, h2 = outs
            x_all = _ffn_call(h2, x1, mod, row1(post_ffn_g[l]), ffn_w1[l // 2], ffn_w3[l // 2], ffn_w2[l // 2],
                              tm, mod_idx)
        else:
            x1, h2, rt, rtt = outs
            tme = min(1024, n_out * TOP_K // N_EXPERTS)
            dest, src_tok, blk_e, nused = _route_plan(rtt, tme)
            xs = _sc_gather_rows(h2, src_tok)
            y = _moe_call(blk_e, nused, xs, moe_w1[l // 2], moe_w3[l // 2], moe_w2[l // 2], tme)
            ya = _sc_gather_rows(y, dest[:n_out])
            yb = _sc_gather_rows(y, dest[n_out:])
            x_all = _combine_call(ya, yb, rt, x1, mod, row1(post_ffn_g[l]), tm, mod_idx)
    return x_all[:n_lat].reshape(bsz, seq, d)
```

```python
import functools
import math

import numpy as np
import jax
import jax.numpy as jnp
from jax import lax
from jax.experimental import pallas as pl
from jax.experimental.pallas import tpu as pltpu
from jax.experimental.pallas import tpu_sc as plsc

BF = jnp.bfloat16
F32 = jnp.float32

RMS_EPS = 1e-6
GRID_W = 64
POOL_WINDOWS = (2, 4, 8, 16)
LRU_BLOCKS = 8
CONV_W = 4
LRU_C = 8.0
HEADS = 8
Q_LORA = 384
KV_LORA = 256
QK_NOPE = 128
QK_ROPE = 64
V_DIM = 128
MLA_SCALE = (QK_NOPE + QK_ROPE) ** -0.5
Q_SCALE = MLA_SCALE * math.log2(math.e)
ROPE_FREQS = QK_ROPE // 4
ROPE_THETA = 10000.0
N_EXPERTS = 8
TOP_K = 2

LANE = 128
SUBLANE = 8
HALO = 2 * SUBLANE
SEQ_TILE = 256
HEAD_W = 256
VT_ROWS = V_DIM + HALO
INPROJ_TN = 1024
FFN_TF = 1024
V7X_VMEM_BYTES = 64 * 1024 * 1024
VMEM_LIMIT = V7X_VMEM_BYTES * 3 // 4


def _cparams(*sem):
    return pltpu.CompilerParams(dimension_semantics=sem, vmem_limit_bytes=VMEM_LIMIT)


def _rms(x, g):
    ms = jnp.mean(x * x, axis=-1, keepdims=True)
    return x * lax.rsqrt(ms + RMS_EPS) * g


def _sigmoid(x):
    return 0.5 * jnp.tanh(0.5 * x) + 0.5


def _silu(x):
    return x * _sigmoid(x)


def _gelu_tanh(x):
    return 0.5 * x * (1.0 + jnp.tanh(math.sqrt(2.0 / math.pi) * (x + 0.044715 * (x * x * x))))


def _dot(a, b):
    return jnp.dot(a, b, preferred_element_type=F32)


def _pack_rows(x):
    w = x.shape[1] // 2
    lo = lax.bitcast_convert_type(x[:, :w].astype(BF).astype(F32), jnp.uint32)
    hi = lax.bitcast_convert_type(x[:, w:].astype(BF).astype(F32), jnp.uint32)
    return (lo >> 16) | hi


def _unpack_rows(u):
    lo = lax.bitcast_convert_type(u << 16, F32)
    hi = lax.bitcast_convert_type(u & jnp.uint32(0xFFFF0000), F32)
    return lo, hi


def _mod_kernel(c_ref, w_ref, b_ref, o_ref):
    s = _silu(c_ref[...])
    o_ref[...] = _dot(s.astype(BF), w_ref[...].astype(BF)) + b_ref[...]


def _mod_call(cc, mod_w, mod_b):
    depth, d, n6 = mod_w.shape
    tn = 1536
    return pl.pallas_call(
        _mod_kernel,
        grid=(depth, n6 // tn),
        in_specs=[pl.BlockSpec((SUBLANE, d), lambda l, j: (0, 0)),
                  pl.BlockSpec((None, d, tn), lambda l, j: (l, 0, j)),
                  pl.BlockSpec((None, 1, tn), lambda l, j: (l, 0, j))],
        out_specs=pl.BlockSpec((None, SUBLANE, tn), lambda l, j: (l, 0, j)),
        out_shape=jax.ShapeDtypeStruct((depth, SUBLANE, n6), F32),
        compiler_params=_cparams("parallel", "arbitrary"),
        name="mod",
    )(cc, mod_w, mod_b.reshape(depth, 1, n6))


def _row_pair_specs(tm, width, n_lat_rows):
    nb = n_lat_rows // tm
    lat = pl.BlockSpec((tm, width), lambda i, *_: (jnp.minimum(i, nb - 1), 0))
    ctx = pl.BlockSpec((tm, width), lambda i, *_: (jnp.maximum(i - nb, 0), 0))
    return lat, ctx


def _pick(is_ctx, lat_ref, ctx_ref, rs=slice(None)):
    if ctx_ref is None:
        return lat_ref[rs, :]
    return jnp.where(is_ctx, ctx_ref[rs, :], lat_ref[rs, :])


def _inproj_kernel(x_ref, *rest, n_lat_blocks):
    xc_ref = rest[0] if len(rest) == 6 else None
    mod_ref, g_ref, w_ref, z_ref, h_scr = rest[-5:]

    @pl.when(pl.program_id(1) == 0)
    def _():
        h = _rms(_pick(pl.program_id(0) >= n_lat_blocks, x_ref, xc_ref), g_ref[...])
        h = h * (1.0 + mod_ref[1:2, :]) + mod_ref[0:1, :]
        h_scr[...] = h.astype(BF)

    z_ref[...] = _dot(h_scr[...], w_ref[...]).astype(BF)


def _inproj_call(x, x_ctx, mod, g, w, tm, mod_idx):
    d = x.shape[1]
    m = x.shape[0] + (0 if x_ctx is None else x_ctx.shape[0])
    n = w.shape[1]
    tn = INPROJ_TN
    if x_ctx is None:
        x_specs, xs = [pl.BlockSpec((tm, d), lambda i, j: (i, 0))], [x]
    else:
        x_specs, xs = list(_row_pair_specs(tm, d, x.shape[0])), [x, x_ctx]
    return pl.pallas_call(
        functools.partial(_inproj_kernel, n_lat_blocks=x.shape[0] // tm),
        grid=(m // tm, n // tn),
        in_specs=x_specs + [pl.BlockSpec((None, 6, d), lambda i, j: (mod_idx(i), 0, 0)),
                            pl.BlockSpec((1, d), lambda i, j: (0, 0)),
                            pl.BlockSpec((d, tn), lambda i, j: (0, j))],
        out_specs=pl.BlockSpec((tm, tn), lambda i, j: (i, j)),
        out_shape=jax.ShapeDtypeStruct((m, n), BF),
        scratch_shapes=[pltpu.VMEM((tm, d), BF)],
        compiler_params=_cparams("parallel", "arbitrary"),
        name="inproj",
    )(*xs, mod, g, w)


def _rope(x, cos, sin):
    return x * cos + pltpu.roll(x, LANE // 2, 1) * sin


_NT = (((1,), (1,)), ((), ()))


def _qkv_kernel(z_ref, cos_ref, sin_ref, cost_ref, sint_ref, qg_ref, kvg_ref, wuqt_ref, wk_ref, wvt_ref,
                qt_ref, k_ref, vt_ref):
    z = z_ref[...]
    cq = z[:, :Q_LORA].astype(F32)
    ckv = z[:, Q_LORA:Q_LORA + KV_LORA].astype(F32)
    kr = z[:, Q_LORA + KV_LORA:].astype(F32)
    cqn = _rms(cq, qg_ref[...]).astype(BF)
    ckvn = _rms(ckv, kvg_ref[...]).astype(BF)
    qt = lax.dot_general(wuqt_ref[...], cqn, _NT, preferred_element_type=F32)
    kn = _dot(ckvn, wk_ref[...])
    vt = lax.dot_general(wvt_ref[...], ckvn, _NT, preferred_element_type=F32)
    krot = _rope(kr, cos_ref[...], sin_ref[...]).astype(BF)
    cost = cost_ref[...]
    sint = sint_ref[...]
    for h in range(HEADS):
        c0 = h * HEAD_W
        c1 = c0 + QK_NOPE
        c2 = c1 + QK_ROPE
        qt_ref[c0:c1, :] = (qt[c0:c1, :] * Q_SCALE).astype(BF)
        qt_ref[c1:c2, :] = ((qt[c1:c2, :] * cost + qt[c2:c0 + HEAD_W, :] * sint) * Q_SCALE).astype(BF)
        qt_ref[c2:c0 + HEAD_W, :] = jnp.zeros((QK_ROPE, qt.shape[1]), BF)
        k_ref[h, :, 0:QK_NOPE] = kn[:, h * QK_NOPE:(h + 1) * QK_NOPE].astype(BF)
        k_ref[h, :, QK_NOPE:HEAD_W] = krot
        vt_ref[h, 0:V_DIM, :] = vt[h * V_DIM:(h + 1) * V_DIM, :].astype(BF)
        vt_ref[h, V_DIM:VT_ROWS, :] = jnp.ones((VT_ROWS - V_DIM, vt.shape[1]), BF)


def _qkv_call(z, cos_t, sin_t, qg, kvg, wuqt, wk, wvt, bsz, seq, ctx_len, z_off):
    m = z.shape[0]
    cos_tt = cos_t[:, :QK_ROPE].T
    sin_tt = sin_t[:, :QK_ROPE].T
    ts = SEQ_TILE
    nt = seq // ts
    nlat = bsz * nt
    lk = seq + ctx_len
    zw = Q_LORA + KV_LORA + LANE
    assert z_off % zw == 0
    zcol = z_off // zw

    def tab_idx(i):
        return (jnp.where(i < nlat, i % nt, nt), 0)

    def kv_idx(i):
        return (jnp.where(i < nlat, i // nt, i - nlat), 0, jnp.where(i < nlat, i % nt, nt), 0)

    def vt_idx(i):
        return (jnp.where(i < nlat, i // nt, i - nlat), 0, 0, jnp.where(i < nlat, i % nt, nt))

    return pl.pallas_call(
        _qkv_kernel,
        grid=(m // ts,),
        in_specs=[pl.BlockSpec((ts, zw), lambda i: (i, zcol)),
                  pl.BlockSpec((ts, LANE), tab_idx),
                  pl.BlockSpec((ts, LANE), tab_idx),
                  pl.BlockSpec((QK_ROPE, ts), lambda i: tab_idx(i)[::-1]),
                  pl.BlockSpec((QK_ROPE, ts), lambda i: tab_idx(i)[::-1]),
                  pl.BlockSpec((1, Q_LORA), lambda i: (0, 0)),
                  pl.BlockSpec((1, KV_LORA), lambda i: (0, 0)),
                  pl.BlockSpec(wuqt.shape, lambda i: (0, 0)),
                  pl.BlockSpec(wk.shape, lambda i: (0, 0)),
                  pl.BlockSpec(wvt.shape, lambda i: (0, 0))],
        out_specs=[pl.BlockSpec((HEADS * HEAD_W, ts), lambda i: (0, i)),
                   pl.BlockSpec((None, HEADS, ts, HEAD_W), kv_idx),
                   pl.BlockSpec((None, HEADS, VT_ROWS, ts), vt_idx)],
        out_shape=[jax.ShapeDtypeStruct((HEADS * HEAD_W, m), BF),
                   jax.ShapeDtypeStruct((bsz, HEADS, lk, HEAD_W), BF),
                   jax.ShapeDtypeStruct((bsz, HEADS, VT_ROWS, lk), BF)],
        compiler_params=_cparams("parallel"),
        name="qkv_up",
    )(z, cos_t, sin_t, cos_tt, sin_tt, qg, kvg, wuqt, wk, wvt)


def _col_reduce(x, pair, red):
    n = x.shape[0] // 4
    a = pair(pair(x[0:n], x[n:2 * n]), pair(x[2 * n:3 * n], x[3 * n:4 * n]))
    return red(a, axis=0, keepdims=True)


ATTN_LAG_LIMIT = 20.0


def _attn_kernel(qt_ref, k_ref, vt_ref, o_ref, p_scr, *, chunks, kp):
    qt = qt_ref[...]
    tq = qt.shape[1]

    nk = len(chunks)

    def scores(c):
        k0, kn = chunks[c]
        return _dot(k_ref[k0:k0 + kn, :], qt)

    def values(c):
        k0, kn = chunks[c]
        return _dot(vt_ref[:, k0:k0 + kn], p_scr[c % 2, 0:kn, :])

    def finish(a):
        o_ref[...] = (a[0:V_DIM] / a[V_DIM:V_DIM + 1]).T.astype(o_ref.dtype)

    s = scores(0)
    ref = _col_reduce(s, jnp.maximum, jnp.max)
    p_scr[0, 0:chunks[0][1], :] = jnp.exp2((s - ref).astype(BF))
    acc = alpha = None
    lag = jnp.zeros((1, tq), F32)
    for c in range(1, nk):
        s = scores(c)
        p_scr[c % 2, 0:chunks[c][1], :] = jnp.exp2((s - ref).astype(BF))
        mc = _col_reduce(s, jnp.maximum, jnp.max)
        pv = values(c - 1)
        acc = pv if acc is None else acc + pv
        if alpha is not None:
            acc = acc * alpha
        lag = jnp.maximum(lag, mc - ref)
        new_ref = jnp.maximum(ref, mc)
        alpha = jnp.exp2(ref - new_ref)
        ref = new_ref
    pv = values(nk - 1)
    acc = pv if acc is None else acc + pv
    over = jnp.max(lag) > ATTN_LAG_LIMIT

    @pl.when(jnp.logical_not(over))
    def _():
        finish(acc)

    @pl.when(over)
    def _():
        def body(j, carry):
            m, a = carry
            r0 = pl.multiple_of(j * kp, kp)
            sj = _dot(k_ref[pl.ds(r0, kp), :], qt)
            m_new = jnp.maximum(m, jnp.max(sj, axis=0, keepdims=True))
            pj = jnp.exp2((sj - m_new).astype(BF))
            a = jnp.exp2(m - m_new) * a + _dot(vt_ref[:, pl.ds(r0, kp)], pj)
            return m_new, a

        init = (jnp.full((1, tq), -1e30, F32), jnp.zeros((VT_ROWS, tq), F32))
        finish(lax.fori_loop(0, k_ref.shape[0] // kp, body, init)[1])


ATTN_FIRST = 256
ATTN_CHUNK = 512


def _attn_call(q, k, v, *, bsz, row_blk0, nq, tq, kblk, klen):
    rest = klen - ATTN_FIRST
    step = ATTN_CHUNK if rest % ATTN_CHUNK == 0 else ATTN_FIRST
    assert rest % step == 0
    chunks = ((0, ATTN_FIRST),) + tuple((ATTN_FIRST + i * step, step) for i in range(rest // step))
    tk = max(n for _, n in chunks)
    kern = functools.partial(_attn_kernel, chunks=chunks, kp=ATTN_FIRST)
    return pl.pallas_call(
        kern,
        grid=(bsz, HEADS, nq),
        in_specs=[pl.BlockSpec((HEAD_W, tq), lambda b, h, i: (h, row_blk0 + b * nq + i)),
                  pl.BlockSpec((None, None, klen, HEAD_W), lambda b, h, i: (b, h, kblk, 0)),
                  pl.BlockSpec((None, None, VT_ROWS, klen), lambda b, h, i: (b, h, 0, kblk))],
        out_specs=pl.BlockSpec((tq, V_DIM), lambda b, h, i: (b * nq + i, h)),
        out_shape=jax.ShapeDtypeStruct((bsz * nq * tq, HEADS * V_DIM), BF),
        scratch_shapes=[pltpu.VMEM((2, tk, tq), BF)],
        compiler_params=_cparams("parallel", "parallel", "arbitrary"),
        name="attn",
    )(q, k, v)


def _seq_flags(i, nlat, nt):
    is_ctx = i >= nlat
    t = jnp.where(is_ctx, 0, i % nt)
    first = jnp.logical_or(is_ctx, t == 0)
    last = jnp.logical_or(is_ctx, t == nt - 1)
    return is_ctx, t, first, last


def _pool_kernel(x_ref, xp_ref, xn_ref, pw_ref, ps_ref, o_ref, *, nlat, nt, seq, ctx_len):
    ts = x_ref.shape[0]
    is_ctx, t, first, last = _seq_flags(pl.program_id(0), nlat, nt)
    seq_len = jnp.where(is_ctx, ctx_len, seq)
    x = x_ref[...]
    xp = jnp.where(first, jnp.zeros_like(xp_ref[...]), xp_ref[...])
    xn = jnp.where(last, jnp.zeros_like(xn_ref[...]), xn_ref[...])
    xe = jnp.concatenate([xp, x, xn], axis=0)
    tpos = t * ts + lax.broadcasted_iota(jnp.int32, (ts, 1), 0)
    rel = (lax.broadcasted_iota(jnp.int32, (ts, ts + 2 * HALO), 1) - HALO
           - lax.broadcasted_iota(jnp.int32, (ts, ts + 2 * HALO), 0))
    gw = x.shape[1] // len(POOL_WINDOWS)
    cols = [slice(g * gw, (g + 1) * gw) for g in range(len(POOL_WINDOWS))]
    sums = []
    for cs, w in zip(cols, POOL_WINDOWS):
        band = jnp.where(rel >= -(w // 2), jnp.where(rel < w - w // 2, 1.0, 0.0), 0.0).astype(BF)
        sums.append(_dot(band, xe[:, cs]))
    for g, (cs, w) in enumerate(zip(cols, POOL_WINDOWS)):
        cnt = (jnp.minimum(tpos + (w - w // 2), seq_len) - jnp.maximum(tpos - w // 2, 0)).astype(F32)
        mean_minus = sums[g] / cnt - x[:, cs].astype(F32)
        o_ref[:, cs] = (_dot(mean_minus.astype(BF), pw_ref[g]) * ps_ref[:, cs]).astype(BF)


def _halo_specs(ts, width, col, row_blk, m):
    r = ts // HALO
    nh = m // HALO
    prev = pl.BlockSpec((HALO, width), lambda *a: (jnp.maximum(row_blk(*a) * r - 1, 0), col))
    nxt = pl.BlockSpec((HALO, width), lambda *a: (jnp.minimum((row_blk(*a) + 1) * r, nh - 1), col))
    return prev, nxt


def _pool_call(z, pool_w, pool_scale, bsz, seq, ctx_len, n_rows):
    m = z.shape[0]
    ts = SEQ_TILE
    nt = seq // ts
    nlat = bsz * nt
    width = pool_scale.shape[1]
    prev, nxt = _halo_specs(ts, width, 0, lambda i: i, m)
    kern = functools.partial(_pool_kernel, nlat=nlat, nt=nt, seq=seq, ctx_len=ctx_len)
    return pl.pallas_call(
        kern,
        grid=(n_rows // ts,),
        in_specs=[pl.BlockSpec((ts, width), lambda i: (i, 0)), prev, nxt,
                  pl.BlockSpec(pool_w.shape, lambda i: (0, 0, 0)),
                  pl.BlockSpec((1, width), lambda i: (0, 0))],
        out_specs=pl.BlockSpec((ts, width), lambda i: (i, 0)),
        out_shape=jax.ShapeDtypeStruct((n_rows, width), BF),
        compiler_params=_cparams("parallel"),
        name="pool",
    )(z, z, z, pool_w, pool_scale)


def _lru_kernel(x_ref, xp_ref, xn_ref, cw_ref, cb_ref, wa_ref, ba_ref, wx_ref, bx_ref, lam_ref, o_ref,
                xf_scr, hl_scr, ca_scr, ga_scr, gb_scr, hp_scr, h_scr, *, reverse, nt):
    ts, width = x_ref.shape
    gs = SUBLANE
    ng = ts // gs
    bw = width // LRU_BLOCKS
    assert LRU_BLOCKS == gs
    s = pl.program_id(1)
    t = (nt - s) if reverse else (s - 1)
    first = jnp.logical_or(s == 0, t == 0)
    last = jnp.logical_or(s == 0, t == nt - 1)

    @pl.when(s == 0)
    def _():
        h_scr[...] = jnp.zeros_like(h_scr)

    left = CONV_W // 2
    xp = xp_ref[...].astype(F32)[HALO - gs:HALO]
    xn = xn_ref[...].astype(F32)[0:gs]
    xp = jnp.where(first, jnp.zeros_like(xp), xp)
    xn = jnp.where(last, jnp.zeros_like(xn), xn)
    lam = lam_ref[...]
    neg = -lam
    softplus = jnp.maximum(neg, 0.0) + jnp.log1p(jnp.exp(-jnp.abs(neg)))
    coef = (-LRU_C * math.log2(math.e)) * softplus
    row = lax.broadcasted_iota(jnp.int32, (ng, bw), 0)
    slab = lambda n, j: (n, pl.ds(j, ng, stride=gs), slice(None))
    order = range(gs - 1, -1, -1) if reverse else range(gs)

    for n in range(LRU_BLOCKS):
        cs = slice(n * bw, (n + 1) * bw)
        xf_scr[n] = x_ref[:, cs].astype(F32)
        xs = {j: xf_scr[slab(n, j)] for j in range(gs)}
        for j in range(-left, 0):
            xs[j] = jnp.where(row == 0, xp[gs + j:gs + j + 1, cs], pltpu.roll(xs[gs + j], 1, 0))
        for j in range(gs, gs + CONV_W - 1 - left):
            xs[j] = jnp.where(row == ng - 1, xn[j - gs:j - gs + 1, cs], pltpu.roll(xs[j - gs], ng - 1, 0))
        us = []
        for j in range(gs):
            u = cb_ref[:, cs] + cw_ref[0:1, cs] * xs[j - left]
            for k in range(1, CONV_W):
                u = u + cw_ref[k:k + 1, cs] * xs[j - left + k]
            us.append(u)
        un = jnp.concatenate(us, axis=0)
        ub = un.astype(BF)
        r = _sigmoid(_dot(ub, wa_ref[n]) + ba_ref[:, cs])
        gi = _sigmoid(_dot(ub, wx_ref[n]) + bx_ref[:, cs])
        a = jnp.exp2(r * coef[:, cs])
        om = 1.0 - a * a
        b = (om * lax.rsqrt(jnp.maximum(om, 1e-30))) * (gi * un)
        hl = ca = None
        for j in order:
            aj = a[j * ng:(j + 1) * ng]
            bj = b[j * ng:(j + 1) * ng]
            hl, ca = (bj, aj) if hl is None else (aj * hl + bj, aj * ca)
            hl_scr[n, j * ng:(j + 1) * ng, :] = hl
            ca_scr[n, j * ng:(j + 1) * ng, :] = ca
        ga_scr[pl.ds(n, ng, stride=LRU_BLOCKS), :] = ca
        gb_scr[pl.ds(n, ng, stride=LRU_BLOCKS), :] = hl

    h = h_scr[...]
    for g in (range(ng - 1, -1, -1) if reverse else range(ng)):
        rows = slice(g * LRU_BLOCKS, (g + 1) * LRU_BLOCKS)
        hp_scr[rows, :] = h
        h = ga_scr[rows, :] * h + gb_scr[rows, :]
    h_scr[...] = h

    for n in range(LRU_BLOCKS):
        hp = hp_scr[pl.ds(n, ng, stride=LRU_BLOCKS), :]
        for j in range(gs):
            xf_scr[slab(n, j)] = hl_scr[n, j * ng:(j + 1) * ng, :] + ca_scr[n, j * ng:(j + 1) * ng, :] * hp
        o_ref[:, n * bw:(n + 1) * bw] = xf_scr[n].astype(o_ref.dtype)


def _lru_call(z, conv_w, conv_b, wa, ba, wx, bx, lam, bsz, seq, reverse):
    m = z.shape[0]
    ts = SEQ_TILE
    nt = seq // ts
    nlat = bsz * nt
    width = conv_b.shape[1]

    def row_blk(b, s):
        t = (nt - s) if reverse else (s - 1)
        return jnp.where(s == 0, nlat + b, b * nt + t)

    prev, nxt = _halo_specs(ts, width, 1, row_blk, m)
    vec = lambda shape: pl.BlockSpec(shape, lambda b, s: (0,) * len(shape))
    kern = functools.partial(_lru_kernel, reverse=reverse, nt=nt)
    return pl.pallas_call(
        kern,
        grid=(bsz, nt + 1),
        in_specs=[pl.BlockSpec((ts, width), lambda b, s: (row_blk(b, s), 1)), prev, nxt,
                  vec(conv_w.shape), vec(conv_b.shape), vec(wa.shape), vec(ba.shape),
                  vec(wx.shape), vec(bx.shape), vec(lam.shape)],
        out_specs=pl.BlockSpec((ts, width), lambda b, s: (row_blk(b, s), 0)),
        out_shape=jax.ShapeDtypeStruct((m, width), BF),
        scratch_shapes=[pltpu.VMEM((LRU_BLOCKS, ts, width // LRU_BLOCKS), F32)] * 3
        + [pltpu.VMEM((ts // SUBLANE * LRU_BLOCKS, width // LRU_BLOCKS), F32)] * 3
        + [pltpu.VMEM((LRU_BLOCKS, width // LRU_BLOCKS), F32)],
        compiler_params=_cparams("parallel", "arbitrary"),
        name="lru_bwd" if reverse else "lru_fwd",
    )(z, z, z, conv_w, conv_b, wa, ba, wx, bx, lam)


def _merge_kernel(*refs, route, n_lat_blocks):
    refs = list(refs)
    x_ref = refs.pop(0)
    xc_ref = refs.pop(0) if n_lat_blocks else None
    mod_ref, mp_ref, hf_ref, hb_ref, lg_ref, o_ref = (refs.pop(0) for _ in range(6))
    oc_ref = refs.pop(0) if n_lat_blocks else None
    gt_ref, wp_ref, wl_ref, wm_ref, wo_ref, g1_ref, g2_ref = (refs.pop(0) for _ in range(7))
    if route:
        rw_ref, x1_ref, h2_ref, rt_ref, rtt_ref = refs
    else:
        x1_ref, h2_ref = refs
    is_ctx = pl.program_id(0) >= n_lat_blocks if n_lat_blocks else None
    tm, d = x_ref.shape
    halves = [slice(0, tm // 2), slice(tm // 2, tm)]

    def branches(rs):
        y_pool = _dot(mp_ref[rs, :], wp_ref[...])
        lru_in = (hf_ref[rs, :].astype(F32) + hb_ref[rs, :].astype(F32)) * _gelu_tanh(lg_ref[rs, :].astype(F32))
        y_lru = _dot(lru_in.astype(BF), wl_ref[...])
        y_mla = _dot(_pick(is_ctx, o_ref, oc_ref, rs), wm_ref[...])
        return y_pool, y_lru, y_mla

    def mixed(rs, ys):
        mix = (_sigmoid(gt_ref[rs, 0:d].astype(F32)) * ys[0]
               + _sigmoid(gt_ref[rs, d:2 * d].astype(F32)) * ys[1]
               + _sigmoid(gt_ref[rs, 2 * d:3 * d].astype(F32)) * ys[2])
        return _dot(mix.astype(BF), wo_ref[...])

    ys = [branches(rs) for rs in halves]
    outs = [mixed(rs, y) for rs, y in zip(halves, ys)]
    for rs, y in zip(halves, outs):
        x1 = _pick(is_ctx, x_ref, xc_ref, rs) + mod_ref[2:3, :] * _rms(y, g1_ref[...])
        x1_ref[rs, :] = x1
        h2 = _rms(x1, g2_ref[...]) * (1.0 + mod_ref[4:5, :]) + mod_ref[3:4, :]
        if route:
            h2_ref[rs, :] = _pack_rows(h2)
            _route_rows(h2, rw_ref, rt_ref, rtt_ref, rs)
        else:
            h2_ref[rs, :] = h2.astype(h2_ref.dtype)


def _route_rows(h2, rw_ref, rt_ref, rtt_ref, rs):
    logit = [jnp.sum(h2 * rw_ref[e:e + 1, :], axis=1, keepdims=True) for e in range(N_EXPERTS)]
    v1, i1 = logit[0], jnp.zeros_like(logit[0])
    for e in range(1, N_EXPERTS):
        upd = logit[e] > v1
        v1 = jnp.where(upd, logit[e], v1)
        i1 = jnp.where(upd, float(e), i1)
    v2, i2 = jnp.full_like(v1, -jnp.inf), jnp.zeros_like(v1)
    for e in range(N_EXPERTS):
        cand = jnp.where(i1 == float(e), -jnp.inf, logit[e])
        upd = cand > v2
        v2 = jnp.where(upd, cand, v2)
        i2 = jnp.where(upd, float(e), i2)
    ex = jnp.exp(v2 - v1)
    gate1 = 1.0 / (1.0 + ex)
    gate2 = ex / (1.0 + ex)
    col = lax.broadcasted_iota(jnp.int32, (h2.shape[0], rt_ref.shape[1]), 1)
    table = jnp.where(col == 0, i1, jnp.where(col == 1, i2,
                      jnp.where(col == 2, gate1, jnp.where(col == 3, gate2, 0.0))))
    rt_ref[rs, :] = table
    rtt_ref[:, rs] = table.T[0:rtt_ref.shape[0], :]


def _merge_call(x, x_ctx, mod, mp, hf, hb, z, o, o_ctx, wp, wl, wm, wo, g1, g2, rw, tm, n_rows, mod_idx):
    d = x.shape[1]
    route = rw is not None
    pair = x_ctx is not None
    row = lambda c: pl.BlockSpec((tm, d), lambda i: (i, c))
    full = lambda a: pl.BlockSpec(a.shape, lambda i: (0,) * a.ndim)
    x_specs = list(_row_pair_specs(tm, d, x.shape[0])) if pair else [row(0)]
    o_specs = list(_row_pair_specs(tm, d, o.shape[0])) if pair else [row(0)]
    in_specs = (x_specs + [pl.BlockSpec((None, 6, d), lambda i: (mod_idx(i), 0, 0)), row(0), row(0), row(0), row(2)]
                + o_specs + [pl.BlockSpec((tm, 3 * d), lambda i: (i, 1)),
                             full(wp), full(wl), full(wm), full(wo), full(g1), full(g2)])
    args = ([x] + ([x_ctx] if pair else []) + [mod, mp, hf, hb, z, o] + ([o_ctx] if pair else [])
            + [z, wp, wl, wm, wo, g1, g2])
    out_specs = [row(0), row(0)]
    out_shape = [jax.ShapeDtypeStruct((n_rows, d), F32), jax.ShapeDtypeStruct((n_rows, d), BF)]
    if route:
        out_specs[1] = pl.BlockSpec((tm, d // 2), lambda i: (i, 0))
        out_shape[1] = jax.ShapeDtypeStruct((n_rows, d // 2), jnp.uint32)
        in_specs.append(full(rw))
        args.append(rw)
        out_specs.append(pl.BlockSpec((tm, LANE), lambda i: (i, 0)))
        out_shape.append(jax.ShapeDtypeStruct((n_rows, LANE), F32))
        out_specs.append(pl.BlockSpec((SUBLANE, tm), lambda i: (0, i)))
        out_shape.append(jax.ShapeDtypeStruct((SUBLANE, n_rows), F32))
    return pl.pallas_call(
        functools.partial(_merge_kernel, route=route, n_lat_blocks=x.shape[0] // tm if pair else 0),
        grid=(n_rows // tm,),
        in_specs=in_specs, out_specs=out_specs, out_shape=out_shape,
        compiler_params=_cparams("parallel"),
        name="merge",
    )(*args)


def _swiglu_step(x, w1, w3, w2):
    a = _dot(x, w1)
    b = _dot(x, w3)
    return _dot((_silu(a) * b).astype(BF), w2)


def _ffn_kernel(h_ref, x_ref, mod_ref, g_ref, w13_ref, w2_ref, o_ref):
    f = pl.program_id(1)
    tf = w2_ref.shape[0]

    @pl.when(f == 0)
    def _():
        o_ref[...] = jnp.zeros_like(o_ref)

    ab = _dot(h_ref[...], w13_ref[...])
    o_ref[...] += _dot((_silu(ab[:, :tf]) * ab[:, tf:]).astype(BF), w2_ref[...])

    @pl.when(f == pl.num_programs(1) - 1)
    def _():
        o_ref[...] = x_ref[...] + mod_ref[5:6, :] * _rms(o_ref[...], g_ref[...])


def _ffn_call(h2, x1, mod, g, w1, w3, w2, tm, mod_idx):
    m, d = x1.shape
    tf = FFN_TF
    pad = -w1.shape[1] % tf
    w1, w3 = (jnp.pad(w, ((0, 0), (0, pad))) for w in (w1, w3))
    ff = w1.shape[1]
    w13 = jnp.concatenate([w[:, f * tf:(f + 1) * tf].astype(BF) for f in range(ff // tf) for w in (w1, w3)], axis=1)
    w2 = jnp.pad(w2, ((0, pad), (0, 0))).astype(BF)
    return pl.pallas_call(
        _ffn_kernel,
        grid=(m // tm, ff // tf),
        in_specs=[pl.BlockSpec((tm, d), lambda i, f: (i, 0)),
                  pl.BlockSpec((tm, d), lambda i, f: (i, 0)),
                  pl.BlockSpec((None, 6, d), lambda i, f: (mod_idx(i), 0, 0)),
                  pl.BlockSpec((1, d), lambda i, f: (0, 0)),
                  pl.BlockSpec((d, 2 * tf), lambda i, f: (0, f)),
                  pl.BlockSpec((tf, d), lambda i, f: (f, 0))],
        out_specs=pl.BlockSpec((tm, d), lambda i, f: (i, 0)),
        out_shape=jax.ShapeDtypeStruct((m, d), F32),
        compiler_params=_cparams("parallel", "arbitrary"),
        name="ffn",
    )(h2, x1, mod, g, w13, w2)


def _moe_kernel(blk_e_ref, nused_ref, x_ref, w1_ref, w3_ref, w2_ref, o_ref, xb, acc):
    i = pl.program_id(0)
    f = pl.program_id(1)
    used = i < nused_ref[0]
    w = x_ref.shape[1]

    @pl.when(f == 0)
    def _():
        acc[...] = jnp.zeros_like(acc)
        lo, hi = _unpack_rows(x_ref[...])
        xb[:, :w] = lo.astype(BF)
        xb[:, w:] = hi.astype(BF)

    @pl.when(used)
    def _():
        acc[...] += _swiglu_step(xb[...], w1_ref[...].astype(BF), w3_ref[...].astype(BF), w2_ref[...].astype(BF))

    @pl.when(f == pl.num_programs(1) - 1)
    def _():
        o_ref[...] = _pack_rows(acc[...])


def _moe_call(blk_e, nused, xs, w1, w3, w2, tme):
    n_rows, dp = xs.shape
    d = w1.shape[1]
    ff = w1.shape[2]
    tf = 512
    nf = ff // tf

    def ftile(i, f, nu):
        return jnp.where(i < nu[0], f, nf - 1)

    return pl.pallas_call(
        _moe_kernel,
        grid_spec=pltpu.PrefetchScalarGridSpec(
            num_scalar_prefetch=2,
            grid=(n_rows // tme, nf),
            in_specs=[pl.BlockSpec((tme, dp), lambda i, f, be, nu: (i, 0)),
                      pl.BlockSpec((None, d, tf), lambda i, f, be, nu: (be[i], 0, ftile(i, f, nu))),
                      pl.BlockSpec((None, d, tf), lambda i, f, be, nu: (be[i], 0, ftile(i, f, nu))),
                      pl.BlockSpec((None, tf, d), lambda i, f, be, nu: (be[i], ftile(i, f, nu), 0))],
            out_specs=pl.BlockSpec((tme, dp), lambda i, f, be, nu: (i, 0)),
            scratch_shapes=[pltpu.VMEM((tme, d), BF), pltpu.VMEM((tme, d), F32)]),
        out_shape=jax.ShapeDtypeStruct((n_rows, dp), jnp.uint32),
        compiler_params=_cparams("arbitrary", "arbitrary"),
        name="moe",
    )(blk_e, nused, xs, w1, w3, w2)


SC_CORES = 2
SC_SUBCORES = 16
SC_GATHER_ROWS = 128


def _sc_gather_rows(table, idx):
    n_idx = idx.shape[0]
    d = table.shape[1]
    workers = SC_CORES * SC_SUBCORES
    per_w = n_idx // workers
    rows = min(SC_GATHER_ROWS, per_w)
    assert n_idx % (8 * workers) == 0 and per_w % rows == 0
    mesh = plsc.VectorSubcoreMesh(core_axis_name="c", subcore_axis_name="s",
                                  num_cores=SC_CORES, num_subcores=SC_SUBCORES)

    def body(table_hbm, idx_hbm, out_hbm, idx_v, rows_v, sem):
        wid = lax.axis_index("s") * SC_CORES + lax.axis_index("c")
        base = wid * per_w

        @pl.loop(0, per_w // rows)
        def _(it):
            off = pl.multiple_of(base + it * rows, 8)
            pltpu.sync_copy(idx_hbm.at[pl.ds(off, rows)], idx_v)
            pltpu.async_copy(table_hbm.at[idx_v], rows_v, sem).wait()
            pltpu.sync_copy(rows_v, out_hbm.at[pl.ds(off, rows)])

    return pl.kernel(
        body,
        out_type=jax.ShapeDtypeStruct((n_idx, d), table.dtype),
        mesh=mesh,
        scratch_types=[pltpu.VMEM((rows,), jnp.int32), pltpu.VMEM((rows, d), table.dtype),
                       pltpu.SemaphoreType.DMA],
        name="sc_gather",
    )(table, idx)


def _combine_kernel(ya_ref, yb_ref, rt_ref, x_ref, mod_ref, g_ref, o_ref):
    g1 = rt_ref[:, 2:3]
    g2 = rt_ref[:, 3:4]
    a_lo, a_hi = _unpack_rows(ya_ref[...])
    b_lo, b_hi = _unpack_rows(yb_ref[...])
    f = jnp.concatenate([g1 * a_lo + g2 * b_lo, g1 * a_hi + g2 * b_hi], axis=1)
    o_ref[...] = x_ref[...] + mod_ref[5:6, :] * _rms(f, g_ref[...])


def _combine_call(yab, rt, x1, mod, g, tm, mod_idx):
    n, d = x1.shape
    return pl.pallas_call(
        _combine_kernel,
        grid=(n // tm,),
        in_specs=[pl.BlockSpec((tm, d // 2), lambda i: (i, 0)),
                  pl.BlockSpec((tm, d // 2), lambda i: (i + n // tm, 0)),
                  pl.BlockSpec((tm, LANE), lambda i: (i, 0)),
                  pl.BlockSpec((tm, d), lambda i: (i, 0)),
                  pl.BlockSpec((None, 6, d), lambda i: (mod_idx(i), 0, 0)),
                  pl.BlockSpec((1, d), lambda i: (0, 0))],
        out_specs=pl.BlockSpec((tm, d), lambda i: (i, 0)),
        out_shape=jax.ShapeDtypeStruct((n, d), F32),
        compiler_params=_cparams("parallel"),
        name="combine",
    )(yab, yab, rt, x1, mod, g)


def _route_plan(rt, tme):
    n = rt.shape[1]
    e_flat = rt[:TOP_K].reshape(-1).astype(jnp.int32)
    onehot = (e_flat[:, None] == jnp.arange(N_EXPERTS, dtype=jnp.int32)[None, :]).astype(jnp.int32)
    csum = jnp.cumsum(onehot, axis=0)
    counts = csum[-1]
    padded = (counts + tme - 1) // tme * tme
    pad_end = jnp.cumsum(padded)
    pad_start = pad_end - padded
    dest = jnp.sum(onehot * (csum + pad_start[None, :]), axis=1) - 1
    n_rows = n * TOP_K + N_EXPERTS * tme
    n_blk = n_rows // tme
    src_tok = (jnp.arange(n_rows, dtype=jnp.int32) % n).at[dest].set(
        jnp.arange(n * TOP_K, dtype=jnp.int32) % n, unique_indices=True)
    blk_start = jnp.arange(n_blk, dtype=jnp.int32) * tme
    blk_e = jnp.minimum(jnp.sum((pad_end[None, :] <= blk_start[:, None]).astype(jnp.int32), axis=1), N_EXPERTS - 1)
    nused = (pad_end[-1] // tme).astype(jnp.int32).reshape(1)
    return dest, src_tok, blk_e, nused


def _swap_perm():
    j = np.arange(QK_ROPE)
    axis, half, f = j // (2 * ROPE_FREQS), (j % (2 * ROPE_FREQS)) // ROPE_FREQS, j % ROPE_FREQS
    return axis * 2 * ROPE_FREQS + (1 - half) * ROPE_FREQS + f


def _rope_tables(seq, ctx_len):
    rows = seq // GRID_W
    row = jnp.repeat(jnp.arange(rows, dtype=F32), GRID_W)
    col = (jnp.arange(rows * GRID_W) % GRID_W).astype(F32)
    inv = ROPE_THETA ** (-jnp.arange(ROPE_FREQS, dtype=F32) / ROPE_FREQS)
    ang = jnp.stack([row[:, None] * inv, col[:, None] * inv], axis=1)
    cos, sin = jnp.cos(ang), jnp.sin(ang)
    cos64 = jnp.stack([cos, cos], axis=2).reshape(seq, QK_ROPE)
    sin64 = jnp.stack([-sin, sin], axis=2).reshape(seq, QK_ROPE)
    pad = jnp.zeros((seq, LANE - QK_ROPE), F32)
    cos_t = jnp.concatenate([cos64, pad], axis=1)
    sin_t = jnp.concatenate([sin64, pad], axis=1)
    ident = jnp.concatenate([jnp.ones((ctx_len, QK_ROPE), F32), jnp.zeros((ctx_len, LANE - QK_ROPE), F32)], axis=1)
    return jnp.concatenate([cos_t, ident], axis=0), jnp.concatenate([sin_t, jnp.zeros((ctx_len, LANE), F32)], axis=0)


def _prep_w_in(w):
    d = w.shape[0]
    pw = lw = d
    o = np.cumsum([0, pw, lw, lw, Q_LORA, KV_LORA, QK_ROPE, 3 * d])
    pool, lx, lg, cq, ckv, kr, gt = (w[:, o[i]:o[i + 1]] for i in range(7))
    cols = [pool, lx, lg, gt, cq, ckv, kr, kr[:, _swap_perm()]]
    n = sum(c.shape[1] for c in cols)
    cols.append(jnp.zeros((d, -n % INPROJ_TN), w.dtype))
    return jnp.concatenate(cols, axis=1).astype(BF)


def _prep_w_uq(w):
    qk = QK_NOPE + QK_ROPE
    w = w.reshape(w.shape[0], HEADS, qk)
    rope = w[:, :, QK_NOPE:]
    return jnp.concatenate([w, rope[:, :, _swap_perm()]], axis=2).reshape(w.shape[0], HEADS * HEAD_W).T.astype(BF)


def kernel(x, c, ctx, c_ctx, mod_w, mod_b, pre_mix_g, post_mix_g, pre_ffn_g, post_ffn_g, w_in, pool_w, pool_scale,
           pool_proj, conv_w, conv_b, gate_a_w, gate_a_b, gate_x_w, gate_x_b, lru_lambda, lru_proj, q_norm_g, w_uq,
           kv_norm_g, w_ukv, mla_proj, w_out, ffn_w1, ffn_w3, ffn_w2, router_w, moe_w1, moe_w3, moe_w2):
    bsz, seq, d = x.shape
    ctx_len = ctx.shape[1]
    depth = mod_w.shape[0]
    assert ctx_len == SEQ_TILE and seq % SEQ_TILE == 0 and seq % GRID_W == 0
    n_lat = bsz * seq
    n_ctx = bsz * ctx_len
    n_all = n_lat + n_ctx
    tm = min(1024, seq, n_ctx)
    assert seq % tm == 0 and n_ctx % tm == 0
    tm_merge = tm // 2

    def mod_idx_for(rows):
        return lambda i: jnp.where(i < n_lat // rows, i // (seq // rows), bsz)

    mod_idx = mod_idx_for(tm)

    assert bsz + 1 <= SUBLANE
    cc = jnp.concatenate([c, c_ctx[None, :], jnp.zeros((SUBLANE - bsz - 1, d), F32)], axis=0)
    mods = _mod_call(cc, mod_w, mod_b)[:, :bsz + 1].reshape(depth, bsz + 1, 6, d)
    cos_t, sin_t = _rope_tables(seq, ctx_len)
    x_all, x_ctx = x.reshape(n_lat, d), ctx.reshape(n_ctx, d)
    row1 = lambda v: v.reshape(1, -1)

    for l in range(depth):
        last = l == depth - 1
        n_out = n_lat if last else n_all
        mod = mods[l]
        z = _inproj_call(x_all, x_ctx, mod, row1(pre_mix_g[l]), _prep_w_in(w_in[l]), tm, mod_idx)
        wkv = w_ukv[l].reshape(KV_LORA, HEADS, QK_NOPE + V_DIM)
        wk = wkv[:, :, :QK_NOPE].reshape(KV_LORA, HEADS * QK_NOPE).astype(BF)
        wvt = wkv[:, :, QK_NOPE:].reshape(KV_LORA, HEADS * V_DIM).T.astype(BF)
        q, k, v = _qkv_call(z, cos_t, sin_t, row1(q_norm_g[l]), row1(kv_norm_g[l]), _prep_w_uq(w_uq[l]),
                            wk, wvt, bsz, seq, ctx_len, 6 * d)
        tq = min(1024, seq)
        o = _attn_call(q, k, v, bsz=bsz, row_blk0=0, nq=seq // tq, tq=tq, kblk=0, klen=seq + ctx_len)
        o_ctx = None
        if not last:
            o_ctx = _attn_call(q, k, v, bsz=bsz, row_blk0=n_lat // ctx_len, nq=1, tq=ctx_len,
                               kblk=seq // ctx_len, klen=ctx_len)
            if x_ctx is None:
                o, o_ctx = jnp.concatenate([o, o_ctx], axis=0), None
        mp = _pool_call(z, pool_w[l].astype(BF), row1(pool_scale[l]), bsz, seq, ctx_len, n_out)
        hs = []
        for dr in range(2):
            hs.append(_lru_call(z, conv_w[l], row1(conv_b[l]), gate_a_w[l, dr].astype(BF), row1(gate_a_b[l, dr]),
                                gate_x_w[l, dr].astype(BF), row1(gate_x_b[l, dr]), row1(lru_lambda[l, dr]),
                                bsz, seq, dr == 1))
        moe_layer = l % 2 == 1
        rw = None
        if moe_layer:
            rw = router_w[l // 2].T
        outs = _merge_call(x_all, x_ctx if o_ctx is not None else None, mod, mp, hs[0], hs[1], z, o, o_ctx,
                           pool_proj[l].astype(BF), lru_proj[l].astype(BF), mla_proj[l].astype(BF),
                           w_out[l].astype(BF), row1(post_mix_g[l]), row1(pre_ffn_g[l]),
                           rw, tm_merge, n_out, mod_idx_for(tm_merge))
        x_ctx = None
        if not moe_layer:
            x1, h2 = outs
            x_all = _ffn_call(h2, x1, mod, row1(post_ffn_g[l]), ffn_w1[l // 2], ffn_w3[l // 2], ffn_w2[l // 2],
                              tm, mod_idx)
        else:
            x1, h2, rt, rtt = outs
            tme = min(1024, n_out * TOP_K // N_EXPERTS)
            dest, src_tok, blk_e, nused = _route_plan(rtt, tme)
            xs = _sc_gather_rows(h2, src_tok)
            y = _moe_call(blk_e, nused, xs, moe_w1[l // 2], moe_w3[l // 2], moe_w2[l // 2], tme)
            yab = _sc_gather_rows(y, dest)
            x_all = _combine_call(yab, rt, x1, mod, row1(post_ffn_g[l]), tm, mod_idx)
    return x_all[:n_lat].reshape(bsz, seq, d)
```
